```python
import jax, jax.numpy as jnp
from jax import lax
import numpy as np

D_MODEL = 1024
BATCH = 16
SEQ = 4096
DEPTH = 1

D_MIX = D_MODEL
ROPE_THETA = 500000.0
EPS = 1e-6
NEG = -1e30
FORCE = 1e4
MLA_HEADS = 8
MLA_Q_LORA = 256
MLA_KV_LORA = 128
MLA_NOPE = 64
MLA_ROPE = 32
MLA_V = 64
MLA_QK = MLA_NOPE + MLA_ROPE
MLA_WIDTH = MLA_HEADS * MLA_V
MLA_QBLOCK = 128
NSA_HEADS = 8
NSA_GROUPS = 2
NSA_REP = NSA_HEADS // NSA_GROUPS
NSA_DK = 64
NSA_DV = 64
NSA_WIDTH = NSA_HEADS * NSA_DV
NSA_ROT = NSA_DK // 4
CMP_LEN = 32
CMP_STRIDE = 16
CMP_HIDDEN = 128
SEL_LEN = 64
N_SEL = 16
WIN = 512
NSA_QBLOCK = 32
MEM_TOKENS = 256
XA_HEADS = 4
XA_DH = 128
PEER_HEADS = 8
PEER_KEYS = 128
PEER_EXPERTS = PEER_KEYS * PEER_KEYS
PEER_QDIM = 128
PEER_HALF = PEER_QDIM // 2
PEER_TOPK = 16
PEER_CHUNK = 128

IN_SIZES = (MLA_Q_LORA, MLA_KV_LORA, MLA_ROPE,
            NSA_HEADS * NSA_DK,
            NSA_GROUPS * NSA_DK, NSA_GROUPS * NSA_DV,
            NSA_GROUPS * NSA_DK, NSA_GROUPS * NSA_DV,
            NSA_GROUPS * NSA_DK, NSA_GROUPS * NSA_DV,
            NSA_HEADS * 3)
IN_COLS = sum(IN_SIZES)

kernel_name = "hybrid_mla_nsa_memxattn_peer"


def rms_norm(x, g):
    xf = x.astype(jnp.float32)
    y = xf * lax.rsqrt(jnp.mean(xf * xf, axis=-1, keepdims=True) + EPS)
    return (y * g.astype(jnp.float32)).astype(x.dtype)


def rope_tables(pos, rot_dim):
    inv = 1.0 / (ROPE_THETA ** (jnp.arange(0, rot_dim, 2, dtype=jnp.float32) / rot_dim))
    ang = pos.astype(jnp.float32)[:, None] * inv[None, :]
    return jnp.cos(ang), jnp.sin(ang)


def rotate(x, cos, sin):
    x1, x2 = jnp.split(x.astype(jnp.float32), 2, axis=-1)
    c = cos[None, :, None, :]
    s = sin[None, :, None, :]
    return jnp.concatenate([x1 * c - x2 * s, x2 * c + x1 * s], axis=-1).astype(x.dtype)


def partial_rope(x, cos, sin):
    r = 2 * cos.shape[-1]
    return jnp.concatenate([rotate(x[..., :r], cos, sin), x[..., r:]], axis=-1)


def masked_softmax(s, mask):
    p = jax.nn.softmax(jnp.where(mask, s.astype(jnp.float32), NEG), axis=-1)
    return p * mask


def mla_mixer(c_q, c_kv, k_pe, q_norm, kv_norm, w_uq, w_ukv, qk_norm):
    B, S, _ = c_q.shape
    cos, sin = rope_tables(jnp.arange(S), MLA_ROPE)
    q = (rms_norm(c_q, q_norm) @ w_uq).reshape(B, S, MLA_HEADS, MLA_QK)
    kv = (rms_norm(c_kv, kv_norm) @ w_ukv).reshape(B, S, MLA_HEADS, MLA_NOPE + MLA_V)
    k_nope, v = kv[..., :MLA_NOPE], kv[..., MLA_NOPE:]
    k = jnp.concatenate([k_nope, jnp.broadcast_to(k_pe[:, :, None, :], (B, S, MLA_HEADS, MLA_ROPE))], axis=-1)
    q = rms_norm(q, qk_norm[0])
    k = rms_norm(k, qk_norm[1])
    q = jnp.concatenate([q[..., :MLA_NOPE], rotate(q[..., MLA_NOPE:], cos, sin)], axis=-1)
    k = jnp.concatenate([k[..., :MLA_NOPE], rotate(k[..., MLA_NOPE:], cos, sin)], axis=-1)
    scale = MLA_QK ** -0.5
    kpos = jnp.arange(S)

    def block(i):
        s0 = i * MLA_QBLOCK
        qb = lax.dynamic_slice_in_dim(q, s0, MLA_QBLOCK, axis=1)
        sc = jnp.einsum('bqhd,bkhd->bhqk', qb, k).astype(jnp.float32) * scale
        mask = kpos[None, :] <= (s0 + jnp.arange(MLA_QBLOCK))[:, None]
        p = masked_softmax(sc, mask)
        return jnp.einsum('bhqk,bkhd->bqhd', p.astype(v.dtype), v)

    o = lax.map(block, jnp.arange(S // MLA_QBLOCK))
    return o.transpose(1, 0, 2, 3, 4).reshape(B, S, MLA_WIDTH)


def compress(t, pe, w1, w2):
    B, S, G, d = t.shape
    n_cmp = (S - CMP_LEN) // CMP_STRIDE + 1
    idx = jnp.arange(n_cmp)[:, None] * CMP_STRIDE + jnp.arange(CMP_LEN)[None, :]
    blk = t[:, idx] + pe[None, None, :, None, :]
    blk = blk.transpose(0, 1, 3, 2, 4).reshape(B, n_cmp, G, CMP_LEN * d)
    return jax.nn.gelu(blk @ w1) @ w2


def nsa_mixer(q_in, kc_in, vc_in, ks_in, vs_in, kw_in, vw_in, gate_in, q_norm, k_norm, cmp_pe, cmp_w1, cmp_w2):
    B, S, _ = q_in.shape
    G, R = NSA_GROUPS, NSA_REP
    n_cmp = (S - CMP_LEN) // CMP_STRIDE + 1
    n_blk = S // SEL_LEN
    n_sel = min(N_SEL, n_blk)
    cos, sin = rope_tables(jnp.arange(S), NSA_ROT)
    cmp_end = jnp.arange(n_cmp) * CMP_STRIDE + CMP_LEN - 1
    cos_c, sin_c = rope_tables(cmp_end, NSA_ROT)

    def grp(t, d):
        return t.reshape(B, S, G, d)

    q = partial_rope(rms_norm(q_in.reshape(B, S, NSA_HEADS, NSA_DK), q_norm), cos, sin).reshape(B, S, G, R, NSA_DK)
    kc = compress(grp(kc_in, NSA_DK), cmp_pe[0], cmp_w1[0], cmp_w2[0])
    vc = compress(grp(vc_in, NSA_DV), cmp_pe[1], cmp_w1[1], cmp_w2[1])
    kc = partial_rope(rms_norm(kc, k_norm[0]), cos_c, sin_c)
    ks = partial_rope(rms_norm(grp(ks_in, NSA_DK), k_norm[1]), cos, sin)
    vs = grp(vs_in, NSA_DV)
    kw = partial_rope(rms_norm(grp(kw_in, NSA_DK), k_norm[2]), cos, sin)
    vw = grp(vw_in, NSA_DV)
    gates = jax.nn.sigmoid(gate_in.astype(jnp.float32)).reshape(B, S, G, R, 3).astype(q_in.dtype)

    ks_blk = ks.reshape(B, n_blk, SEL_LEN, G, NSA_DK).transpose(0, 3, 1, 2, 4).reshape(B, G, n_blk, SEL_LEN * NSA_DK)
    vs_blk = vs.reshape(B, n_blk, SEL_LEN, G, NSA_DV).transpose(0, 3, 1, 2, 4).reshape(B, G, n_blk, SEL_LEN * NSA_DV)
    cmp_start = jnp.arange(n_cmp) * CMP_STRIDE
    blk_ids = jnp.arange(n_blk)
    blk_start = blk_ids * SEL_LEN
    overlap = ((cmp_start[:, None] < blk_start[None, :] + SEL_LEN)
               & (cmp_start[:, None] + CMP_LEN > blk_start[None, :])).astype(jnp.float32)
    kw_pad = jnp.pad(kw, ((0, 0), (WIN, 0), (0, 0), (0, 0)))
    vw_pad = jnp.pad(vw, ((0, 0), (WIN, 0), (0, 0), (0, 0)))
    scale = NSA_DK ** -0.5
    QB = NSA_QBLOCK

    def block(i):
        s0 = i * QB
        qpos = s0 + jnp.arange(QB)
        qb = lax.dynamic_slice_in_dim(q, s0, QB, axis=1)
        sc = jnp.einsum('bqgrd,bngd->bgrqn', qb, kc).astype(jnp.float32) * scale
        p_c = masked_softmax(sc, cmp_end[None, :] <= qpos[:, None])
        o_c = jnp.einsum('bgrqn,bngd->bqgrd', p_c.astype(vc.dtype), vc)
        imp = jnp.einsum('bgrqn,nm->bgqm', p_c, overlap)
        cur = qpos // SEL_LEN
        forced = (blk_ids[None, :] == 0) | (blk_ids[None, :] == cur[:, None]) | (blk_ids[None, :] == cur[:, None] - 1)
        score = jnp.where(blk_ids[None, :] <= cur[:, None], jnp.where(forced, FORCE, imp), NEG)
        top_val, top_idx = lax.top_k(score, n_sel)
        flat = top_idx.reshape(B, G, QB * n_sel, 1)
        ks_g = jnp.take_along_axis(ks_blk, flat, axis=2).reshape(B, G, QB, n_sel * SEL_LEN, NSA_DK)
        vs_g = jnp.take_along_axis(vs_blk, flat, axis=2).reshape(B, G, QB, n_sel * SEL_LEN, NSA_DV)
        tok = (top_idx[..., None] * SEL_LEN + jnp.arange(SEL_LEN)).reshape(B, G, QB, n_sel * SEL_LEN)
        ok = jnp.repeat(top_val > 0.5 * NEG, SEL_LEN, axis=-1)
        smask = ok & (tok <= qpos[None, None, :, None])
        ss = jnp.einsum('bqgrd,bgqkd->bgrqk', qb, ks_g).astype(jnp.float32) * scale
        p_s = masked_softmax(ss, smask[:, :, None])
        o_s = jnp.einsum('bgrqk,bgqkd->bqgrd', p_s.astype(vs_g.dtype), vs_g)
        kwb = lax.dynamic_slice_in_dim(kw_pad, s0, WIN + QB, axis=1)
        vwb = lax.dynamic_slice_in_dim(vw_pad, s0, WIN + QB, axis=1)
        kpos = s0 - WIN + jnp.arange(WIN + QB)
        dist = qpos[:, None] - kpos[None, :]
        wmask = (kpos[None, :] >= 0) & (dist >= 0) & (dist < WIN)
        sw = jnp.einsum('bqgrd,bkgd->bgrqk', qb, kwb).astype(jnp.float32) * scale
        p_w = masked_softmax(sw, wmask)
        o_w = jnp.einsum('bgrqk,bkgd->bqgrd', p_w.astype(vwb.dtype), vwb)
        gb = lax.dynamic_slice_in_dim(gates, s0, QB, axis=1)
        return gb[..., 0:1] * o_c + gb[..., 1:2] * o_s + gb[..., 2:3] * o_w

    o = lax.map(block, jnp.arange(S // QB))
    return o.transpose(1, 0, 2, 3, 4, 5).reshape(B, S, NSA_WIDTH)


def memory_cross_attention(hn, mn, wq, wkv, qk_norm, wo):
    B, S, _ = hn.shape
    M = mn.shape[1]
    q = rms_norm((hn @ wq).reshape(B, S, XA_HEADS, XA_DH), qk_norm[0])
    kv = (mn @ wkv).reshape(B, M, 2, XA_HEADS, XA_DH)
    k = rms_norm(kv[:, :, 0], qk_norm[1])
    v = kv[:, :, 1]
    s = jnp.einsum('bqhd,bkhd->bhqk', q, k).astype(jnp.float32) * (XA_DH ** -0.5)
    p = jax.nn.softmax(s, axis=-1)
    o = jnp.einsum('bhqk,bkhd->bqhd', p.astype(v.dtype), v).reshape(B, S, XA_HEADS * XA_DH)
    return o @ wo


def peer_ffn(hn, wq, sub_keys, u, v):
    B, S, D = hn.shape
    xt = hn.reshape(-1, PEER_CHUNK, D)

    def chunk(xc):
        q = (xc @ wq).reshape(PEER_CHUNK, PEER_HEADS, 2, PEER_HALF)
        s = jnp.einsum('thpd,hpkd->thpk', q, sub_keys).astype(jnp.float32)
        s1, i1 = lax.top_k(s[:, :, 0], PEER_TOPK)
        s2, i2 = lax.top_k(s[:, :, 1], PEER_TOPK)
        cand = (s1[..., :, None] + s2[..., None, :]).reshape(PEER_CHUNK, PEER_HEADS, PEER_TOPK * PEER_TOPK)
        cidx = (i1[..., :, None] * PEER_KEYS + i2[..., None, :]).reshape(PEER_CHUNK, PEER_HEADS, PEER_TOPK * PEER_TOPK)
        top, pos = lax.top_k(cand, PEER_TOPK)
        eidx = jnp.take_along_axis(cidx, pos, axis=-1)
        g = jax.nn.softmax(top, axis=-1)
        u_e = jnp.take(u, eidx, axis=0)
        act = jax.nn.gelu(jnp.einsum('thkd,td->thk', u_e, xc).astype(jnp.float32))
        v_e = jnp.take(v, eidx, axis=0)
        return jnp.einsum('thk,thkd->td', (g * act).astype(v.dtype), v_e)

    return lax.map(chunk, xt).reshape(B, S, D)


def setup_inputs(seed: int = 0) -> dict:
    key = jax.random.key(seed)
    ks = jax.random.split(key, 32)
    f32 = jnp.float32
    L = DEPTH

    def nrm(k, shape, scale):
        return jax.random.normal(k, shape, f32) * scale

    def gain(k, shape):
        return 1.0 + 0.02 * jax.random.normal(k, shape, f32)

    return {
        "x": nrm(ks[0], (BATCH, SEQ, D_MODEL), 1.0),
        "mem": nrm(ks[1], (BATCH, MEM_TOKENS, D_MODEL), 1.0),
        "mix_norm": gain(ks[2], (L, D_MODEL)),
        "w_in": nrm(ks[3], (L, D_MODEL, IN_COLS), D_MODEL ** -0.5),
        "mla_q_norm": gain(ks[4], (L, MLA_Q_LORA)),
        "mla_kv_norm": gain(ks[5], (L, MLA_KV_LORA)),
        "mla_w_uq": nrm(ks[6], (L, MLA_Q_LORA, MLA_HEADS * MLA_QK), MLA_Q_LORA ** -0.5),
        "mla_w_ukv": nrm(ks[7], (L, MLA_KV_LORA, MLA_HEADS * (MLA_NOPE + MLA_V)), MLA_KV_LORA ** -0.5),
        "mla_qk_norm": gain(ks[8], (L, 2, MLA_QK)),
        "nsa_q_norm": gain(ks[9], (L, NSA_DK)),
        "nsa_k_norm": gain(ks[10], (L, 3, NSA_DK)),
        "nsa_cmp_pe": nrm(ks[11], (L, 2, CMP_LEN, NSA_DK), 0.1),
        "nsa_cmp_w1": nrm(ks[12], (L, 2, CMP_LEN * NSA_DK, CMP_HIDDEN), (CMP_LEN * NSA_DK) ** -0.5),
        "nsa_cmp_w2": nrm(ks[13], (L, 2, CMP_HIDDEN, NSA_DK), CMP_HIDDEN ** -0.5),
        "mix_out_norm": gain(ks[14], (L, D_MIX)),
        "w_out": nrm(ks[15], (L, D_MIX, D_MODEL), D_MIX ** -0.5),
        "xa_norm": gain(ks[16], (L, D_MODEL)),
        "mem_norm": gain(ks[17], (L, D_MODEL)),
        "xa_wq": nrm(ks[18], (L, D_MODEL, XA_HEADS * XA_DH), D_MODEL ** -0.5),
        "xa_wkv": nrm(ks[19], (L, D_MODEL, 2 * XA_HEADS * XA_DH), D_MODEL ** -0.5),
        "xa_qk_norm": gain(ks[20], (L, 2, XA_DH)),
        "xa_wo": nrm(ks[21], (L, XA_HEADS * XA_DH, D_MODEL), (XA_HEADS * XA_DH) ** -0.5),
        "ffn_norm": gain(ks[22], (L, D_MODEL)),
        "peer_wq": nrm(ks[23], (L, D_MODEL, PEER_HEADS * PEER_QDIM), D_MODEL ** -0.5),
        "peer_keys": nrm(ks[24], (L, PEER_HEADS, 2, PEER_KEYS, PEER_HALF), PEER_HALF ** -0.5),
        "peer_u": nrm(ks[25], (L, PEER_EXPERTS, D_MODEL), D_MODEL ** -0.5),
        "peer_v": nrm(ks[26], (L, PEER_EXPERTS, D_MODEL), PEER_HEADS ** -0.5),
    }


def reference(x, mem, mix_norm, w_in, mla_q_norm, mla_kv_norm, mla_w_uq, mla_w_ukv, mla_qk_norm,
              nsa_q_norm, nsa_k_norm, nsa_cmp_pe, nsa_cmp_w1, nsa_cmp_w2, mix_out_norm, w_out,
              xa_norm, mem_norm, xa_wq, xa_wkv, xa_qk_norm, xa_wo,
              ffn_norm, peer_wq, peer_keys, peer_u, peer_v):
    offsets = [int(o) for o in np.cumsum(IN_SIZES)[:-1]]
    h = x
    for l in range(DEPTH):
        n = rms_norm(h, mix_norm[l])
        (c_q, c_kv, k_pe, q_nsa, kc_in, vc_in, ks_in, vs_in,
         kw_in, vw_in, gate_in) = jnp.split(n @ w_in[l], offsets, axis=-1)
        o_mla = mla_mixer(c_q, c_kv, k_pe, mla_q_norm[l], mla_kv_norm[l], mla_w_uq[l], mla_w_ukv[l], mla_qk_norm[l])
        o_nsa = nsa_mixer(q_nsa, kc_in, vc_in, ks_in, vs_in, kw_in, vw_in, gate_in,
                          nsa_q_norm[l], nsa_k_norm[l], nsa_cmp_pe[l], nsa_cmp_w1[l], nsa_cmp_w2[l])
        g_out = mix_out_norm[l]
        mixed = jnp.concatenate([rms_norm(o_mla, g_out[:MLA_WIDTH]), rms_norm(o_nsa, g_out[MLA_WIDTH:])], axis=-1)
        h = h + mixed @ w_out[l]
        h = h + memory_cross_attention(rms_norm(h, xa_norm[l]), rms_norm(mem, mem_norm[l]),
                                       xa_wq[l], xa_wkv[l], xa_qk_norm[l], xa_wo[l])
        h = h + peer_ffn(rms_norm(h, ffn_norm[l]), peer_wq[l], peer_keys[l], peer_u[l], peer_v[l])
    return h
```

```python
import functools

import jax
import jax.numpy as jnp
from jax import lax
from jax.experimental import pallas as pl
from jax.experimental.pallas import tpu as pltpu

F32 = jnp.float32
BF16 = jnp.bfloat16
I32 = jnp.int32

LANES = 128
VMEM_LIMIT = 56 * 1024 * 1024

ROPE_THETA = 500000.0
EPS = 1e-6
NEG = -1e30
FORCE = 1e4

MLA_HEADS = 8
MLA_Q_LORA = 256
MLA_KV_LORA = 128
MLA_NOPE = 64
MLA_ROPE = 32
MLA_V = 64
MLA_QK = MLA_NOPE + MLA_ROPE

NSA_HEADS = 8
NSA_GROUPS = 2
NSA_REP = NSA_HEADS // NSA_GROUPS
NSA_DK = 64
NSA_ROT = NSA_DK // 4
CMP_LEN = 32
CMP_STRIDE = 16
SEL_LEN = 64
SEL_SHIFT = SEL_LEN.bit_length() - 1
SEL_BLOCKS_PAD = 64
N_SEL = 16
WIN = 512

XA_HEADS = 4
XA_DH = 128

PEER_HEADS = 8
PEER_KEYS = 128
PEER_HALF = 64
PEER_TOPK = 16
PEER_PICKS = PEER_HEADS * PEER_TOPK
PEER_TOK = 128
PEER_WORD_ROWS = 4
PEER_CHUNK_STRIDE = 136

COL_QN = 0
COL_KS = 1024
COL_KW = 1280
COL_CQ = 1536
COL_CKV = 1792
COL_KPE = 1920
COL_VS = 2048
COL_VW = 2304
COL_KC = 2560
COL_GATE = 2688
COL_VC = 3072
IN_COLS_PAD = 3200


def _cparams(*sem):
    return pltpu.CompilerParams(dimension_semantics=sem, vmem_limit_bytes=VMEM_LIMIT)


def _rms(x, g, n):
    ms = jnp.sum(x * x, axis=-1, keepdims=True) * (1.0 / n)
    return x * lax.rsqrt(ms + EPS) * g


def _rope(x, c, sa, sb, half):
    return x * c + pltpu.roll(x, LANES - half, 1) * sa + pltpu.roll(x, half, 1) * sb


def _gelu(x):
    return 0.5 * x * (1.0 + jnp.tanh(0.7978845608028654 * (x + 0.044715 * (x * x * x))))


def _dot_nt(a, b, **kw):
    return lax.dot_general(a, b, (((1,), (1,)), ((), ())), preferred_element_type=F32, **kw)


def _norm_matmul_kernel(x_ref, g_ref, w_ref, o_ref):
    x = x_ref[...]
    n = _rms(x, g_ref[...], x.shape[-1])
    o_ref[...] = jnp.dot(n.astype(BF16), w_ref[...], preferred_element_type=F32)


def _norm_matmul(x2d, g, w, tm):
    t, d = x2d.shape
    n = w.shape[1]
    return pl.pallas_call(
        _norm_matmul_kernel,
        grid=(t // tm,),
        in_specs=[pl.BlockSpec((tm, d), lambda i: (i, 0)),
                  pl.BlockSpec((1, d), lambda i: (0, 0)),
                  pl.BlockSpec((d, n), lambda i: (0, 0))],
        out_specs=pl.BlockSpec((tm, n), lambda i: (i, 0)),
        out_shape=jax.ShapeDtypeStruct((t, n), F32),
        compiler_params=_cparams("parallel"),
    )(x2d, g.reshape(1, d), w)


def _mla_prep_kernel(y_ref, gq_ref, gkv_ref, gqk_ref, wuq_ref, wuk_ref, wuv_ref,
                     c_ref, sa_ref, sb_ref, q_ref, k_ref, v_ref):
    y = y_ref[0]
    cq = y[:, 0:MLA_Q_LORA]
    ckv = y[:, MLA_Q_LORA:MLA_Q_LORA + MLA_KV_LORA]
    kpe = y[:, MLA_Q_LORA + MLA_KV_LORA:]
    nq = _rms(cq, gq_ref[...], MLA_Q_LORA).astype(BF16)
    nkv = _rms(ckv, gkv_ref[...], MLA_KV_LORA).astype(BF16)
    qf = jnp.dot(nq, wuq_ref[...], preferred_element_type=F32)
    kf = jnp.dot(nkv, wuk_ref[...], preferred_element_type=F32)
    vf = jnp.dot(nkv, wuv_ref[...], preferred_element_type=F32)
    c, sa, sb = c_ref[...], sa_ref[...], sb_ref[...]
    gq = gqk_ref[0:1, :]
    gk = gqk_ref[1:2, :]
    scale = MLA_QK ** -0.5
    for h in range(MLA_HEADS):
        sl = slice(h * LANES, (h + 1) * LANES)
        qh = _rope(_rms(qf[:, sl], gq, MLA_QK), c, sa, sb, MLA_ROPE // 2)
        kh = _rope(_rms(kf[:, sl] + kpe, gk, MLA_QK), c, sa, sb, MLA_ROPE // 2)
        q_ref[0, h] = (qh * scale).astype(BF16)
        k_ref[0, h] = kh.astype(BF16)
        v_ref[0, h] = vf[:, sl].astype(BF16)


def _mla_prep(y3, gq, gkv, gqk, wuq, wuk, wuv, tabs, tm):
    b, s, _ = y3.shape
    hs = jax.ShapeDtypeStruct((b, MLA_HEADS, s, LANES), BF16)
    full = lambda shape: pl.BlockSpec(shape, lambda bi, i: (0,) * len(shape))
    tab = pl.BlockSpec((tm, LANES), lambda bi, i: (i, 0))
    hspec = pl.BlockSpec((1, MLA_HEADS, tm, LANES), lambda bi, i: (bi, 0, i, 0))
    return pl.pallas_call(
        _mla_prep_kernel,
        grid=(b, s // tm),
        in_specs=[pl.BlockSpec((1, tm, 512), lambda bi, i: (bi, i, COL_CQ // 512)),
                  full((1, MLA_Q_LORA)), full((1, MLA_KV_LORA)), full((2, LANES)),
                  full((MLA_Q_LORA, MLA_HEADS * LANES)), full((MLA_KV_LORA, MLA_HEADS * LANES)),
                  full((MLA_KV_LORA, MLA_HEADS * LANES)), tab, tab, tab],
        out_specs=[hspec, hspec, hspec],
        out_shape=[hs, hs, hs],
        compiler_params=_cparams("parallel", "parallel"),
    )(y3, gq, gkv, gqk, wuq, wuk, wuv, *tabs)


def _nsa_prep_kernel(yq_ref, yv_ref, gq_ref, gk_ref, c_ref, sa_ref, sb_ref,
                     q_ref, ks_ref, kw_ref, vs_ref, vw_ref):
    c, sa, sb = c_ref[...], sa_ref[...], sb_ref[...]
    half = NSA_ROT // 2
    scale = NSA_DK ** -0.5
    yq = yq_ref[0]
    for h in range(NSA_HEADS):
        qh = _rms(yq[:, h * LANES:(h + 1) * LANES], gq_ref[...], NSA_DK)
        q_ref[0, h] = (_rope(qh, c, sa, sb, half) * scale).astype(BF16)
    for g in range(NSA_GROUPS):
        ks = yq[:, COL_KS + g * LANES:COL_KS + (g + 1) * LANES]
        kw = yq[:, COL_KW + g * LANES:COL_KW + (g + 1) * LANES]
        ks_ref[0, g] = _rope(_rms(ks, gk_ref[0:1, :], NSA_DK), c, sa, sb, half).astype(BF16)
        kw_ref[0, g] = _rope(_rms(kw, gk_ref[1:2, :], NSA_DK), c, sa, sb, half).astype(BF16)
    yv = yv_ref[0]
    for g in range(NSA_GROUPS):
        vs_ref[0, g] = yv[:, g * LANES:(g + 1) * LANES].astype(BF16)
        vw_ref[0, g] = yv[:, (NSA_GROUPS + g) * LANES:(NSA_GROUPS + g + 1) * LANES].astype(BF16)


def _nsa_prep(y3, gq, gk, tabs, tm):
    b, s, _ = y3.shape
    full = lambda shape: pl.BlockSpec(shape, lambda bi, i: (0,) * len(shape))
    tab = pl.BlockSpec((tm, LANES), lambda bi, i: (i, 0))
    qs = jax.ShapeDtypeStruct((b, NSA_HEADS, s, LANES), BF16)
    gs = jax.ShapeDtypeStruct((b, NSA_GROUPS, s, LANES), BF16)
    gspec = pl.BlockSpec((1, NSA_GROUPS, tm, LANES), lambda bi, i: (bi, 0, i, 0))
    return pl.pallas_call(
        _nsa_prep_kernel,
        grid=(b, s // tm),
        in_specs=[pl.BlockSpec((1, tm, 1536), lambda bi, i: (bi, i, 0)),
                  pl.BlockSpec((1, tm, 512), lambda bi, i: (bi, i, COL_VS // 512)),
                  full((1, LANES)), full((2, LANES)), tab, tab, tab],
        out_specs=[pl.BlockSpec((1, NSA_HEADS, tm, LANES), lambda bi, i: (bi, 0, i, 0)),
                   gspec, gspec, gspec, gspec],
        out_shape=[qs, gs, gs, gs, gs],
        compiler_params=_cparams("parallel", "parallel"),
    )(y3, y3, gq, gk, *tabs)


def _compress_kernel(tk_ref, tv_ref, pe_ref, w1_ref, w2_ref, gk_ref, c_ref, sa_ref, sb_ref,
                     kc_ref, vc_ref):
    nrow = tk_ref.shape[2]

    def mlp(t, kv):
        ta = (t + pe_ref[kv, 0:1, :]).astype(BF16)
        tb = (t + pe_ref[kv, 1:2, :]).astype(BF16)
        za = jnp.dot(ta, w1_ref[kv, 0], preferred_element_type=F32)
        zb = jnp.dot(tb, w1_ref[kv, 1], preferred_element_type=F32)
        hid = _gelu(za + pltpu.roll(zb, nrow - 1, 0))
        return jnp.dot(hid.astype(BF16), w2_ref[kv], preferred_element_type=F32)

    kc = mlp(tk_ref[0, 0], 0)
    kc = _rope(_rms(kc, gk_ref[...], NSA_DK), c_ref[...], sa_ref[...], sb_ref[...], NSA_ROT // 2)
    kc_ref[0, 0] = kc.astype(BF16)
    vc_ref[0, 0] = mlp(tv_ref[0, 0], 1).astype(BF16)


def _compress(tk16, tv16, pe, w1, w2, gk, tabs):
    b, g, nrow, width = tk16.shape
    full = lambda shape: pl.BlockSpec(shape, lambda bi, gi: (0,) * len(shape))
    tspec = pl.BlockSpec((1, 1, nrow, width), lambda bi, gi: (bi, gi, 0, 0))
    ospec = pl.BlockSpec((1, 1, nrow, LANES), lambda bi, gi: (bi, gi, 0, 0))
    os_ = jax.ShapeDtypeStruct((b, g, nrow, LANES), BF16)
    return pl.pallas_call(
        _compress_kernel,
        grid=(b, g),
        in_specs=[tspec, tspec, full(pe.shape), full(w1.shape), full(w2.shape), full((1, LANES)),
                  full((nrow, LANES)), full((nrow, LANES)), full((nrow, LANES))],
        out_specs=[ospec, ospec],
        out_shape=[os_, os_],
        compiler_params=_cparams("parallel", "parallel"),
    )(tk16, tv16, pe, w1, w2, gk, *tabs)


def _attend(q, k_ref, v_ref, lo, hi, tk, mask_fn, heads, tq):
    def body(j, carry):
        m, l, acc = carry
        start = pl.multiple_of(j * tk, tk)
        k = k_ref[pl.ds(start, tk), :]
        v = v_ref[pl.ds(start, tk), :]
        s = _dot_nt(q, k).reshape(heads, tq, tk)
        msk = mask_fn(j)[None]
        s = jnp.where(msk, s, NEG)
        m_new = jnp.maximum(m, jnp.max(s, axis=-1, keepdims=True))
        p = jnp.where(msk, jnp.exp(s - m_new), 0.0)
        alpha = jnp.exp(m - m_new)
        l = alpha * l + jnp.sum(p, axis=-1, keepdims=True)
        pv = jnp.dot(p.reshape(heads * tq, tk).astype(BF16), v, preferred_element_type=F32)
        acc = alpha.reshape(heads * tq, 1) * acc + pv
        return m_new, l, acc

    m0 = jnp.full((heads, tq, 1), NEG, F32)
    l0 = jnp.zeros((heads, tq, 1), F32)
    a0 = jnp.zeros((heads * tq, LANES), F32)
    _, l, acc = lax.fori_loop(lo, hi, body, (m0, l0, a0))
    inv = jnp.where(l > 0.0, 1.0 / l, 0.0)
    return acc * inv.reshape(heads * tq, 1)


def _mla_attn_kernel(q_ref, k_ref, v_ref, o_ref, *, tq, tk):
    q0 = pl.program_id(2) * tq
    qpos = q0 + lax.broadcasted_iota(I32, (tq, tk), 0)
    kcol = lax.broadcasted_iota(I32, (tq, tk), 1)
    o_ref[0] = _attend(q_ref[0, 0], k_ref.at[0, 0], v_ref.at[0, 0], 0, (q0 + tq - 1) // tk + 1, tk,
                       lambda j: j * tk + kcol <= qpos, 1, tq)


def _mla_attn(q, k, v, tq, tk):
    b, h, s, _ = q.shape
    kv = pl.BlockSpec((1, 1, s, LANES), lambda bi, hi, i: (bi, hi, 0, 0))
    return pl.pallas_call(
        functools.partial(_mla_attn_kernel, tq=tq, tk=tk),
        grid=(b, h, s // tq),
        in_specs=[pl.BlockSpec((1, 1, tq, LANES), lambda bi, hi, i: (bi, hi, i, 0)), kv, kv],
        out_specs=pl.BlockSpec((1, tq, LANES), lambda bi, hi, i: (bi, i, hi)),
        out_shape=jax.ShapeDtypeStruct((b, s, h * LANES), F32),
        compiler_params=_cparams("parallel", "parallel", "arbitrary"),
    )(q, k, v)


def _store_heads(o_ref, o, tq):
    for r in range(NSA_REP):
        o_ref[0, :, r * LANES:(r + 1) * LANES] = o[r * tq:(r + 1) * tq]


def _cmp_attn_kernel(q_ref, kc_ref, vc_ref, o_ref, sel_ref, *, tq):
    ncmp = kc_ref.shape[2]
    q0 = pl.program_id(2) * tq
    q = q_ref[0].reshape(NSA_REP * tq, LANES)
    s = _dot_nt(q, kc_ref[0, 0]).reshape(NSA_REP, tq, ncmp)
    qpos = q0 + lax.broadcasted_iota(I32, (tq, ncmp), 0)
    n = lax.broadcasted_iota(I32, (tq, ncmp), 1)
    msk = (n * CMP_STRIDE + (CMP_LEN - 1) <= qpos)[None]
    s = jnp.where(msk, s, NEG)
    m = jnp.max(s, axis=-1, keepdims=True)
    p = jnp.where(msk, jnp.exp(s - m), 0.0)
    l = jnp.sum(p, axis=-1, keepdims=True)
    p = p * jnp.where(l > 0.0, 1.0 / l, 0.0)
    o = jnp.dot(p.reshape(NSA_REP * tq, ncmp).astype(BF16), vc_ref[0, 0], preferred_element_type=F32)
    _store_heads(o_ref, o, tq)

    nblk = SEL_BLOCKS_PAD
    psum = p[0] + p[1] + p[2] + p[3]
    bm = lax.broadcasted_iota(I32, (nblk, ncmp), 0)
    bn = lax.broadcasted_iota(I32, (nblk, ncmp), 1)
    overlap = ((bn * CMP_STRIDE < bm * SEL_LEN + SEL_LEN)
               & (bn * CMP_STRIDE + CMP_LEN > bm * SEL_LEN)).astype(F32)
    imp = _dot_nt(overlap, psum, precision=lax.Precision.HIGHEST)
    blk = lax.broadcasted_iota(I32, (nblk, tq), 0)
    cur = (q0 + lax.broadcasted_iota(I32, (nblk, tq), 1)) >> SEL_SHIFT
    forced = (blk == 0) | (blk == cur) | (blk == cur - 1)
    score = jnp.where(blk <= cur, jnp.where(forced, FORCE, imp), NEG)
    rank = jnp.zeros((nblk, tq), I32)
    for j in range(nblk):
        row = score[j:j + 1, :]
        ahead = (row > score) | ((row == score) & (blk > j))
        rank = rank + ahead.astype(I32)
    sel = ((rank < N_SEL) & (blk <= cur)).astype(F32)
    sel = jnp.concatenate([sel, jnp.zeros((LANES - nblk, tq), F32)], axis=0)
    sel_ref[0, 0] = sel.T.astype(BF16)


def _cmp_attn(q, kc, vc, tq):
    b, _, s, _ = q.shape
    ncmp = kc.shape[2]
    cspec = pl.BlockSpec((1, 1, ncmp, LANES), lambda bi, gi, i: (bi, gi, 0, 0))
    return pl.pallas_call(
        functools.partial(_cmp_attn_kernel, tq=tq),
        grid=(b, NSA_GROUPS, s // tq),
        in_specs=[pl.BlockSpec((1, NSA_REP, tq, LANES), lambda bi, gi, i: (bi, gi, i, 0)), cspec, cspec],
        out_specs=[pl.BlockSpec((1, tq, NSA_REP * LANES), lambda bi, gi, i: (bi, i, gi)),
                   pl.BlockSpec((1, 1, tq, LANES), lambda bi, gi, i: (bi, gi, i, 0))],
        out_shape=[jax.ShapeDtypeStruct((b, s, NSA_HEADS * LANES), F32),
                   jax.ShapeDtypeStruct((b, NSA_GROUPS, s, LANES), BF16)],
        compiler_params=_cparams("parallel", "parallel", "arbitrary"),
    )(q, kc, vc)


def _sel_attn_kernel(q_ref, k_ref, v_ref, sel_ref, o_ref, *, tq, tk):
    q0 = pl.program_id(2) * tq
    qpos = q0 + lax.broadcasted_iota(I32, (tq, tk), 0)
    kcol = lax.broadcasted_iota(I32, (tq, tk), 1)
    eb = lax.broadcasted_iota(I32, (LANES, tk), 0)
    ec = lax.broadcasted_iota(I32, (LANES, tk), 1)
    sel = sel_ref[0, 0]

    def mask(j):
        expand = (((j * tk + ec) >> SEL_SHIFT) == eb).astype(BF16)
        chosen = jnp.dot(sel, expand, preferred_element_type=F32)
        return (chosen > 0.5) & (j * tk + kcol <= qpos)

    q = q_ref[0].reshape(NSA_REP * tq, LANES)
    o = _attend(q, k_ref.at[0, 0], v_ref.at[0, 0], 0, (q0 + tq - 1) // tk + 1, tk, mask, NSA_REP, tq)
    _store_heads(o_ref, o, tq)


def _win_attn_kernel(q_ref, k_ref, v_ref, o_ref, *, tq, tk):
    q0 = pl.program_id(2) * tq
    qpos = q0 + lax.broadcasted_iota(I32, (tq, tk), 0)
    kcol = lax.broadcasted_iota(I32, (tq, tk), 1)

    def mask(j):
        dist = qpos - (j * tk + kcol)
        return (dist >= 0) & (dist < WIN)

    q = q_ref[0].reshape(NSA_REP * tq, LANES)
    lo = jnp.maximum(q0 - (WIN - 1), 0) // tk
    o = _attend(q, k_ref.at[0, 0], v_ref.at[0, 0], lo, (q0 + tq - 1) // tk + 1, tk, mask, NSA_REP, tq)
    _store_heads(o_ref, o, tq)


def _nsa_branch_attn(kernel, q, k, v, extra, tq, tk):
    b, _, s, _ = q.shape
    kv = pl.BlockSpec((1, 1, s, LANES), lambda bi, gi, i: (bi, gi, 0, 0))
    in_specs = [pl.BlockSpec((1, NSA_REP, tq, LANES), lambda bi, gi, i: (bi, gi, i, 0)), kv, kv]
    in_specs += [pl.BlockSpec((1, 1, tq, LANES), lambda bi, gi, i: (bi, gi, i, 0)) for _ in extra]
    return pl.pallas_call(
        functools.partial(kernel, tq=tq, tk=tk),
        grid=(b, NSA_GROUPS, s // tq),
        in_specs=in_specs,
        out_specs=pl.BlockSpec((1, tq, NSA_REP * LANES), lambda bi, gi, i: (bi, i, gi)),
        out_shape=jax.ShapeDtypeStruct((b, s, NSA_HEADS * LANES), F32),
        compiler_params=_cparams("parallel", "parallel", "arbitrary"),
    )(q, k, v, *extra)


def _mix_out_kernel(x_ref, om_ref, oc_ref, os_ref, ow_ref, gate_ref, gm_ref, gn_ref, w_ref, o_ref):
    gate = jax.nn.sigmoid(gate_ref[0])
    parts = []
    for h in range(NSA_HEADS):
        sl = slice(h * LANES, (h + 1) * LANES)
        g0 = gate[:, h:h + 1]
        g1 = gate[:, LANES + h:LANES + h + 1]
        g2 = gate[:, 2 * LANES + h:2 * LANES + h + 1]
        parts.append(g0 * oc_ref[0, :, sl] + g1 * os_ref[0, :, sl] + g2 * ow_ref[0, :, sl])
    o_nsa = jnp.concatenate(parts, axis=-1)
    width = MLA_HEADS * MLA_V
    mixed = jnp.concatenate([_rms(om_ref[0], gm_ref[...], width), _rms(o_nsa, gn_ref[...], width)], axis=-1)
    o_ref[0] = x_ref[0] + jnp.dot(mixed.astype(BF16), w_ref[...], preferred_element_type=F32)


def _mix_out(x, om, oc, os_, ow, y3, gm, gn, w, tm):
    b, s, d = x.shape
    wide = om.shape[-1]
    full = lambda shape: pl.BlockSpec(shape, lambda bi, i: (0,) * len(shape))
    row = lambda width: pl.BlockSpec((1, tm, width), lambda bi, i: (bi, i, 0))
    return pl.pallas_call(
        _mix_out_kernel,
        grid=(b, s // tm),
        in_specs=[row(d), row(wide), row(wide), row(wide), row(wide),
                  pl.BlockSpec((1, tm, 3 * LANES), lambda bi, i: (bi, i, COL_GATE // (3 * LANES))),
                  full((1, wide)), full((1, wide)), full(w.shape)],
        out_specs=row(d),
        out_shape=jax.ShapeDtypeStruct((b, s, d), F32),
        compiler_params=_cparams("parallel", "parallel"),
    )(x, om, oc, os_, ow, y3, gm, gn, w)


def _mem_kv_kernel(m_ref, g_ref, w_ref, gk_ref, k_ref, v_ref):
    mem = m_ref[0]
    n = _rms(mem, g_ref[...], mem.shape[-1]).astype(BF16)
    kv = jnp.dot(n, w_ref[...], preferred_element_type=F32)
    width = XA_HEADS * XA_DH
    for h in range(XA_HEADS):
        sl = slice(h * XA_DH, (h + 1) * XA_DH)
        k_ref[0, :, sl] = _rms(kv[:, sl], gk_ref[...], XA_DH).astype(BF16)
    v_ref[0] = kv[:, width:].astype(BF16)


def _mem_kv(mem, g, w, gk):
    b, m, d = mem.shape
    width = XA_HEADS * XA_DH
    full = lambda shape: pl.BlockSpec(shape, lambda bi: (0,) * len(shape))
    ospec = pl.BlockSpec((1, m, width), lambda bi: (bi, 0, 0))
    os_ = jax.ShapeDtypeStruct((b, m, width), BF16)
    return pl.pallas_call(
        _mem_kv_kernel,
        grid=(b,),
        in_specs=[pl.BlockSpec((1, m, d), lambda bi: (bi, 0, 0)), full((1, d)), full(w.shape), full((1, XA_DH))],
        out_specs=[ospec, ospec],
        out_shape=[os_, os_],
        compiler_params=_cparams("parallel"),
    )(mem, g, w, gk)


def _xattn_kernel(h_ref, g_ref, wq_ref, gq_ref, k_ref, v_ref, wo_ref, o_ref):
    h = h_ref[0]
    hn = _rms(h, g_ref[...], h.shape[-1]).astype(BF16)
    q = jnp.dot(hn, wq_ref[...], preferred_element_type=F32)
    outs = []
    for hd in range(XA_HEADS):
        sl = slice(hd * XA_DH, (hd + 1) * XA_DH)
        qh = (_rms(q[:, sl], gq_ref[...], XA_DH) * (XA_DH ** -0.5)).astype(BF16)
        s = _dot_nt(qh, k_ref[0, :, sl])
        p = jnp.exp(s - jnp.max(s, axis=-1, keepdims=True))
        p = p * (1.0 / jnp.sum(p, axis=-1, keepdims=True))
        outs.append(jnp.dot(p.astype(BF16), v_ref[0, :, sl], preferred_element_type=F32))
    o = jnp.concatenate(outs, axis=-1).astype(BF16)
    o_ref[0] = h + jnp.dot(o, wo_ref[...], preferred_element_type=F32)


def _xattn(h, g, wq, gq, k, v, wo, tm):
    b, s, d = h.shape
    m, width = k.shape[1], k.shape[2]
    full = lambda shape: pl.BlockSpec(shape, lambda bi, i: (0,) * len(shape))
    kv = pl.BlockSpec((1, m, width), lambda bi, i: (bi, 0, 0))
    row = pl.BlockSpec((1, tm, d), lambda bi, i: (bi, i, 0))
    return pl.pallas_call(
        _xattn_kernel,
        grid=(b, s // tm),
        in_specs=[row, full((1, d)), full(wq.shape), full((1, XA_DH)), kv, kv, full(wo.shape)],
        out_specs=row,
        out_shape=jax.ShapeDtypeStruct((b, s, d), F32),
        compiler_params=_cparams("parallel", "parallel"),
    )(h, g, wq, gq, k, v, wo)


def _peer_score_kernel(h_ref, g_ref, wq_ref, kbd_ref, hn_ref, s_ref):
    h = h_ref[...]
    hn = _rms(h, g_ref[...], h.shape[-1]).astype(BF16)
    hn_ref[...] = hn
    q = jnp.dot(hn, wq_ref[...], preferred_element_type=F32).astype(BF16)
    s_ref[...] = _dot_nt(kbd_ref[...], q)


def _peer_score(h2d, g, wq, kbd, tm):
    t, d = h2d.shape
    nk = kbd.shape[0]
    full = lambda shape: pl.BlockSpec(shape, lambda i: (0,) * len(shape))
    return pl.pallas_call(
        _peer_score_kernel,
        grid=(t // tm,),
        in_specs=[pl.BlockSpec((tm, d), lambda i: (i, 0)), full((1, d)), full(wq.shape), full(kbd.shape)],
        out_specs=[pl.BlockSpec((tm, d), lambda i: (i, 0)), pl.BlockSpec((nk, tm), lambda i: (0, i))],
        out_shape=[jax.ShapeDtypeStruct((t, d), BF16), jax.ShapeDtypeStruct((nk, t), F32)],
        compiler_params=_cparams("parallel"),
    )(h2d, g, wq, kbd)


def _topk_rows(x, k, payload=None):
    n = x.shape[0]
    rows = lax.broadcasted_iota(I32, x.shape, 0)
    vals, idxs = [], []
    for _ in range(k):
        m = jnp.max(x, axis=0, keepdims=True)
        idx = jnp.min(jnp.where(x == m, rows, n), axis=0, keepdims=True)
        hit = rows == idx
        vals.append(m)
        if payload is None:
            idxs.append(idx)
        else:
            idxs.append(jnp.max(jnp.where(hit, payload, -1), axis=0, keepdims=True))
        x = jnp.where(hit, -jnp.inf, x)
    return jnp.concatenate(vals, axis=0), jnp.concatenate(idxs, axis=0)


def _peer_route_kernel(s_ref, e_ref, g_ref):
    tm = s_ref.shape[1]

    def head(h, carry):
        base = pl.multiple_of(h * (2 * PEER_KEYS), 2 * PEER_KEYS)
        s1, i1 = _topk_rows(s_ref[pl.ds(base, PEER_KEYS), :], PEER_TOPK)
        s2, i2 = _topk_rows(s_ref[pl.ds(base + PEER_KEYS, PEER_KEYS), :], PEER_TOPK)
        cand = (s1[:, None, :] + s2[None, :, :]).reshape(PEER_TOPK * PEER_TOPK, tm)
        cidx = (i1[:, None, :] * PEER_KEYS + i2[None, :, :]).reshape(PEER_TOPK * PEER_TOPK, tm)
        top, eidx = _topk_rows(cand, PEER_TOPK, payload=cidx)
        p = jnp.exp(top - top[0:1, :])
        p = p * (1.0 / jnp.sum(p, axis=0, keepdims=True))
        row = pl.multiple_of(h * PEER_TOPK, PEER_TOPK)
        e_ref[pl.ds(row, PEER_TOPK), :] = eidx * PEER_WORD_ROWS
        g_ref[pl.ds(row, PEER_TOPK), :] = p
        return carry

    lax.fori_loop(0, PEER_HEADS, head, 0)


def _peer_route(st, tm):
    nk, t = st.shape
    ospec = pl.BlockSpec((PEER_PICKS, tm), lambda i: (0, i))
    return pl.pallas_call(
        _peer_route_kernel,
        grid=(t // tm,),
        in_specs=[pl.BlockSpec((nk, tm), lambda i: (0, i))],
        out_specs=[ospec, ospec],
        out_shape=[jax.ShapeDtypeStruct((PEER_PICKS, t), I32), jax.ShapeDtypeStruct((PEER_PICKS, t), F32)],
        compiler_params=_cparams("parallel"),
    )(st)


def _gather_rows(idx_ref, tab_ref, tile_ref, j):
    for k in range(PEER_PICKS):
        r = pl.multiple_of(idx_ref[k, j], PEER_WORD_ROWS)
        tile_ref[pl.ds(k, PEER_WORD_ROWS, stride=PEER_CHUNK_STRIDE), :] = tab_ref[pl.ds(r, PEER_WORD_ROWS), :]
    chunks = [pltpu.bitcast(tile_ref[pl.ds(c * PEER_CHUNK_STRIDE, PEER_PICKS), :], BF16)
              for c in range(PEER_WORD_ROWS)]
    return jnp.concatenate(chunks, axis=1)


def _peer_up_kernel(idx_ref, x_ref, g2_ref, tab_ref, w_ref, tile_ref, acc_ref):
    half = x_ref.shape[1] // 2
    x = x_ref[...]
    xcat = jnp.concatenate([x[:, :half], x[:, half:]], axis=0)
    shape = (2 * PEER_PICKS, PEER_TOK)
    even = (lax.broadcasted_iota(I32, shape, 0) & 1) == 0
    lane = lax.broadcasted_iota(I32, shape, 1)
    acc_ref[...] = jnp.zeros(shape, F32)

    def token(j, carry):
        rows = _gather_rows(idx_ref, tab_ref, tile_ref, j)
        r2 = _dot_nt(rows, xcat)
        mine = jnp.where(even, r2[:, :PEER_TOK], r2[:, PEER_TOK:])
        acc_ref[...] += jnp.where(lane == j, mine, 0.0)
        return carry

    lax.fori_loop(0, PEER_TOK, token, 0)
    acc = acc_ref[...]
    act = acc + jnp.where(even, pltpu.roll(acc, 2 * PEER_PICKS - 1, 0), pltpu.roll(acc, 1, 0))
    w_ref[...] = (g2_ref[...] * _gelu(act)).T


def _peer_down_kernel(idx_ref, w_ref, h_ref, tab_ref, o_ref, tile_ref):
    half = h_ref.shape[1] // 2
    shape = (16, 2 * PEER_PICKS)
    parity = (lax.broadcasted_iota(I32, shape, 1) & 1) == lax.broadcasted_iota(I32, shape, 0)

    def token(j, carry):
        rows = _gather_rows(idx_ref, tab_ref, tile_ref, j)
        wrow = jnp.broadcast_to(w_ref[pl.ds(j, 1), :], shape)
        wm = jnp.where(parity, wrow, 0.0).astype(BF16)
        o2 = jnp.dot(wm, rows, preferred_element_type=F32)
        o_ref[pl.ds(j, 1), 0:half] = h_ref[pl.ds(j, 1), 0:half] + o2[0:1, :]
        o_ref[pl.ds(j, 1), half:] = h_ref[pl.ds(j, 1), half:] + o2[1:2, :]
        return carry

    lax.fori_loop(0, PEER_TOK, token, 0)


def _tile_scratch():
    return pltpu.VMEM((PEER_WORD_ROWS * PEER_CHUNK_STRIDE, LANES), I32)


def _table_spec(tab):
    return pl.BlockSpec(tab.shape, lambda c: (0, 0), pipeline_mode=pl.Buffered(1))


def _idx_spec():
    return pl.BlockSpec((PEER_PICKS, PEER_TOK), lambda c: (0, c), memory_space=pltpu.SMEM)


def _peer_up(eidx, hn, g2, tab):
    t, d = hn.shape
    return pl.pallas_call(
        _peer_up_kernel,
        grid=(t // PEER_TOK,),
        in_specs=[_idx_spec(),
                  pl.BlockSpec((PEER_TOK, d), lambda c: (c, 0)),
                  pl.BlockSpec((2 * PEER_PICKS, PEER_TOK), lambda c: (0, c)),
                  _table_spec(tab)],
        out_specs=pl.BlockSpec((PEER_TOK, 2 * PEER_PICKS), lambda c: (c, 0)),
        out_shape=jax.ShapeDtypeStruct((t, 2 * PEER_PICKS), F32),
        scratch_shapes=[_tile_scratch(), pltpu.VMEM((2 * PEER_PICKS, PEER_TOK), F32)],
        compiler_params=_cparams("arbitrary"),
    )(eidx, hn, g2, tab)


def _peer_down(eidx, w, h2d, tab):
    t, d = h2d.shape
    return pl.pallas_call(
        _peer_down_kernel,
        grid=(t // PEER_TOK,),
        in_specs=[_idx_spec(),
                  pl.BlockSpec((PEER_TOK, 2 * PEER_PICKS), lambda c: (c, 0)),
                  pl.BlockSpec((PEER_TOK, d), lambda c: (c, 0)),
                  _table_spec(tab)],
        out_specs=pl.BlockSpec((PEER_TOK, d), lambda c: (c, 0)),
        out_shape=jax.ShapeDtypeStruct((t, d), F32),
        scratch_shapes=[_tile_scratch()],
        compiler_params=_cparams("arbitrary"),
    )(eidx, w, h2d, tab)


def _pad_heads(w, heads, dim, dim_pad=LANES):
    lead = w.shape[:-1]
    w = w.reshape(lead + (heads, dim))
    w = jnp.pad(w, [(0, 0)] * len(lead) + [(0, 0), (0, dim_pad - dim)])
    return w.reshape(lead + (heads * dim_pad,))


def _pad_lanes(v, offset=0, width=LANES):
    return jnp.pad(v, [(0, 0)] * (v.ndim - 1) + [(offset, width - offset - v.shape[-1])])


def _rope_tabs(pos, rot_dim, offset):
    inv = 1.0 / (ROPE_THETA ** (jnp.arange(0, rot_dim, 2, dtype=F32) / rot_dim))
    ang = pos.astype(F32)[:, None] * inv[None, :]
    cos, sin = jnp.cos(ang), jnp.sin(ang)
    zero = jnp.zeros_like(sin)
    c = jnp.pad(jnp.concatenate([cos, cos], -1) - 1.0, ((0, 0), (offset, LANES - offset - rot_dim))) + 1.0
    sa = _pad_lanes(jnp.concatenate([-sin, zero], -1), offset)
    sb = _pad_lanes(jnp.concatenate([zero, sin], -1), offset)
    return c, sa, sb


def _pack_table(tab):
    e, d = tab.shape
    bits = lax.bitcast_convert_type(tab.astype(BF16), jnp.uint16).astype(jnp.uint32)
    word = bits[:, :d // 2] | (bits[:, d // 2:] << 16)
    return lax.bitcast_convert_type(word, I32).reshape(e * PEER_WORD_ROWS, LANES)


def _relayout_w_in(w_in):
    d = w_in.shape[0]
    o = 0
    seg = {}
    for name, size in (("cq", 256), ("ckv", 128), ("kpe", 32), ("qn", 512), ("kc", 128), ("vc", 128),
                       ("ks", 128), ("vs", 128), ("kw", 128), ("vw", 128), ("gate", 24)):
        seg[name] = w_in[:, o:o + size]
        o += size
    gates = [_pad_lanes(seg["gate"][:, j::3]) for j in range(3)]
    cols = [_pad_heads(seg["qn"], NSA_HEADS, NSA_DK),
            _pad_heads(seg["ks"], NSA_GROUPS, NSA_DK), _pad_heads(seg["kw"], NSA_GROUPS, NSA_DK),
            seg["cq"], seg["ckv"], _pad_lanes(seg["kpe"], MLA_NOPE),
            _pad_heads(seg["vs"], NSA_GROUPS, NSA_DK), _pad_heads(seg["vw"], NSA_GROUPS, NSA_DK),
            seg["kc"]] + gates + [seg["vc"]]
    w = jnp.concatenate(cols, axis=1)
    assert w.shape == (d, IN_COLS_PAD)
    return w.astype(BF16)


def _layer(h, mem, p):
    b, s, d = h.shape
    t = b * s
    assert s % 512 == 0 and N_SEL <= s // SEL_LEN <= SEL_BLOCKS_PAD and d == 1024
    pos = jnp.arange(s)

    y = _norm_matmul(h.reshape(t, d), p["mix_norm"], _relayout_w_in(p["w_in"]), 512)
    y3 = y.reshape(b, s, IN_COLS_PAD)

    wuq = _pad_heads(p["mla_w_uq"], MLA_HEADS, MLA_QK).astype(BF16)
    wukv = p["mla_w_ukv"].reshape(MLA_KV_LORA, MLA_HEADS, MLA_NOPE + MLA_V)
    wuk = _pad_heads(wukv[..., :MLA_NOPE].reshape(MLA_KV_LORA, -1), MLA_HEADS, MLA_NOPE).astype(BF16)
    wuv = _pad_heads(wukv[..., MLA_NOPE:].reshape(MLA_KV_LORA, -1), MLA_HEADS, MLA_V).astype(BF16)
    q, k, v = _mla_prep(y3, p["mla_q_norm"][None], p["mla_kv_norm"][None], _pad_lanes(p["mla_qk_norm"]),
                        wuq, wuk, wuv, _rope_tabs(pos, MLA_ROPE, MLA_NOPE), 512)
    o_mla = _mla_attn(q, k, v, 256, 512)

    tabs = _rope_tabs(pos, NSA_ROT, 0)
    qn, ks, kw, vs, vw = _nsa_prep(y3, _pad_lanes(p["nsa_q_norm"][None]), _pad_lanes(p["nsa_k_norm"][1:3]),
                                   tabs, 512)
    nrow = s // CMP_STRIDE

    def blocks16(col):
        tkv = y3[:, :, col:col + NSA_GROUPS * NSA_DK].reshape(b, nrow, CMP_STRIDE, NSA_GROUPS, NSA_DK)
        return tkv.transpose(0, 3, 1, 2, 4).reshape(b, NSA_GROUPS, nrow, CMP_STRIDE * NSA_DK)

    pe = p["nsa_cmp_pe"].reshape(2, 2, CMP_STRIDE * NSA_DK)
    w1 = p["nsa_cmp_w1"].reshape(2, 2, CMP_STRIDE * NSA_DK, -1).astype(BF16)
    w2 = _pad_lanes(p["nsa_cmp_w2"]).astype(BF16)
    cmp_end = jnp.arange(nrow) * CMP_STRIDE + CMP_LEN - 1
    kc, vc = _compress(blocks16(COL_KC), blocks16(COL_VC), pe, w1, w2, _pad_lanes(p["nsa_k_norm"][0:1]),
                       _rope_tabs(cmp_end, NSA_ROT, 0))
    o_cmp, sel = _cmp_attn(qn, kc, vc, 128)
    o_sel = _nsa_branch_attn(_sel_attn_kernel, qn, ks, vs, (sel,), 128, 256)
    o_win = _nsa_branch_attn(_win_attn_kernel, qn, kw, vw, (), 128, 256)

    width = MLA_HEADS * MLA_V
    g_out = p["mix_out_norm"]
    gm = _pad_heads(g_out[None, :width], MLA_HEADS, MLA_V)
    gn = _pad_heads(g_out[None, width:], NSA_HEADS, NSA_DK)
    w_out = p["w_out"].reshape(2 * MLA_HEADS, MLA_V, d)
    w_out = jnp.pad(w_out, ((0, 0), (0, LANES - MLA_V), (0, 0))).reshape(2 * MLA_HEADS * LANES, d).astype(BF16)
    h = _mix_out(h, o_mla, o_cmp, o_sel, o_win, y3, gm, gn, w_out, 256)

    kx, vx = _mem_kv(mem, p["mem_norm"][None], p["xa_wkv"].astype(BF16), p["xa_qk_norm"][1:2])
    h = _xattn(h, p["xa_norm"][None], p["xa_wq"].astype(BF16), p["xa_qk_norm"][0:1], kx, vx,
               p["xa_wo"].astype(BF16), 256)

    keys = p["peer_keys"]
    nhp = PEER_HEADS * 2
    eye = jnp.eye(nhp, dtype=F32)
    kbd = (keys.reshape(nhp, PEER_KEYS, PEER_HALF)[:, :, None, :] * eye[:, None, :, None])
    kbd = kbd.reshape(nhp * PEER_KEYS, nhp * PEER_HALF).astype(BF16)
    h2d = h.reshape(t, d)
    hn, st = _peer_score(h2d, p["ffn_norm"][None], p["peer_wq"].astype(BF16), kbd, 512)
    eidx, gw = _peer_route(st, LANES)
    w = _peer_up(eidx, hn, jnp.repeat(gw, 2, axis=0), _pack_table(p["peer_u"]))
    out = _peer_down(eidx, w, h2d, _pack_table(p["peer_v"]))
    return out.reshape(b, s, d)


def kernel(x, mem, mix_norm, w_in, mla_q_norm, mla_kv_norm, mla_w_uq, mla_w_ukv, mla_qk_norm, nsa_q_norm, nsa_k_norm, nsa_cmp_pe, nsa_cmp_w1, nsa_cmp_w2, mix_out_norm, w_out, xa_norm, mem_norm, xa_wq, xa_wkv, xa_qk_norm, xa_wo, ffn_norm, peer_wq, peer_keys, peer_u, peer_v):
    params = dict(mix_norm=mix_norm, w_in=w_in, mla_q_norm=mla_q_norm, mla_kv_norm=mla_kv_norm,
                  mla_w_uq=mla_w_uq, mla_w_ukv=mla_w_ukv, mla_qk_norm=mla_qk_norm, nsa_q_norm=nsa_q_norm,
                  nsa_k_norm=nsa_k_norm, nsa_cmp_pe=nsa_cmp_pe, nsa_cmp_w1=nsa_cmp_w1, nsa_cmp_w2=nsa_cmp_w2,
                  mix_out_norm=mix_out_norm, w_out=w_out, xa_norm=xa_norm, mem_norm=mem_norm, xa_wq=xa_wq,
                  xa_wkv=xa_wkv, xa_qk_norm=xa_qk_norm, xa_wo=xa_wo, ffn_norm=ffn_norm, peer_wq=peer_wq,
                  peer_keys=peer_keys, peer_u=peer_u, peer_v=peer_v)
    h = x
    for layer in range(w_in.shape[0]):
        h = _layer(h, mem, {name: val[layer] for name, val in params.items()})
    return h
```

```python
import functools

import jax
import jax.numpy as jnp
from jax import lax
from jax.experimental import pallas as pl
from jax.experimental.pallas import tpu as pltpu

F32 = jnp.float32
BF16 = jnp.bfloat16
I32 = jnp.int32

LANES = 128
VMEM_LIMIT = 56 * 1024 * 1024

ROPE_THETA = 500000.0
EPS = 1e-6
NEG = -1e30
FORCE = 1e4

MLA_HEADS = 8
MLA_Q_LORA = 256
MLA_KV_LORA = 128
MLA_NOPE = 64
MLA_ROPE = 32
MLA_V = 64
MLA_QK = MLA_NOPE + MLA_ROPE

NSA_HEADS = 8
NSA_GROUPS = 2
NSA_REP = NSA_HEADS // NSA_GROUPS
NSA_DK = 64
NSA_ROT = NSA_DK // 4
CMP_LEN = 32
CMP_STRIDE = 16
SEL_LEN = 64
SEL_SHIFT = SEL_LEN.bit_length() - 1
SEL_BLOCKS_PAD = 64
N_SEL = 16
WIN = 512

XA_HEADS = 4
XA_DH = 128

PEER_HEADS = 8
PEER_KEYS = 128
PEER_HALF = 64
PEER_TOPK = 16
PEER_PICKS = PEER_HEADS * PEER_TOPK
PEER_TOK = 128
PEER_WORD_ROWS = 4
PEER_CHUNK_STRIDE = 136
PEER_UNROLL = 8

COL_QN = 0
COL_KS = 1024
COL_KW = 1280
COL_CQ = 1536
COL_CKV = 1792
COL_KPE = 1920
COL_VS = 2048
COL_VW = 2304
COL_KC = 2560
COL_GATE = 2688
COL_VC = 3072
IN_COLS_PAD = 3200


def _cparams(*sem):
    return pltpu.CompilerParams(dimension_semantics=sem, vmem_limit_bytes=VMEM_LIMIT)


def _rms(x, g, n):
    ms = jnp.sum(x * x, axis=-1, keepdims=True) * (1.0 / n)
    return x * lax.rsqrt(ms + EPS) * g


def _rope(x, c, sa, sb, half):
    return x * c + pltpu.roll(x, LANES - half, 1) * sa + pltpu.roll(x, half, 1) * sb


def _gelu(x):
    return 0.5 * x * (1.0 + jnp.tanh(0.7978845608028654 * (x + 0.044715 * (x * x * x))))


def _dot_nt(a, b, **kw):
    return lax.dot_general(a, b, (((1,), (1,)), ((), ())), preferred_element_type=F32, **kw)


def _norm_matmul_kernel(x_ref, g_ref, w_ref, o_ref):
    x = x_ref[...]
    n = _rms(x, g_ref[...], x.shape[-1])
    o_ref[...] = jnp.dot(n.astype(BF16), w_ref[...], preferred_element_type=F32)


def _norm_matmul(x2d, g, w, tm):
    t, d = x2d.shape
    n = w.shape[1]
    return pl.pallas_call(
        _norm_matmul_kernel,
        grid=(t // tm,),
        in_specs=[pl.BlockSpec((tm, d), lambda i: (i, 0)),
                  pl.BlockSpec((1, d), lambda i: (0, 0)),
                  pl.BlockSpec((d, n), lambda i: (0, 0))],
        out_specs=pl.BlockSpec((tm, n), lambda i: (i, 0)),
        out_shape=jax.ShapeDtypeStruct((t, n), F32),
        compiler_params=_cparams("parallel"),
    )(x2d, g.reshape(1, d), w)


def _mla_prep_kernel(y_ref, gq_ref, gkv_ref, gqk_ref, wuq_ref, wuk_ref, wuv_ref,
                     c_ref, sa_ref, sb_ref, q_ref, k_ref, v_ref):
    y = y_ref[0]
    cq = y[:, 0:MLA_Q_LORA]
    ckv = y[:, MLA_Q_LORA:MLA_Q_LORA + MLA_KV_LORA]
    kpe = y[:, MLA_Q_LORA + MLA_KV_LORA:]
    nq = _rms(cq, gq_ref[...], MLA_Q_LORA).astype(BF16)
    nkv = _rms(ckv, gkv_ref[...], MLA_KV_LORA).astype(BF16)
    qf = jnp.dot(nq, wuq_ref[...], preferred_element_type=F32)
    kf = jnp.dot(nkv, wuk_ref[...], preferred_element_type=F32)
    vf = jnp.dot(nkv, wuv_ref[...], preferred_element_type=F32)
    c, sa, sb = c_ref[...], sa_ref[...], sb_ref[...]
    gq = gqk_ref[0:1, :]
    gk = gqk_ref[1:2, :]
    scale = MLA_QK ** -0.5
    for h in range(MLA_HEADS):
        sl = slice(h * LANES, (h + 1) * LANES)
        qh = _rope(_rms(qf[:, sl], gq, MLA_QK), c, sa, sb, MLA_ROPE // 2)
        kh = _rope(_rms(kf[:, sl] + kpe, gk, MLA_QK), c, sa, sb, MLA_ROPE // 2)
        q_ref[0, h] = (qh * scale).astype(BF16)
        k_ref[0, h] = kh.astype(BF16)
        v_ref[0, h] = vf[:, sl].astype(BF16)


def _mla_prep(y3, gq, gkv, gqk, wuq, wuk, wuv, tabs, tm):
    b, s, _ = y3.shape
    hs = jax.ShapeDtypeStruct((b, MLA_HEADS, s, LANES), BF16)
    full = lambda shape: pl.BlockSpec(shape, lambda bi, i: (0,) * len(shape))
    tab = pl.BlockSpec((tm, LANES), lambda bi, i: (i, 0))
    hspec = pl.BlockSpec((1, MLA_HEADS, tm, LANES), lambda bi, i: (bi, 0, i, 0))
    return pl.pallas_call(
        _mla_prep_kernel,
        grid=(b, s // tm),
        in_specs=[pl.BlockSpec((1, tm, 512), lambda bi, i: (bi, i, COL_CQ // 512)),
                  full((1, MLA_Q_LORA)), full((1, MLA_KV_LORA)), full((2, LANES)),
                  full((MLA_Q_LORA, MLA_HEADS * LANES)), full((MLA_KV_LORA, MLA_HEADS * LANES)),
                  full((MLA_KV_LORA, MLA_HEADS * LANES)), tab, tab, tab],
        out_specs=[hspec, hspec, hspec],
        out_shape=[hs, hs, hs],
        compiler_params=_cparams("parallel", "parallel"),
    )(y3, gq, gkv, gqk, wuq, wuk, wuv, *tabs)


def _nsa_prep_kernel(yq_ref, yv_ref, gq_ref, gk_ref, c_ref, sa_ref, sb_ref,
                     q_ref, ks_ref, kw_ref, vs_ref, vw_ref):
    c, sa, sb = c_ref[...], sa_ref[...], sb_ref[...]
    half = NSA_ROT // 2
    scale = NSA_DK ** -0.5
    yq = yq_ref[0]
    for h in range(NSA_HEADS):
        qh = _rms(yq[:, h * LANES:(h + 1) * LANES], gq_ref[...], NSA_DK)
        q_ref[0, h] = (_rope(qh, c, sa, sb, half) * scale).astype(BF16)
    for g in range(NSA_GROUPS):
        ks = yq[:, COL_KS + g * LANES:COL_KS + (g + 1) * LANES]
        kw = yq[:, COL_KW + g * LANES:COL_KW + (g + 1) * LANES]
        ks_ref[0, g] = _rope(_rms(ks, gk_ref[0:1, :], NSA_DK), c, sa, sb, half).astype(BF16)
        kw_ref[0, g] = _rope(_rms(kw, gk_ref[1:2, :], NSA_DK), c, sa, sb, half).astype(BF16)
    yv = yv_ref[0]
    for g in range(NSA_GROUPS):
        vs_ref[0, g] = yv[:, g * LANES:(g + 1) * LANES].astype(BF16)
        vw_ref[0, g] = yv[:, (NSA_GROUPS + g) * LANES:(NSA_GROUPS + g + 1) * LANES].astype(BF16)


def _nsa_prep(y3, gq, gk, tabs, tm):
    b, s, _ = y3.shape
    full = lambda shape: pl.BlockSpec(shape, lambda bi, i: (0,) * len(shape))
    tab = pl.BlockSpec((tm, LANES), lambda bi, i: (i, 0))
    qs = jax.ShapeDtypeStruct((b, NSA_HEADS, s, LANES), BF16)
    gs = jax.ShapeDtypeStruct((b, NSA_GROUPS, s, LANES), BF16)
    gspec = pl.BlockSpec((1, NSA_GROUPS, tm, LANES), lambda bi, i: (bi, 0, i, 0))
    return pl.pallas_call(
        _nsa_prep_kernel,
        grid=(b, s // tm),
        in_specs=[pl.BlockSpec((1, tm, 1536), lambda bi, i: (bi, i, 0)),
                  pl.BlockSpec((1, tm, 512), lambda bi, i: (bi, i, COL_VS // 512)),
                  full((1, LANES)), full((2, LANES)), tab, tab, tab],
        out_specs=[pl.BlockSpec((1, NSA_HEADS, tm, LANES), lambda bi, i: (bi, 0, i, 0)),
                   gspec, gspec, gspec, gspec],
        out_shape=[qs, gs, gs, gs, gs],
        compiler_params=_cparams("parallel", "parallel"),
    )(y3, y3, gq, gk, *tabs)


def _compress_kernel(tk_ref, tv_ref, pe_ref, w1_ref, w2_ref, gk_ref, c_ref, sa_ref, sb_ref,
                     kc_ref, vc_ref):
    nrow = tk_ref.shape[2]

    def mlp(t, kv):
        ta = (t + pe_ref[kv, 0:1, :]).astype(BF16)
        tb = (t + pe_ref[kv, 1:2, :]).astype(BF16)
        za = jnp.dot(ta, w1_ref[kv, 0], preferred_element_type=F32)
        zb = jnp.dot(tb, w1_ref[kv, 1], preferred_element_type=F32)
        hid = _gelu(za + pltpu.roll(zb, nrow - 1, 0))
        return jnp.dot(hid.astype(BF16), w2_ref[kv], preferred_element_type=F32)

    kc = mlp(tk_ref[0, 0], 0)
    kc = _rope(_rms(kc, gk_ref[...], NSA_DK), c_ref[...], sa_ref[...], sb_ref[...], NSA_ROT // 2)
    kc_ref[0, 0] = kc.astype(BF16)
    vc_ref[0, 0] = mlp(tv_ref[0, 0], 1).astype(BF16)


def _compress(tk16, tv16, pe, w1, w2, gk, tabs):
    b, g, nrow, width = tk16.shape
    full = lambda shape: pl.BlockSpec(shape, lambda bi, gi: (0,) * len(shape))
    tspec = pl.BlockSpec((1, 1, nrow, width), lambda bi, gi: (bi, gi, 0, 0))
    ospec = pl.BlockSpec((1, 1, nrow, LANES), lambda bi, gi: (bi, gi, 0, 0))
    os_ = jax.ShapeDtypeStruct((b, g, nrow, LANES), BF16)
    return pl.pallas_call(
        _compress_kernel,
        grid=(b, g),
        in_specs=[tspec, tspec, full(pe.shape), full(w1.shape), full(w2.shape), full((1, LANES)),
                  full((nrow, LANES)), full((nrow, LANES)), full((nrow, LANES))],
        out_specs=[ospec, ospec],
        out_shape=[os_, os_],
        compiler_params=_cparams("parallel", "parallel"),
    )(tk16, tv16, pe, w1, w2, gk, *tabs)


def _attend(q, k_ref, v_ref, lo, hi, tk, mask_fn, heads, tq):
    def body(j, carry):
        m, l, acc = carry
        start = pl.multiple_of(j * tk, tk)
        k = k_ref[pl.ds(start, tk), :]
        v = v_ref[pl.ds(start, tk), :]
        s = _dot_nt(q, k).reshape(heads, tq, tk)
        msk = mask_fn(j)[None]
        s = jnp.where(msk, s, NEG)
        m_new = jnp.maximum(m, jnp.max(s, axis=-1, keepdims=True))
        p = jnp.where(msk, jnp.exp(s - m_new), 0.0)
        alpha = jnp.exp(m - m_new)
        l = alpha * l + jnp.sum(p, axis=-1, keepdims=True)
        pv = jnp.dot(p.reshape(heads * tq, tk).astype(BF16), v, preferred_element_type=F32)
        acc = alpha.reshape(heads * tq, 1) * acc + pv
        return m_new, l, acc

    m0 = jnp.full((heads, tq, 1), NEG, F32)
    l0 = jnp.zeros((heads, tq, 1), F32)
    a0 = jnp.zeros((heads * tq, LANES), F32)
    _, l, acc = lax.fori_loop(lo, hi, body, (m0, l0, a0))
    inv = jnp.where(l > 0.0, 1.0 / l, 0.0)
    return acc * inv.reshape(heads * tq, 1)


def _mla_attn_kernel(q_ref, k_ref, v_ref, o_ref, *, tq, tk):
    q0 = pl.program_id(2) * tq
    qpos = q0 + lax.broadcasted_iota(I32, (tq, tk), 0)
    kcol = lax.broadcasted_iota(I32, (tq, tk), 1)
    o_ref[0] = _attend(q_ref[0, 0], k_ref.at[0, 0], v_ref.at[0, 0], 0, (q0 + tq - 1) // tk + 1, tk,
                       lambda j: j * tk + kcol <= qpos, 1, tq)


def _mla_attn(q, k, v, tq, tk):
    b, h, s, _ = q.shape
    kv = pl.BlockSpec((1, 1, s, LANES), lambda bi, hi, i: (bi, hi, 0, 0))
    return pl.pallas_call(
        functools.partial(_mla_attn_kernel, tq=tq, tk=tk),
        grid=(b, h, s // tq),
        in_specs=[pl.BlockSpec((1, 1, tq, LANES), lambda bi, hi, i: (bi, hi, i, 0)), kv, kv],
        out_specs=pl.BlockSpec((1, tq, LANES), lambda bi, hi, i: (bi, i, hi)),
        out_shape=jax.ShapeDtypeStruct((b, s, h * LANES), F32),
        compiler_params=_cparams("parallel", "parallel", "arbitrary"),
    )(q, k, v)


def _store_heads(o_ref, o, tq):
    for r in range(NSA_REP):
        o_ref[0, :, r * LANES:(r + 1) * LANES] = o[r * tq:(r + 1) * tq]


def _cmp_attn_kernel(q_ref, kc_ref, vc_ref, o_ref, sel_ref, *, tq):
    ncmp = kc_ref.shape[2]
    q0 = pl.program_id(2) * tq
    q = q_ref[0].reshape(NSA_REP * tq, LANES)
    s = _dot_nt(q, kc_ref[0, 0]).reshape(NSA_REP, tq, ncmp)
    qpos = q0 + lax.broadcasted_iota(I32, (tq, ncmp), 0)
    n = lax.broadcasted_iota(I32, (tq, ncmp), 1)
    msk = (n * CMP_STRIDE + (CMP_LEN - 1) <= qpos)[None]
    s = jnp.where(msk, s, NEG)
    m = jnp.max(s, axis=-1, keepdims=True)
    p = jnp.where(msk, jnp.exp(s - m), 0.0)
    l = jnp.sum(p, axis=-1, keepdims=True)
    p = p * jnp.where(l > 0.0, 1.0 / l, 0.0)
    o = jnp.dot(p.reshape(NSA_REP * tq, ncmp).astype(BF16), vc_ref[0, 0], preferred_element_type=F32)
    _store_heads(o_ref, o, tq)

    nblk = SEL_BLOCKS_PAD
    psum = p[0] + p[1] + p[2] + p[3]
    bm = lax.broadcasted_iota(I32, (nblk, ncmp), 0)
    bn = lax.broadcasted_iota(I32, (nblk, ncmp), 1)
    overlap = ((bn * CMP_STRIDE < bm * SEL_LEN + SEL_LEN)
               & (bn * CMP_STRIDE + CMP_LEN > bm * SEL_LEN)).astype(F32)
    imp = _dot_nt(overlap, psum, precision=lax.Precision.HIGHEST)
    blk = lax.broadcasted_iota(I32, (nblk, tq), 0)
    cur = (q0 + lax.broadcasted_iota(I32, (nblk, tq), 1)) >> SEL_SHIFT
    forced = (blk == 0) | (blk == cur) | (blk == cur - 1)
    score = jnp.where(blk <= cur, jnp.where(forced, FORCE, imp), NEG)
    rank = jnp.zeros((nblk, tq), I32)
    for j in range(nblk):
        row = score[j:j + 1, :]
        ahead = (row > score) | ((row == score) & (blk > j))
        rank = rank + ahead.astype(I32)
    sel = ((rank < N_SEL) & (blk <= cur)).astype(F32)
    sel = jnp.concatenate([sel, jnp.zeros((LANES - nblk, tq), F32)], axis=0)
    sel_ref[0, 0] = sel.T.astype(BF16)


def _cmp_attn(q, kc, vc, tq):
    b, _, s, _ = q.shape
    ncmp = kc.shape[2]
    cspec = pl.BlockSpec((1, 1, ncmp, LANES), lambda bi, gi, i: (bi, gi, 0, 0))
    return pl.pallas_call(
        functools.partial(_cmp_attn_kernel, tq=tq),
        grid=(b, NSA_GROUPS, s // tq),
        in_specs=[pl.BlockSpec((1, NSA_REP, tq, LANES), lambda bi, gi, i: (bi, gi, i, 0)), cspec, cspec],
        out_specs=[pl.BlockSpec((1, tq, NSA_REP * LANES), lambda bi, gi, i: (bi, i, gi)),
                   pl.BlockSpec((1, 1, tq, LANES), lambda bi, gi, i: (bi, gi, i, 0))],
        out_shape=[jax.ShapeDtypeStruct((b, s, NSA_HEADS * LANES), F32),
                   jax.ShapeDtypeStruct((b, NSA_GROUPS, s, LANES), BF16)],
        compiler_params=_cparams("parallel", "parallel", "arbitrary"),
    )(q, kc, vc)


def _sel_attn_kernel(q_ref, k_ref, v_ref, sel_ref, o_ref, *, tq, tk):
    q0 = pl.program_id(2) * tq
    qpos = q0 + lax.broadcasted_iota(I32, (tq, tk), 0)
    kcol = lax.broadcasted_iota(I32, (tq, tk), 1)
    eb = lax.broadcasted_iota(I32, (LANES, tk), 0)
    ec = lax.broadcasted_iota(I32, (LANES, tk), 1)
    sel = sel_ref[0, 0]

    def mask(j):
        expand = (((j * tk + ec) >> SEL_SHIFT) == eb).astype(BF16)
        chosen = jnp.dot(sel, expand, preferred_element_type=F32)
        return (chosen > 0.5) & (j * tk + kcol <= qpos)

    q = q_ref[0].reshape(NSA_REP * tq, LANES)
    o = _attend(q, k_ref.at[0, 0], v_ref.at[0, 0], 0, (q0 + tq - 1) // tk + 1, tk, mask, NSA_REP, tq)
    _store_heads(o_ref, o, tq)


def _win_attn_kernel(q_ref, k_ref, v_ref, o_ref, *, tq, tk):
    q0 = pl.program_id(2) * tq
    qpos = q0 + lax.broadcasted_iota(I32, (tq, tk), 0)
    kcol = lax.broadcasted_iota(I32, (tq, tk), 1)

    def mask(j):
        dist = qpos - (j * tk + kcol)
        return (dist >= 0) & (dist < WIN)

    q = q_ref[0].reshape(NSA_REP * tq, LANES)
    lo = jnp.maximum(q0 - (WIN - 1), 0) // tk
    o = _attend(q, k_ref.at[0, 0], v_ref.at[0, 0], lo, (q0 + tq - 1) // tk + 1, tk, mask, NSA_REP, tq)
    _store_heads(o_ref, o, tq)


def _nsa_branch_attn(kernel, q, k, v, extra, tq, tk):
    b, _, s, _ = q.shape
    kv = pl.BlockSpec((1, 1, s, LANES), lambda bi, gi, i: (bi, gi, 0, 0))
    in_specs = [pl.BlockSpec((1, NSA_REP, tq, LANES), lambda bi, gi, i: (bi, gi, i, 0)), kv, kv]
    in_specs += [pl.BlockSpec((1, 1, tq, LANES), lambda bi, gi, i: (bi, gi, i, 0)) for _ in extra]
    return pl.pallas_call(
        functools.partial(kernel, tq=tq, tk=tk),
        grid=(b, NSA_GROUPS, s // tq),
        in_specs=in_specs,
        out_specs=pl.BlockSpec((1, tq, NSA_REP * LANES), lambda bi, gi, i: (bi, i, gi)),
        out_shape=jax.ShapeDtypeStruct((b, s, NSA_HEADS * LANES), F32),
        compiler_params=_cparams("parallel", "parallel", "arbitrary"),
    )(q, k, v, *extra)


def _mix_out_kernel(x_ref, om_ref, oc_ref, os_ref, ow_ref, gate_ref, gm_ref, gn_ref, w_ref, o_ref):
    gate = jax.nn.sigmoid(gate_ref[0])
    parts = []
    for h in range(NSA_HEADS):
        sl = slice(h * LANES, (h + 1) * LANES)
        g0 = gate[:, h:h + 1]
        g1 = gate[:, LANES + h:LANES + h + 1]
        g2 = gate[:, 2 * LANES + h:2 * LANES + h + 1]
        parts.append(g0 * oc_ref[0, :, sl] + g1 * os_ref[0, :, sl] + g2 * ow_ref[0, :, sl])
    o_nsa = jnp.concatenate(parts, axis=-1)
    width = MLA_HEADS * MLA_V
    mixed = jnp.concatenate([_rms(om_ref[0], gm_ref[...], width), _rms(o_nsa, gn_ref[...], width)], axis=-1)
    o_ref[0] = x_ref[0] + jnp.dot(mixed.astype(BF16), w_ref[...], preferred_element_type=F32)


def _mix_out(x, om, oc, os_, ow, y3, gm, gn, w, tm):
    b, s, d = x.shape
    wide = om.shape[-1]
    full = lambda shape: pl.BlockSpec(shape, lambda bi, i: (0,) * len(shape))
    row = lambda width: pl.BlockSpec((1, tm, width), lambda bi, i: (bi, i, 0))
    return pl.pallas_call(
        _mix_out_kernel,
        grid=(b, s // tm),
        in_specs=[row(d), row(wide), row(wide), row(wide), row(wide),
                  pl.BlockSpec((1, tm, 3 * LANES), lambda bi, i: (bi, i, COL_GATE // (3 * LANES))),
                  full((1, wide)), full((1, wide)), full(w.shape)],
        out_specs=row(d),
        out_shape=jax.ShapeDtypeStruct((b, s, d), F32),
        compiler_params=_cparams("parallel", "parallel"),
    )(x, om, oc, os_, ow, y3, gm, gn, w)


def _mem_kv_kernel(m_ref, g_ref, w_ref, gk_ref, k_ref, v_ref):
    mem = m_ref[0]
    n = _rms(mem, g_ref[...], mem.shape[-1]).astype(BF16)
    kv = jnp.dot(n, w_ref[...], preferred_element_type=F32)
    width = XA_HEADS * XA_DH
    for h in range(XA_HEADS):
        sl = slice(h * XA_DH, (h + 1) * XA_DH)
        k_ref[0, :, sl] = _rms(kv[:, sl], gk_ref[...], XA_DH).astype(BF16)
    v_ref[0] = kv[:, width:].astype(BF16)


def _mem_kv(mem, g, w, gk):
    b, m, d = mem.shape
    width = XA_HEADS * XA_DH
    full = lambda shape: pl.BlockSpec(shape, lambda bi: (0,) * len(shape))
    ospec = pl.BlockSpec((1, m, width), lambda bi: (bi, 0, 0))
    os_ = jax.ShapeDtypeStruct((b, m, width), BF16)
    return pl.pallas_call(
        _mem_kv_kernel,
        grid=(b,),
        in_specs=[pl.BlockSpec((1, m, d), lambda bi: (bi, 0, 0)), full((1, d)), full(w.shape), full((1, XA_DH))],
        out_specs=[ospec, ospec],
        out_shape=[os_, os_],
        compiler_params=_cparams("parallel"),
    )(mem, g, w, gk)


def _xattn_kernel(h_ref, g_ref, wq_ref, gq_ref, k_ref, v_ref, wo_ref, o_ref):
    h = h_ref[0]
    hn = _rms(h, g_ref[...], h.shape[-1]).astype(BF16)
    q = jnp.dot(hn, wq_ref[...], preferred_element_type=F32)
    outs = []
    for hd in range(XA_HEADS):
        sl = slice(hd * XA_DH, (hd + 1) * XA_DH)
        qh = (_rms(q[:, sl], gq_ref[...], XA_DH) * (XA_DH ** -0.5)).astype(BF16)
        s = _dot_nt(qh, k_ref[0, :, sl])
        p = jnp.exp(s - jnp.max(s, axis=-1, keepdims=True))
        p = p * (1.0 / jnp.sum(p, axis=-1, keepdims=True))
        outs.append(jnp.dot(p.astype(BF16), v_ref[0, :, sl], preferred_element_type=F32))
    o = jnp.concatenate(outs, axis=-1).astype(BF16)
    o_ref[0] = h + jnp.dot(o, wo_ref[...], preferred_element_type=F32)


def _xattn(h, g, wq, gq, k, v, wo, tm):
    b, s, d = h.shape
    m, width = k.shape[1], k.shape[2]
    full = lambda shape: pl.BlockSpec(shape, lambda bi, i: (0,) * len(shape))
    kv = pl.BlockSpec((1, m, width), lambda bi, i: (bi, 0, 0))
    row = pl.BlockSpec((1, tm, d), lambda bi, i: (bi, i, 0))
    return pl.pallas_call(
        _xattn_kernel,
        grid=(b, s // tm),
        in_specs=[row, full((1, d)), full(wq.shape), full((1, XA_DH)), kv, kv, full(wo.shape)],
        out_specs=row,
        out_shape=jax.ShapeDtypeStruct((b, s, d), F32),
        compiler_params=_cparams("parallel", "parallel"),
    )(h, g, wq, gq, k, v, wo)


def _peer_score_kernel(h_ref, g_ref, wq_ref, kbd_ref, hn_ref, s_ref):
    h = h_ref[...]
    hn = _rms(h, g_ref[...], h.shape[-1]).astype(BF16)
    hn_ref[...] = hn
    q = jnp.dot(hn, wq_ref[...], preferred_element_type=F32).astype(BF16)
    s_ref[...] = _dot_nt(kbd_ref[...], q)


def _peer_score(h2d, g, wq, kbd, tm):
    t, d = h2d.shape
    nk = kbd.shape[0]
    full = lambda shape: pl.BlockSpec(shape, lambda i: (0,) * len(shape))
    return pl.pallas_call(
        _peer_score_kernel,
        grid=(t // tm,),
        in_specs=[pl.BlockSpec((tm, d), lambda i: (i, 0)), full((1, d)), full(wq.shape), full(kbd.shape)],
        out_specs=[pl.BlockSpec((tm, d), lambda i: (i, 0)), pl.BlockSpec((nk, tm), lambda i: (0, i))],
        out_shape=[jax.ShapeDtypeStruct((t, d), BF16), jax.ShapeDtypeStruct((nk, t), F32)],
        compiler_params=_cparams("parallel"),
    )(h2d, g, wq, kbd)


def _topk_rows(x, k, ids, payload=None):
    vals, outs = [], []
    for _ in range(k):
        m = jnp.max(x, axis=0, keepdims=True)
        idx = jnp.min(jnp.where(x == m, ids, jnp.iinfo(jnp.int32).max), axis=0, keepdims=True)
        hit = ids == idx
        vals.append(m)
        if payload is None:
            outs.append(idx)
        else:
            outs.append(jnp.max(jnp.where(hit, payload, -1), axis=0, keepdims=True))
        x = jnp.where(hit, -jnp.inf, x)
    return jnp.concatenate(vals, axis=0), jnp.concatenate(outs, axis=0)


PEER_EDGE = 4
assert (PEER_EDGE + 1) ** 2 > PEER_TOPK


def _peer_route_kernel(s_ref, e_ref, g_ref, e_scr, g_scr):
    tm = s_ref.shape[1]
    key_ids = lax.broadcasted_iota(I32, (PEER_KEYS, tm), 0)
    a_ids = lax.broadcasted_iota(I32, (PEER_TOPK, tm), 0)

    def head(h, carry):
        base = pl.multiple_of(h * (2 * PEER_KEYS), 2 * PEER_KEYS)
        s1, i1 = _topk_rows(s_ref[pl.ds(base, PEER_KEYS), :], PEER_TOPK, key_ids)
        s2, i2 = _topk_rows(s_ref[pl.ds(base + PEER_KEYS, PEER_KEYS), :], PEER_TOPK, key_ids)
        i1 = i1 * PEER_KEYS
        cand, flat, cidx = [], [], []
        for a in range(PEER_EDGE):
            cand.append(s1[a:a + 1, :] + s2)
            flat.append(a * PEER_TOPK + a_ids)
            cidx.append(i1[a:a + 1, :] + i2)
        for b in range(PEER_EDGE):
            cand.append(jnp.where(a_ids >= PEER_EDGE, s1 + s2[b:b + 1, :], -jnp.inf))
            flat.append(a_ids * PEER_TOPK + b)
            cidx.append(i1 + i2[b:b + 1, :])
        top, eidx = _topk_rows(jnp.concatenate(cand, axis=0), PEER_TOPK, jnp.concatenate(flat, axis=0),
                               payload=jnp.concatenate(cidx, axis=0))
        p = jnp.exp(top - top[0:1, :])
        p = p * (1.0 / jnp.sum(p, axis=0, keepdims=True))
        row = pl.multiple_of(h * PEER_TOPK, PEER_TOPK)
        e_scr[pl.ds(row, PEER_TOPK), :] = eidx * PEER_WORD_ROWS
        g_scr[pl.ds(row, PEER_TOPK), :] = p
        return carry

    lax.fori_loop(0, PEER_HEADS, head, 0)
    e_ref[...] = e_scr[...].T
    g_ref[...] = g_scr[...].T


def _peer_route(st):
    nk, t = st.shape
    tm = LANES
    ospec = pl.BlockSpec((tm, PEER_PICKS), lambda i: (i, 0))
    return pl.pallas_call(
        _peer_route_kernel,
        grid=(t // tm,),
        in_specs=[pl.BlockSpec((nk, tm), lambda i: (0, i))],
        out_specs=[ospec, ospec],
        out_shape=[jax.ShapeDtypeStruct((t, PEER_PICKS), I32), jax.ShapeDtypeStruct((t, PEER_PICKS), F32)],
        scratch_shapes=[pltpu.VMEM((PEER_PICKS, tm), I32), pltpu.VMEM((PEER_PICKS, tm), F32)],
        compiler_params=_cparams("parallel"),
    )(st)


def _gather_rows(idx_ref, tab_ref, tile_ref, j):
    for k in range(PEER_PICKS):
        r = pl.multiple_of(idx_ref[j, k], PEER_WORD_ROWS)
        tile_ref[pl.ds(k, PEER_WORD_ROWS, stride=PEER_CHUNK_STRIDE), :] = tab_ref[pl.ds(r, PEER_WORD_ROWS), :]


def _tile_rows(tile_ref):
    chunks = [pltpu.bitcast(tile_ref[pl.ds(c * PEER_CHUNK_STRIDE, PEER_PICKS), :], BF16)
              for c in range(PEER_WORD_ROWS)]
    return jnp.concatenate(chunks, axis=1)


def _token_loop(idx_ref, tab_ref, tile_a, tile_b, compute):
    _gather_rows(idx_ref, tab_ref, tile_a, 0)

    def group(i, carry):
        for u in range(PEER_UNROLL):
            j = i * PEER_UNROLL + u
            cur, nxt = (tile_a, tile_b) if u % 2 == 0 else (tile_b, tile_a)
            rows = _tile_rows(cur)
            _gather_rows(idx_ref, tab_ref, nxt, jnp.minimum(j + 1, PEER_TOK - 1))
            compute(j, rows)
        return carry

    lax.fori_loop(0, PEER_TOK // PEER_UNROLL, group, 0)


def _pair_matrix():
    r = lax.broadcasted_iota(I32, (2 * PEER_PICKS, PEER_PICKS), 0)
    c = lax.broadcasted_iota(I32, (2 * PEER_PICKS, PEER_PICKS), 1)
    return (r >> 1) == c


def _peer_up_kernel(idx_ref, x_ref, g_ref, tab_ref, w_ref, tile_a, tile_b, act_ref, xf_ref):
    half = x_ref.shape[1] // 2
    sub = lax.broadcasted_iota(I32, (16, half), 0)
    odd_lane = (lax.broadcasted_iota(I32, (1, 2 * PEER_PICKS), 1) & 1) == 1
    xf_ref[...] = x_ref[...].astype(F32)

    def compute(j, rows):
        xr = xf_ref[pl.ds(j, 1), :]
        lo = jnp.broadcast_to(xr[:, :half], (16, half))
        hi = jnp.broadcast_to(xr[:, half:], (16, half))
        xsel = jnp.where(sub == 0, lo, jnp.where(sub == 1, hi, 0.0)).astype(BF16)
        r = _dot_nt(xsel, rows)
        act_ref[pl.ds(j, 1), :] = jnp.where(odd_lane, r[1:2, :], r[0:1, :])

    _token_loop(idx_ref, tab_ref, tile_a, tile_b, compute)
    pair = _pair_matrix()
    act = jnp.dot(act_ref[...], pair.astype(F32), preferred_element_type=F32,
                  precision=lax.Precision.HIGHEST)
    w = (g_ref[...] * _gelu(act)).astype(BF16)
    w_ref[...] = _dot_nt(w, pair.astype(BF16))


def _peer_down_kernel(idx_ref, w_ref, h_ref, tab_ref, o_ref, tile_a, tile_b):
    half = h_ref.shape[1] // 2
    shape = (16, 2 * PEER_PICKS)
    parity = (lax.broadcasted_iota(I32, shape, 1) & 1) == lax.broadcasted_iota(I32, shape, 0)

    def compute(j, rows):
        wrow = jnp.broadcast_to(w_ref[pl.ds(j, 1), :], shape)
        wm = jnp.where(parity, wrow, 0.0).astype(BF16)
        o2 = jnp.dot(wm, rows, preferred_element_type=F32)
        o_ref[pl.ds(j, 1), 0:half] = h_ref[pl.ds(j, 1), 0:half] + o2[0:1, :]
        o_ref[pl.ds(j, 1), half:] = h_ref[pl.ds(j, 1), half:] + o2[1:2, :]

    _token_loop(idx_ref, tab_ref, tile_a, tile_b, compute)


def _tile_scratch():
    return pltpu.VMEM((PEER_WORD_ROWS * PEER_CHUNK_STRIDE, LANES), I32)


def _table_spec(tab):
    return pl.BlockSpec(tab.shape, lambda c: (0, 0), pipeline_mode=pl.Buffered(1))


def _idx_spec():
    return pl.BlockSpec((PEER_TOK, PEER_PICKS), lambda c: (c, 0), memory_space=pltpu.SMEM)


def _peer_up(eidx, hn, gw, tab):
    t, d = hn.shape
    return pl.pallas_call(
        _peer_up_kernel,
        grid=(t // PEER_TOK,),
        in_specs=[_idx_spec(),
                  pl.BlockSpec((PEER_TOK, d), lambda c: (c, 0)),
                  pl.BlockSpec((PEER_TOK, PEER_PICKS), lambda c: (c, 0)),
                  _table_spec(tab)],
        out_specs=pl.BlockSpec((PEER_TOK, 2 * PEER_PICKS), lambda c: (c, 0)),
        out_shape=jax.ShapeDtypeStruct((t, 2 * PEER_PICKS), F32),
        scratch_shapes=[_tile_scratch(), _tile_scratch(), pltpu.VMEM((PEER_TOK, 2 * PEER_PICKS), F32),
                        pltpu.VMEM((PEER_TOK, d), F32)],
        compiler_params=_cparams("arbitrary"),
    )(eidx, hn, gw, tab)


def _peer_down(eidx, w, h2d, tab):
    t, d = h2d.shape
    return pl.pallas_call(
        _peer_down_kernel,
        grid=(t // PEER_TOK,),
        in_specs=[_idx_spec(),
                  pl.BlockSpec((PEER_TOK, 2 * PEER_PICKS), lambda c: (c, 0)),
                  pl.BlockSpec((PEER_TOK, d), lambda c: (c, 0)),
                  _table_spec(tab)],
        out_specs=pl.BlockSpec((PEER_TOK, d), lambda c: (c, 0)),
        out_shape=jax.ShapeDtypeStruct((t, d), F32),
        scratch_shapes=[_tile_scratch(), _tile_scratch()],
        compiler_params=_cparams("arbitrary"),
    )(eidx, w, h2d, tab)


def _pad_heads(w, heads, dim, dim_pad=LANES):
    lead = w.shape[:-1]
    w = w.reshape(lead + (heads, dim))
    w = jnp.pad(w, [(0, 0)] * len(lead) + [(0, 0), (0, dim_pad - dim)])
    return w.reshape(lead + (heads * dim_pad,))


def _pad_lanes(v, offset=0, width=LANES):
    return jnp.pad(v, [(0, 0)] * (v.ndim - 1) + [(offset, width - offset - v.shape[-1])])


def _rope_tabs(pos, rot_dim, offset):
    inv = 1.0 / (ROPE_THETA ** (jnp.arange(0, rot_dim, 2, dtype=F32) / rot_dim))
    ang = pos.astype(F32)[:, None] * inv[None, :]
    cos, sin = jnp.cos(ang), jnp.sin(ang)
    zero = jnp.zeros_like(sin)
    c = jnp.pad(jnp.concatenate([cos, cos], -1) - 1.0, ((0, 0), (offset, LANES - offset - rot_dim))) + 1.0
    sa = _pad_lanes(jnp.concatenate([-sin, zero], -1), offset)
    sb = _pad_lanes(jnp.concatenate([zero, sin], -1), offset)
    return c, sa, sb


def _pack_table(tab):
    e, d = tab.shape
    bits = lax.bitcast_convert_type(tab.astype(BF16), jnp.uint16).astype(jnp.uint32)
    word = bits[:, :d // 2] | (bits[:, d // 2:] << 16)
    return lax.bitcast_convert_type(word, I32).reshape(e * PEER_WORD_ROWS, LANES)


def _relayout_w_in(w_in):
    d = w_in.shape[0]
    o = 0
    seg = {}
    for name, size in (("cq", 256), ("ckv", 128), ("kpe", 32), ("qn", 512), ("kc", 128), ("vc", 128),
                       ("ks", 128), ("vs", 128), ("kw", 128), ("vw", 128), ("gate", 24)):
        seg[name] = w_in[:, o:o + size]
        o += size
    gates = [_pad_lanes(seg["gate"][:, j::3]) for j in range(3)]
    cols = [_pad_heads(seg["qn"], NSA_HEADS, NSA_DK),
            _pad_heads(seg["ks"], NSA_GROUPS, NSA_DK), _pad_heads(seg["kw"], NSA_GROUPS, NSA_DK),
            seg["cq"], seg["ckv"], _pad_lanes(seg["kpe"], MLA_NOPE),
            _pad_heads(seg["vs"], NSA_GROUPS, NSA_DK), _pad_heads(seg["vw"], NSA_GROUPS, NSA_DK),
            seg["kc"]] + gates + [seg["vc"]]
    w = jnp.concatenate(cols, axis=1)
    assert w.shape == (d, IN_COLS_PAD)
    return w.astype(BF16)


def _layer(h, mem, p):
    b, s, d = h.shape
    t = b * s
    assert s % 512 == 0 and N_SEL <= s // SEL_LEN <= SEL_BLOCKS_PAD and d == 1024
    pos = jnp.arange(s)

    y = _norm_matmul(h.reshape(t, d), p["mix_norm"], _relayout_w_in(p["w_in"]), 512)
    y3 = y.reshape(b, s, IN_COLS_PAD)

    wuq = _pad_heads(p["mla_w_uq"], MLA_HEADS, MLA_QK).astype(BF16)
    wukv = p["mla_w_ukv"].reshape(MLA_KV_LORA, MLA_HEADS, MLA_NOPE + MLA_V)
    wuk = _pad_heads(wukv[..., :MLA_NOPE].reshape(MLA_KV_LORA, -1), MLA_HEADS, MLA_NOPE).astype(BF16)
    wuv = _pad_heads(wukv[..., MLA_NOPE:].reshape(MLA_KV_LORA, -1), MLA_HEADS, MLA_V).astype(BF16)
    q, k, v = _mla_prep(y3, p["mla_q_norm"][None], p["mla_kv_norm"][None], _pad_lanes(p["mla_qk_norm"]),
                        wuq, wuk, wuv, _rope_tabs(pos, MLA_ROPE, MLA_NOPE), 512)
    o_mla = _mla_attn(q, k, v, 256, 512)

    tabs = _rope_tabs(pos, NSA_ROT, 0)
    qn, ks, kw, vs, vw = _nsa_prep(y3, _pad_lanes(p["nsa_q_norm"][None]), _pad_lanes(p["nsa_k_norm"][1:3]),
                                   tabs, 512)
    nrow = s // CMP_STRIDE

    def blocks16(col):
        tkv = y3[:, :, col:col + NSA_GROUPS * NSA_DK].reshape(b, nrow, CMP_STRIDE, NSA_GROUPS, NSA_DK)
        return tkv.transpose(0, 3, 1, 2, 4).reshape(b, NSA_GROUPS, nrow, CMP_STRIDE * NSA_DK)

    pe = p["nsa_cmp_pe"].reshape(2, 2, CMP_STRIDE * NSA_DK)
    w1 = p["nsa_cmp_w1"].reshape(2, 2, CMP_STRIDE * NSA_DK, -1).astype(BF16)
    w2 = _pad_lanes(p["nsa_cmp_w2"]).astype(BF16)
    cmp_end = jnp.arange(nrow) * CMP_STRIDE + CMP_LEN - 1
    kc, vc = _compress(blocks16(COL_KC), blocks16(COL_VC), pe, w1, w2, _pad_lanes(p["nsa_k_norm"][0:1]),
                       _rope_tabs(cmp_end, NSA_ROT, 0))
    o_cmp, sel = _cmp_attn(qn, kc, vc, 128)
    o_sel = _nsa_branch_attn(_sel_attn_kernel, qn, ks, vs, (sel,), 128, 256)
    o_win = _nsa_branch_attn(_win_attn_kernel, qn, kw, vw, (), 128, 256)

    width = MLA_HEADS * MLA_V
    g_out = p["mix_out_norm"]
    gm = _pad_heads(g_out[None, :width], MLA_HEADS, MLA_V)
    gn = _pad_heads(g_out[None, width:], NSA_HEADS, NSA_DK)
    w_out = p["w_out"].reshape(2 * MLA_HEADS, MLA_V, d)
    w_out = jnp.pad(w_out, ((0, 0), (0, LANES - MLA_V), (0, 0))).reshape(2 * MLA_HEADS * LANES, d).astype(BF16)
    h = _mix_out(h, o_mla, o_cmp, o_sel, o_win, y3, gm, gn, w_out, 256)

    kx, vx = _mem_kv(mem, p["mem_norm"][None], p["xa_wkv"].astype(BF16), p["xa_qk_norm"][1:2])
    h = _xattn(h, p["xa_norm"][None], p["xa_wq"].astype(BF16), p["xa_qk_norm"][0:1], kx, vx,
               p["xa_wo"].astype(BF16), 256)

    keys = p["peer_keys"]
    nhp = PEER_HEADS * 2
    eye = jnp.eye(nhp, dtype=F32)
    kbd = (keys.reshape(nhp, PEER_KEYS, PEER_HALF)[:, :, None, :] * eye[:, None, :, None])
    kbd = kbd.reshape(nhp * PEER_KEYS, nhp * PEER_HALF).astype(BF16)
    h2d = h.reshape(t, d)
    hn, st = _peer_score(h2d, p["ffn_norm"][None], p["peer_wq"].astype(BF16), kbd, 512)
    eidx, gw = _peer_route(st)
    w = _peer_up(eidx, hn, gw, _pack_table(p["peer_u"]))
    out = _peer_down(eidx, w, h2d, _pack_table(p["peer_v"]))
    return out.reshape(b, s, d)


def kernel(x, mem, mix_norm, w_in, mla_q_norm, mla_kv_norm, mla_w_uq, mla_w_ukv, mla_qk_norm, nsa_q_norm, nsa_k_norm, nsa_cmp_pe, nsa_cmp_w1, nsa_cmp_w2, mix_out_norm, w_out, xa_norm, mem_norm, xa_wq, xa_wkv, xa_qk_norm, xa_wo, ffn_norm, peer_wq, peer_keys, peer_u, peer_v):
    params = dict(mix_norm=mix_norm, w_in=w_in, mla_q_norm=mla_q_norm, mla_kv_norm=mla_kv_norm,
                  mla_w_uq=mla_w_uq, mla_w_ukv=mla_w_ukv, mla_qk_norm=mla_qk_norm, nsa_q_norm=nsa_q_norm,
                  nsa_k_norm=nsa_k_norm, nsa_cmp_pe=nsa_cmp_pe, nsa_cmp_w1=nsa_cmp_w1, nsa_cmp_w2=nsa_cmp_w2,
                  mix_out_norm=mix_out_norm, w_out=w_out, xa_norm=xa_norm, mem_norm=mem_norm, xa_wq=xa_wq,
                  xa_wkv=xa_wkv, xa_qk_norm=xa_qk_norm, xa_wo=xa_wo, ffn_norm=ffn_norm, peer_wq=peer_wq,
                  peer_keys=peer_keys, peer_u=peer_u, peer_v=peer_v)
    h = x
    for layer in range(w_in.shape[0]):
        h = _layer(h, mem, {name: val[layer] for name, val in params.items()})
    return h
```

```python
import functools

import jax
import jax.numpy as jnp
from jax import lax
from jax.experimental import pallas as pl
from jax.experimental.pallas import tpu as pltpu

F32 = jnp.float32
BF16 = jnp.bfloat16
I32 = jnp.int32

LANES = 128
VMEM_LIMIT = 56 * 1024 * 1024

ROPE_THETA = 500000.0
EPS = 1e-6
NEG = -1e30
FORCE = 1e4

MLA_HEADS = 8
MLA_Q_LORA = 256
MLA_KV_LORA = 128
MLA_NOPE = 64
MLA_ROPE = 32
MLA_V = 64
MLA_QK = MLA_NOPE + MLA_ROPE

NSA_HEADS = 8
NSA_GROUPS = 2
NSA_REP = NSA_HEADS // NSA_GROUPS
NSA_DK = 64
NSA_ROT = NSA_DK // 4
CMP_LEN = 32
CMP_STRIDE = 16
SEL_LEN = 64
SEL_SHIFT = SEL_LEN.bit_length() - 1
SEL_BLOCKS_PAD = 64
N_SEL = 16
WIN = 512
SEL_CHUNK = 1024
NSA_TQ = 128
LOG2E = 1.4426950408889634

XA_HEADS = 4
XA_DH = 128

PEER_HEADS = 8
PEER_KEYS = 128
PEER_HALF = 64
PEER_TOPK = 16
PEER_PICKS = PEER_HEADS * PEER_TOPK
PEER_TOK = 128
PEER_WORD_ROWS = 4
PEER_CHUNK_STRIDE = 136
PEER_UNROLL = 8

COL_QN = 0
COL_KS = 1024
COL_KW = 1280
COL_CQ = 1536
COL_CKV = 1792
COL_KPE = 1920
COL_VS = 2048
COL_VW = 2304
COL_KC = 2560
COL_GATE = 2688
COL_VC = 3072
IN_COLS_PAD = 3200


def _cparams(*sem):
    return pltpu.CompilerParams(dimension_semantics=sem, vmem_limit_bytes=VMEM_LIMIT)


def _rms(x, g, n):
    ms = jnp.sum(x * x, axis=-1, keepdims=True) * (1.0 / n)
    return x * lax.rsqrt(ms + EPS) * g


def _rope(x, c, sa, sb, half):
    return x * c + pltpu.roll(x, LANES - half, 1) * sa + pltpu.roll(x, half, 1) * sb


def _gelu(x):
    return 0.5 * x * (1.0 + jnp.tanh(0.7978845608028654 * (x + 0.044715 * (x * x * x))))


def _dot_nt(a, b, **kw):
    return lax.dot_general(a, b, (((1,), (1,)), ((), ())), preferred_element_type=F32, **kw)


def _norm_matmul_kernel(x_ref, g_ref, w_ref, o_ref):
    x = x_ref[...]
    n = _rms(x, g_ref[...], x.shape[-1])
    o_ref[...] = jnp.dot(n.astype(BF16), w_ref[...], preferred_element_type=F32)


def _norm_matmul(x2d, g, w, tm):
    t, d = x2d.shape
    n = w.shape[1]
    return pl.pallas_call(
        _norm_matmul_kernel,
        grid=(t // tm,),
        in_specs=[pl.BlockSpec((tm, d), lambda i: (i, 0)),
                  pl.BlockSpec((1, d), lambda i: (0, 0)),
                  pl.BlockSpec((d, n), lambda i: (0, 0))],
        out_specs=pl.BlockSpec((tm, n), lambda i: (i, 0)),
        out_shape=jax.ShapeDtypeStruct((t, n), F32),
        compiler_params=_cparams("parallel"),
    )(x2d, g.reshape(1, d), w)


def _mla_prep_kernel(y_ref, gq_ref, gkv_ref, gqk_ref, wuq_ref, wuk_ref, wuv_ref,
                     c_ref, sa_ref, sb_ref, q_ref, k_ref, v_ref):
    y = y_ref[0]
    cq = y[:, 0:MLA_Q_LORA]
    ckv = y[:, MLA_Q_LORA:MLA_Q_LORA + MLA_KV_LORA]
    kpe = y[:, MLA_Q_LORA + MLA_KV_LORA:]
    nq = _rms(cq, gq_ref[...], MLA_Q_LORA).astype(BF16)
    nkv = _rms(ckv, gkv_ref[...], MLA_KV_LORA).astype(BF16)
    qf = jnp.dot(nq, wuq_ref[...], preferred_element_type=F32)
    kf = jnp.dot(nkv, wuk_ref[...], preferred_element_type=F32)
    vf = jnp.dot(nkv, wuv_ref[...], preferred_element_type=F32)
    c, sa, sb = c_ref[...], sa_ref[...], sb_ref[...]
    gq = gqk_ref[0:1, :]
    gk = gqk_ref[1:2, :]
    scale = MLA_QK ** -0.5 * LOG2E
    for h in range(MLA_HEADS):
        sl = slice(h * LANES, (h + 1) * LANES)
        qh = _rope(_rms(qf[:, sl], gq, MLA_QK), c, sa, sb, MLA_ROPE // 2)
        kh = _rope(_rms(kf[:, sl] + kpe, gk, MLA_QK), c, sa, sb, MLA_ROPE // 2)
        q_ref[0, h] = (qh * scale).astype(BF16)
        k_ref[0, h] = kh.astype(BF16)
        v_ref[0, h] = vf[:, sl].astype(BF16)


def _mla_prep(y3, gq, gkv, gqk, wuq, wuk, wuv, tabs, tm):
    b, s, _ = y3.shape
    hs = jax.ShapeDtypeStruct((b, MLA_HEADS, s, LANES), BF16)
    full = lambda shape: pl.BlockSpec(shape, lambda bi, i: (0,) * len(shape))
    tab = pl.BlockSpec((tm, LANES), lambda bi, i: (i, 0))
    hspec = pl.BlockSpec((1, MLA_HEADS, tm, LANES), lambda bi, i: (bi, 0, i, 0))
    return pl.pallas_call(
        _mla_prep_kernel,
        grid=(b, s // tm),
        in_specs=[pl.BlockSpec((1, tm, 512), lambda bi, i: (bi, i, COL_CQ // 512)),
                  full((1, MLA_Q_LORA)), full((1, MLA_KV_LORA)), full((2, LANES)),
                  full((MLA_Q_LORA, MLA_HEADS * LANES)), full((MLA_KV_LORA, MLA_HEADS * LANES)),
                  full((MLA_KV_LORA, MLA_HEADS * LANES)), tab, tab, tab],
        out_specs=[hspec, hspec, hspec],
        out_shape=[hs, hs, hs],
        compiler_params=_cparams("parallel", "parallel"),
    )(y3, gq, gkv, gqk, wuq, wuk, wuv, *tabs)


def _nsa_prep_kernel(yq_ref, yv_ref, gq_ref, gk_ref, c_ref, sa_ref, sb_ref,
                     q_ref, ks_ref, kw_ref, vs_ref, vw_ref):
    c, sa, sb = c_ref[...], sa_ref[...], sb_ref[...]
    half = NSA_ROT // 2
    scale = NSA_DK ** -0.5 * LOG2E
    yq = yq_ref[0]
    for h in range(NSA_HEADS):
        qh = _rms(yq[:, h * LANES:(h + 1) * LANES], gq_ref[...], NSA_DK)
        q_ref[0, h] = (_rope(qh, c, sa, sb, half) * scale).astype(BF16)
    for g in range(NSA_GROUPS):
        ks = yq[:, COL_KS + g * LANES:COL_KS + (g + 1) * LANES]
        kw = yq[:, COL_KW + g * LANES:COL_KW + (g + 1) * LANES]
        ks_ref[0, g] = _rope(_rms(ks, gk_ref[0:1, :], NSA_DK), c, sa, sb, half).astype(BF16)
        kw_ref[0, g] = _rope(_rms(kw, gk_ref[1:2, :], NSA_DK), c, sa, sb, half).astype(BF16)
    yv = yv_ref[0]
    for g in range(NSA_GROUPS):
        vs_ref[0, g] = yv[:, g * LANES:(g + 1) * LANES].astype(BF16)
        vw_ref[0, g] = yv[:, (NSA_GROUPS + g) * LANES:(NSA_GROUPS + g + 1) * LANES].astype(BF16)


def _nsa_prep(y3, gq, gk, tabs, tm):
    b, s, _ = y3.shape
    full = lambda shape: pl.BlockSpec(shape, lambda bi, i: (0,) * len(shape))
    tab = pl.BlockSpec((tm, LANES), lambda bi, i: (i, 0))
    qs = jax.ShapeDtypeStruct((b, NSA_HEADS, s, LANES), BF16)
    gs = jax.ShapeDtypeStruct((b, NSA_GROUPS, s, LANES), BF16)
    gspec = pl.BlockSpec((1, NSA_GROUPS, tm, LANES), lambda bi, i: (bi, 0, i, 0))
    return pl.pallas_call(
        _nsa_prep_kernel,
        grid=(b, s // tm),
        in_specs=[pl.BlockSpec((1, tm, 1536), lambda bi, i: (bi, i, 0)),
                  pl.BlockSpec((1, tm, 512), lambda bi, i: (bi, i, COL_VS // 512)),
                  full((1, LANES)), full((2, LANES)), tab, tab, tab],
        out_specs=[pl.BlockSpec((1, NSA_HEADS, tm, LANES), lambda bi, i: (bi, 0, i, 0)),
                   gspec, gspec, gspec, gspec],
        out_shape=[qs, gs, gs, gs, gs],
        compiler_params=_cparams("parallel", "parallel"),
    )(y3, y3, gq, gk, *tabs)


def _compress_kernel(tk_ref, tv_ref, pe_ref, w1_ref, w2_ref, gk_ref, c_ref, sa_ref, sb_ref,
                     kc_ref, vc_ref):
    nrow = tk_ref.shape[2]

    def mlp(t, kv):
        ta = (t + pe_ref[kv, 0:1, :]).astype(BF16)
        tb = (t + pe_ref[kv, 1:2, :]).astype(BF16)
        za = jnp.dot(ta, w1_ref[kv, 0], preferred_element_type=F32)
        zb = jnp.dot(tb, w1_ref[kv, 1], preferred_element_type=F32)
        hid = _gelu(za + pltpu.roll(zb, nrow - 1, 0))
        return jnp.dot(hid.astype(BF16), w2_ref[kv], preferred_element_type=F32)

    kc = mlp(tk_ref[0, 0], 0)
    kc = _rope(_rms(kc, gk_ref[...], NSA_DK), c_ref[...], sa_ref[...], sb_ref[...], NSA_ROT // 2)
    kc_ref[0, 0] = kc.astype(BF16)
    vc_ref[0, 0] = mlp(tv_ref[0, 0], 1).astype(BF16)


def _compress(tk16, tv16, pe, w1, w2, gk, tabs):
    b, g, nrow, width = tk16.shape
    full = lambda shape: pl.BlockSpec(shape, lambda bi, gi: (0,) * len(shape))
    tspec = pl.BlockSpec((1, 1, nrow, width), lambda bi, gi: (bi, gi, 0, 0))
    ospec = pl.BlockSpec((1, 1, nrow, LANES), lambda bi, gi: (bi, gi, 0, 0))
    os_ = jax.ShapeDtypeStruct((b, g, nrow, LANES), BF16)
    return pl.pallas_call(
        _compress_kernel,
        grid=(b, g),
        in_specs=[tspec, tspec, full(pe.shape), full(w1.shape), full(w2.shape), full((1, LANES)),
                  full((nrow, LANES)), full((nrow, LANES)), full((nrow, LANES))],
        out_specs=[ospec, ospec],
        out_shape=[os_, os_],
        compiler_params=_cparams("parallel", "parallel"),
    )(tk16, tv16, pe, w1, w2, gk, *tabs)


def _softmax_init(heads, tq):
    return (jnp.full((heads, tq, 1), NEG, F32), jnp.zeros((heads, tq, 1), F32),
            jnp.zeros((heads * tq, LANES), F32))


def _softmax_step(carry, q, k_ref, v_ref, start, width, mask):
    m, l, acc = carry
    heads, tq, _ = m.shape
    k = k_ref[pl.ds(start, width), :]
    v = v_ref[pl.ds(start, width), :]
    s = _dot_nt(q, k).reshape(heads, tq, width)
    if mask is not None:
        s = jnp.where(mask[None], s, NEG)
    m_new = jnp.maximum(m, jnp.max(s, axis=-1, keepdims=True))
    p = jnp.exp2(s - m_new)
    alpha = jnp.exp2(m - m_new)
    l = alpha * l + jnp.sum(p, axis=-1, keepdims=True)
    pv = jnp.dot(p.reshape(heads * tq, width).astype(BF16), v, preferred_element_type=F32)
    return m_new, l, alpha.reshape(heads * tq, 1) * acc + pv


def _softmax_finish(carry):
    _, l, acc = carry
    return acc * (1.0 / l).reshape(acc.shape[0], 1)


def _mla_attn_kernel(q_ref, k_ref, v_ref, o_ref, *, tq):
    q0 = pl.program_id(2) * tq
    q = q_ref[0, 0]
    kr, vr = k_ref.at[0, 0], v_ref.at[0, 0]
    wide = 2 * tq
    nwide = q0 // wide
    carry = lax.fori_loop(
        0, nwide, lambda j, c: _softmax_step(c, q, kr, vr, pl.multiple_of(j * wide, wide), wide, None),
        _softmax_init(1, tq))
    carry = lax.fori_loop(
        0, (q0 - nwide * wide) // tq,
        lambda j, c: _softmax_step(c, q, kr, vr, pl.multiple_of(nwide * wide, tq), tq, None), carry)
    causal = lax.broadcasted_iota(I32, (tq, tq), 1) <= lax.broadcasted_iota(I32, (tq, tq), 0)
    carry = _softmax_step(carry, q, kr, vr, pl.multiple_of(q0, tq), tq, causal)
    o_ref[0] = _softmax_finish(carry)


def _mla_attn(q, k, v, tq):
    b, h, s, _ = q.shape
    kv = pl.BlockSpec((1, 1, s, LANES), lambda bi, hi, i: (bi, hi, 0, 0))
    return pl.pallas_call(
        functools.partial(_mla_attn_kernel, tq=tq),
        grid=(b, h, s // tq),
        in_specs=[pl.BlockSpec((1, 1, tq, LANES), lambda bi, hi, i: (bi, hi, i, 0)), kv, kv],
        out_specs=pl.BlockSpec((1, tq, LANES), lambda bi, hi, i: (bi, i, hi)),
        out_shape=jax.ShapeDtypeStruct((b, s, h * LANES), F32),
        compiler_params=_cparams("parallel", "parallel", "arbitrary"),
    )(q, k, v)


def _store_heads(o_ref, o, tq):
    for r in range(NSA_REP):
        o_ref[0, :, r * LANES:(r + 1) * LANES] = o[r * tq:(r + 1) * tq]


def _cmp_attn_kernel(q_ref, kc_ref, vc_ref, o_ref, sel_ref, *, tq):
    ncmp = kc_ref.shape[2]
    q0 = pl.program_id(2) * tq
    q = q_ref[0].reshape(NSA_REP * tq, LANES)
    s = _dot_nt(q, kc_ref[0, 0]).reshape(NSA_REP, tq, ncmp)
    qpos = q0 + lax.broadcasted_iota(I32, (tq, ncmp), 0)
    n = lax.broadcasted_iota(I32, (tq, ncmp), 1)
    msk = (n * CMP_STRIDE + (CMP_LEN - 1) <= qpos)[None]
    s = jnp.where(msk, s, NEG)
    m = jnp.max(s, axis=-1, keepdims=True)
    p = jnp.where(msk, jnp.exp2(s - m), 0.0)
    l = jnp.sum(p, axis=-1, keepdims=True)
    p = p * jnp.where(l > 0.0, 1.0 / l, 0.0)
    o = jnp.dot(p.reshape(NSA_REP * tq, ncmp).astype(BF16), vc_ref[0, 0], preferred_element_type=F32)
    _store_heads(o_ref, o, tq)

    nblk = SEL_BLOCKS_PAD
    psum = p[0] + p[1] + p[2] + p[3]
    bm = lax.broadcasted_iota(I32, (nblk, ncmp), 0)
    bn = lax.broadcasted_iota(I32, (nblk, ncmp), 1)
    overlap = ((bn * CMP_STRIDE < bm * SEL_LEN + SEL_LEN)
               & (bn * CMP_STRIDE + CMP_LEN > bm * SEL_LEN)).astype(F32)
    imp = _dot_nt(overlap, psum, precision=lax.Precision.HIGHEST)
    blk = lax.broadcasted_iota(I32, (nblk, tq), 0)
    cur = (q0 + lax.broadcasted_iota(I32, (nblk, tq), 1)) >> SEL_SHIFT
    forced = (blk == 0) | (blk == cur) | (blk == cur - 1)
    score = jnp.where(blk <= cur, jnp.where(forced, FORCE, imp), NEG)
    rank = jnp.zeros((nblk, tq), I32)
    for j in range(nblk):
        row = score[j:j + 1, :]
        ahead = (row > score) | ((row == score) & (blk > j))
        rank = rank + ahead.astype(I32)
    sel = ((rank < N_SEL) & (blk <= cur)).astype(F32)
    sel = jnp.concatenate([sel, jnp.zeros((LANES - nblk, tq), F32)], axis=0)
    sel_ref[0, 0] = sel.T


def _cmp_attn(q, kc, vc, tq):
    b, _, s, _ = q.shape
    ncmp = kc.shape[2]
    cspec = pl.BlockSpec((1, 1, ncmp, LANES), lambda bi, gi, i: (bi, gi, 0, 0))
    return pl.pallas_call(
        functools.partial(_cmp_attn_kernel, tq=tq),
        grid=(b, NSA_GROUPS, s // tq),
        in_specs=[pl.BlockSpec((1, NSA_REP, tq, LANES), lambda bi, gi, i: (bi, gi, i, 0)), cspec, cspec],
        out_specs=[pl.BlockSpec((1, tq, NSA_REP * LANES), lambda bi, gi, i: (bi, i, gi)),
                   pl.BlockSpec((1, 1, tq, LANES), lambda bi, gi, i: (bi, gi, i, 0))],
        out_shape=[jax.ShapeDtypeStruct((b, s, NSA_HEADS * LANES), F32),
                   jax.ShapeDtypeStruct((b, NSA_GROUPS, s, LANES), F32)],
        compiler_params=_cparams("parallel", "parallel", "arbitrary"),
    )(q, kc, vc)


def _sel_attn_kernel(q_ref, k_ref, v_ref, sel_ref, o_ref, *, tq, tk):
    q0 = pl.program_id(2) * tq
    per_chunk = tk // SEL_LEN
    eb = lax.broadcasted_iota(I32, (LANES, tk), 0)
    ec = lax.broadcasted_iota(I32, (LANES, tk), 1)
    expand = ((ec >> SEL_SHIFT) == eb).astype(BF16)
    sel = sel_ref[0, 0]
    q = q_ref[0].reshape(NSA_REP * tq, LANES)
    kr, vr = k_ref.at[0, 0], v_ref.at[0, 0]

    def chosen(j):
        local = pltpu.roll(sel, (LANES - j * per_chunk) % LANES, 1).astype(BF16)
        return jnp.dot(local, expand, preferred_element_type=F32) > 0.5

    nfull = q0 // tk
    carry = lax.fori_loop(
        0, nfull, lambda j, c: _softmax_step(c, q, kr, vr, pl.multiple_of(j * tk, tk), tk, chosen(j)),
        _softmax_init(NSA_REP, tq))
    start = pl.multiple_of(nfull * tk, tk)
    causal = start + lax.broadcasted_iota(I32, (tq, tk), 1) <= q0 + lax.broadcasted_iota(I32, (tq, tk), 0)
    carry = _softmax_step(carry, q, kr, vr, start, tk, chosen(nfull) & causal)
    _store_heads(o_ref, _softmax_finish(carry), tq)


def _win_attn_kernel(q_ref, k_ref, v_ref, o_ref, *, tq, tk):
    q0 = pl.program_id(2) * tq
    start = pl.multiple_of(jnp.maximum(q0 + tq - tk, 0), tq)
    dist = (q0 + lax.broadcasted_iota(I32, (tq, tk), 0)) - (start + lax.broadcasted_iota(I32, (tq, tk), 1))
    q = q_ref[0].reshape(NSA_REP * tq, LANES)
    carry = _softmax_step(_softmax_init(NSA_REP, tq), q, k_ref.at[0, 0], v_ref.at[0, 0], start, tk,
                          (dist >= 0) & (dist < WIN))
    _store_heads(o_ref, _softmax_finish(carry), tq)


def _nsa_branch_attn(kernel, q, k, v, extra, tq, tk):
    b, _, s, _ = q.shape
    assert s % tq == 0 and tk % tq == 0 and s >= tk
    kv = pl.BlockSpec((1, 1, s, LANES), lambda bi, gi, i: (bi, gi, 0, 0))
    in_specs = [pl.BlockSpec((1, NSA_REP, tq, LANES), lambda bi, gi, i: (bi, gi, i, 0)), kv, kv]
    in_specs += [pl.BlockSpec((1, 1, tq, LANES), lambda bi, gi, i: (bi, gi, i, 0)) for _ in extra]
    return pl.pallas_call(
        functools.partial(kernel, tq=tq, tk=tk),
        grid=(b, NSA_GROUPS, s // tq),
        in_specs=in_specs,
        out_specs=pl.BlockSpec((1, tq, NSA_REP * LANES), lambda bi, gi, i: (bi, i, gi)),
        out_shape=jax.ShapeDtypeStruct((b, s, NSA_HEADS * LANES), F32),
        compiler_params=_cparams("parallel", "parallel", "arbitrary"),
    )(q, k, v, *extra)


def _mix_out_kernel(x_ref, om_ref, oc_ref, os_ref, ow_ref, gate_ref, gm_ref, gn_ref, w_ref, o_ref):
    gate = jax.nn.sigmoid(gate_ref[0])
    parts = []
    for h in range(NSA_HEADS):
        sl = slice(h * LANES, (h + 1) * LANES)
        g0 = gate[:, h:h + 1]
        g1 = gate[:, LANES + h:LANES + h + 1]
        g2 = gate[:, 2 * LANES + h:2 * LANES + h + 1]
        parts.append(g0 * oc_ref[0, :, sl] + g1 * os_ref[0, :, sl] + g2 * ow_ref[0, :, sl])
    o_nsa = jnp.concatenate(parts, axis=-1)
    width = MLA_HEADS * MLA_V
    mixed = jnp.concatenate([_rms(om_ref[0], gm_ref[...], width), _rms(o_nsa, gn_ref[...], width)], axis=-1)
    o_ref[0] = x_ref[0] + jnp.dot(mixed.astype(BF16), w_ref[...], preferred_element_type=F32)


def _mix_out(x, om, oc, os_, ow, y3, gm, gn, w, tm):
    b, s, d = x.shape
    wide = om.shape[-1]
    full = lambda shape: pl.BlockSpec(shape, lambda bi, i: (0,) * len(shape))
    row = lambda width: pl.BlockSpec((1, tm, width), lambda bi, i: (bi, i, 0))
    return pl.pallas_call(
        _mix_out_kernel,
        grid=(b, s // tm),
        in_specs=[row(d), row(wide), row(wide), row(wide), row(wide),
                  pl.BlockSpec((1, tm, 3 * LANES), lambda bi, i: (bi, i, COL_GATE // (3 * LANES))),
                  full((1, wide)), full((1, wide)), full(w.shape)],
        out_specs=row(d),
        out_shape=jax.ShapeDtypeStruct((b, s, d), F32),
        compiler_params=_cparams("parallel", "parallel"),
    )(x, om, oc, os_, ow, y3, gm, gn, w)


def _mem_kv_kernel(m_ref, g_ref, w_ref, gk_ref, k_ref, v_ref):
    mem = m_ref[0]
    n = _rms(mem, g_ref[...], mem.shape[-1]).astype(BF16)
    kv = jnp.dot(n, w_ref[...], preferred_element_type=F32)
    width = XA_HEADS * XA_DH
    for h in range(XA_HEADS):
        sl = slice(h * XA_DH, (h + 1) * XA_DH)
        k_ref[0, :, sl] = _rms(kv[:, sl], gk_ref[...], XA_DH).astype(BF16)
    v_ref[0] = kv[:, width:].astype(BF16)


def _mem_kv(mem, g, w, gk):
    b, m, d = mem.shape
    width = XA_HEADS * XA_DH
    full = lambda shape: pl.BlockSpec(shape, lambda bi: (0,) * len(shape))
    ospec = pl.BlockSpec((1, m, width), lambda bi: (bi, 0, 0))
    os_ = jax.ShapeDtypeStruct((b, m, width), BF16)
    return pl.pallas_call(
        _mem_kv_kernel,
        grid=(b,),
        in_specs=[pl.BlockSpec((1, m, d), lambda bi: (bi, 0, 0)), full((1, d)), full(w.shape), full((1, XA_DH))],
        out_specs=[ospec, ospec],
        out_shape=[os_, os_],
        compiler_params=_cparams("parallel"),
    )(mem, g, w, gk)


def _xattn_kernel(h_ref, g_ref, wq_ref, gq_ref, k_ref, v_ref, wo_ref, o_ref):
    h = h_ref[0]
    hn = _rms(h, g_ref[...], h.shape[-1]).astype(BF16)
    q = jnp.dot(hn, wq_ref[...], preferred_element_type=F32)
    outs = []
    for hd in range(XA_HEADS):
        sl = slice(hd * XA_DH, (hd + 1) * XA_DH)
        qh = (_rms(q[:, sl], gq_ref[...], XA_DH) * (XA_DH ** -0.5)).astype(BF16)
        s = _dot_nt(qh, k_ref[0, :, sl])
        p = jnp.exp(s - jnp.max(s, axis=-1, keepdims=True))
        p = p * (1.0 / jnp.sum(p, axis=-1, keepdims=True))
        outs.append(jnp.dot(p.astype(BF16), v_ref[0, :, sl], preferred_element_type=F32))
    o = jnp.concatenate(outs, axis=-1).astype(BF16)
    o_ref[0] = h + jnp.dot(o, wo_ref[...], preferred_element_type=F32)


def _xattn(h, g, wq, gq, k, v, wo, tm):
    b, s, d = h.shape
    m, width = k.shape[1], k.shape[2]
    full = lambda shape: pl.BlockSpec(shape, lambda bi, i: (0,) * len(shape))
    kv = pl.BlockSpec((1, m, width), lambda bi, i: (bi, 0, 0))
    row = pl.BlockSpec((1, tm, d), lambda bi, i: (bi, i, 0))
    return pl.pallas_call(
        _xattn_kernel,
        grid=(b, s // tm),
        in_specs=[row, full((1, d)), full(wq.shape), full((1, XA_DH)), kv, kv, full(wo.shape)],
        out_specs=row,
        out_shape=jax.ShapeDtypeStruct((b, s, d), F32),
        compiler_params=_cparams("parallel", "parallel"),
    )(h, g, wq, gq, k, v, wo)


def _peer_score_kernel(h_ref, g_ref, wq_ref, kbd_ref, hn_ref, s_ref):
    h = h_ref[...]
    hn = _rms(h, g_ref[...], h.shape[-1]).astype(BF16)
    hn_ref[...] = hn
    q = jnp.dot(hn, wq_ref[...], preferred_element_type=F32).astype(BF16)
    s_ref[...] = _dot_nt(kbd_ref[...], q)


def _peer_score(h2d, g, wq, kbd, tm):
    t, d = h2d.shape
    nk = kbd.shape[0]
    full = lambda shape: pl.BlockSpec(shape, lambda i: (0,) * len(shape))
    return pl.pallas_call(
        _peer_score_kernel,
        grid=(t // tm,),
        in_specs=[pl.BlockSpec((tm, d), lambda i: (i, 0)), full((1, d)), full(wq.shape), full(kbd.shape)],
        out_specs=[pl.BlockSpec((tm, d), lambda i: (i, 0)), pl.BlockSpec((nk, tm), lambda i: (0, i))],
        out_shape=[jax.ShapeDtypeStruct((t, d), BF16), jax.ShapeDtypeStruct((nk, t), F32)],
        compiler_params=_cparams("parallel"),
    )(h2d, g, wq, kbd)


def _topk_rows(x, k, ids, payload=None):
    vals, outs = [], []
    for _ in range(k):
        m = jnp.max(x, axis=0, keepdims=True)
        idx = jnp.min(jnp.where(x == m, ids, jnp.iinfo(jnp.int32).max), axis=0, keepdims=True)
        hit = ids == idx
        vals.append(m)
        if payload is None:
            outs.append(idx)
        else:
            outs.append(jnp.max(jnp.where(hit, payload, -1), axis=0, keepdims=True))
        x = jnp.where(hit, -jnp.inf, x)
    return jnp.concatenate(vals, axis=0), jnp.concatenate(outs, axis=0)


PEER_EDGE = 4
assert (PEER_EDGE + 1) ** 2 > PEER_TOPK


def _peer_route_kernel(s_ref, e_ref, g_ref, e_scr, g_scr):
    tm = s_ref.shape[1]
    key_ids = lax.broadcasted_iota(I32, (PEER_KEYS, tm), 0)
    a_ids = lax.broadcasted_iota(I32, (PEER_TOPK, tm), 0)

    def head(h, carry):
        base = pl.multiple_of(h * (2 * PEER_KEYS), 2 * PEER_KEYS)
        s1, i1 = _topk_rows(s_ref[pl.ds(base, PEER_KEYS), :], PEER_TOPK, key_ids)
        s2, i2 = _topk_rows(s_ref[pl.ds(base + PEER_KEYS, PEER_KEYS), :], PEER_TOPK, key_ids)
        i1 = i1 * PEER_KEYS
        cand, flat, cidx = [], [], []
        for a in range(PEER_EDGE):
            cand.append(s1[a:a + 1, :] + s2)
            flat.append(a * PEER_TOPK + a_ids)
            cidx.append(i1[a:a + 1, :] + i2)
        for b in range(PEER_EDGE):
            cand.append(jnp.where(a_ids >= PEER_EDGE, s1 + s2[b:b + 1, :], -jnp.inf))
            flat.append(a_ids * PEER_TOPK + b)
            cidx.append(i1 + i2[b:b + 1, :])
        top, eidx = _topk_rows(jnp.concatenate(cand, axis=0), PEER_TOPK, jnp.concatenate(flat, axis=0),
                               payload=jnp.concatenate(cidx, axis=0))
        p = jnp.exp(top - top[0:1, :])
        p = p * (1.0 / jnp.sum(p, axis=0, keepdims=True))
        row = pl.multiple_of(h * PEER_TOPK, PEER_TOPK)
        e_scr[pl.ds(row, PEER_TOPK), :] = eidx * PEER_WORD_ROWS
        g_scr[pl.ds(row, PEER_TOPK), :] = p
        return carry

    lax.fori_loop(0, PEER_HEADS, head, 0)
    e_ref[...] = e_scr[...].T
    g_ref[...] = g_scr[...].T


def _peer_route(st):
    nk, t = st.shape
    tm = LANES
    ospec = pl.BlockSpec((tm, PEER_PICKS), lambda i: (i, 0))
    return pl.pallas_call(
        _peer_route_kernel,
        grid=(t // tm,),
        in_specs=[pl.BlockSpec((nk, tm), lambda i: (0, i))],
        out_specs=[ospec, ospec],
        out_shape=[jax.ShapeDtypeStruct((t, PEER_PICKS), I32), jax.ShapeDtypeStruct((t, PEER_PICKS), F32)],
        scratch_shapes=[pltpu.VMEM((PEER_PICKS, tm), I32), pltpu.VMEM((PEER_PICKS, tm), F32)],
        compiler_params=_cparams("parallel"),
    )(st)


def _gather_rows(idx_ref, tab_ref, tile_ref, j):
    for k in range(PEER_PICKS):
        r = pl.multiple_of(idx_ref[j, k], PEER_WORD_ROWS)
        tile_ref[pl.ds(k, PEER_WORD_ROWS, stride=PEER_CHUNK_STRIDE), :] = tab_ref[pl.ds(r, PEER_WORD_ROWS), :]


def _tile_rows(tile_ref):
    chunks = [pltpu.bitcast(tile_ref[pl.ds(c * PEER_CHUNK_STRIDE, PEER_PICKS), :], BF16)
              for c in range(PEER_WORD_ROWS)]
    return jnp.concatenate(chunks, axis=1)


def _token_loop(idx_ref, tab_ref, tile_a, tile_b, compute):
    _gather_rows(idx_ref, tab_ref, tile_a, 0)

    def group(i, carry):
        for u in range(PEER_UNROLL):
            j = i * PEER_UNROLL + u
            cur, nxt = (tile_a, tile_b) if u % 2 == 0 else (tile_b, tile_a)
            rows = _tile_rows(cur)
            _gather_rows(idx_ref, tab_ref, nxt, jnp.minimum(j + 1, PEER_TOK - 1))
            compute(j, rows)
        return carry

    lax.fori_loop(0, PEER_TOK // PEER_UNROLL, group, 0)


def _pair_matrix():
    r = lax.broadcasted_iota(I32, (2 * PEER_PICKS, PEER_PICKS), 0)
    c = lax.broadcasted_iota(I32, (2 * PEER_PICKS, PEER_PICKS), 1)
    return (r >> 1) == c


def _peer_up_kernel(idx_ref, x_ref, g_ref, tab_ref, w_ref, tile_a, tile_b, act_ref, xf_ref):
    half = x_ref.shape[1] // 2
    sub = lax.broadcasted_iota(I32, (16, half), 0)
    odd_lane = (lax.broadcasted_iota(I32, (1, 2 * PEER_PICKS), 1) & 1) == 1
    xf_ref[...] = x_ref[...].astype(F32)

    def compute(j, rows):
        xr = xf_ref[pl.ds(j, 1), :]
        lo = jnp.broadcast_to(xr[:, :half], (16, half))
        hi = jnp.broadcast_to(xr[:, half:], (16, half))
        xsel = jnp.where(sub == 0, lo, jnp.where(sub == 1, hi, 0.0)).astype(BF16)
        r = _dot_nt(xsel, rows)
        act_ref[pl.ds(j, 1), :] = jnp.where(odd_lane, r[1:2, :], r[0:1, :])

    _token_loop(idx_ref, tab_ref, tile_a, tile_b, compute)
    pair = _pair_matrix()
    act = jnp.dot(act_ref[...], pair.astype(F32), preferred_element_type=F32,
                  precision=lax.Precision.HIGHEST)
    w = (g_ref[...] * _gelu(act)).astype(BF16)
    w_ref[...] = _dot_nt(w, pair.astype(BF16))


def _peer_down_kernel(idx_ref, w_ref, h_ref, tab_ref, o_ref, tile_a, tile_b):
    half = h_ref.shape[1] // 2
    shape = (16, 2 * PEER_PICKS)
    parity = (lax.broadcasted_iota(I32, shape, 1) & 1) == lax.broadcasted_iota(I32, shape, 0)

    def compute(j, rows):
        wrow = jnp.broadcast_to(w_ref[pl.ds(j, 1), :], shape)
        wm = jnp.where(parity, wrow, 0.0).astype(BF16)
        o2 = jnp.dot(wm, rows, preferred_element_type=F32)
        o_ref[pl.ds(j, 1), 0:half] = h_ref[pl.ds(j, 1), 0:half] + o2[0:1, :]
        o_ref[pl.ds(j, 1), half:] = h_ref[pl.ds(j, 1), half:] + o2[1:2, :]

    _token_loop(idx_ref, tab_ref, tile_a, tile_b, compute)


def _tile_scratch():
    return pltpu.VMEM((PEER_WORD_ROWS * PEER_CHUNK_STRIDE, LANES), I32)


def _table_spec(tab):
    return pl.BlockSpec(tab.shape, lambda c: (0, 0), pipeline_mode=pl.Buffered(1))


def _idx_spec():
    return pl.BlockSpec((PEER_TOK, PEER_PICKS), lambda c: (c, 0), memory_space=pltpu.SMEM)


def _peer_up(eidx, hn, gw, tab):
    t, d = hn.shape
    return pl.pallas_call(
        _peer_up_kernel,
        grid=(t // PEER_TOK,),
        in_specs=[_idx_spec(),
                  pl.BlockSpec((PEER_TOK, d), lambda c: (c, 0)),
                  pl.BlockSpec((PEER_TOK, PEER_PICKS), lambda c: (c, 0)),
                  _table_spec(tab)],
        out_specs=pl.BlockSpec((PEER_TOK, 2 * PEER_PICKS), lambda c: (c, 0)),
        out_shape=jax.ShapeDtypeStruct((t, 2 * PEER_PICKS), F32),
        scratch_shapes=[_tile_scratch(), _tile_scratch(), pltpu.VMEM((PEER_TOK, 2 * PEER_PICKS), F32),
                        pltpu.VMEM((PEER_TOK, d), F32)],
        compiler_params=_cparams("arbitrary"),
    )(eidx, hn, gw, tab)


def _peer_down(eidx, w, h2d, tab):
    t, d = h2d.shape
    return pl.pallas_call(
        _peer_down_kernel,
        grid=(t // PEER_TOK,),
        in_specs=[_idx_spec(),
                  pl.BlockSpec((PEER_TOK, 2 * PEER_PICKS), lambda c: (c, 0)),
                  pl.BlockSpec((PEER_TOK, d), lambda c: (c, 0)),
                  _table_spec(tab)],
        out_specs=pl.BlockSpec((PEER_TOK, d), lambda c: (c, 0)),
        out_shape=jax.ShapeDtypeStruct((t, d), F32),
        scratch_shapes=[_tile_scratch(), _tile_scratch()],
        compiler_params=_cparams("arbitrary"),
    )(eidx, w, h2d, tab)


def _pad_heads(w, heads, dim, dim_pad=LANES):
    lead = w.shape[:-1]
    w = w.reshape(lead + (heads, dim))
    w = jnp.pad(w, [(0, 0)] * len(lead) + [(0, 0), (0, dim_pad - dim)])
    return w.reshape(lead + (heads * dim_pad,))


def _pad_lanes(v, offset=0, width=LANES):
    return jnp.pad(v, [(0, 0)] * (v.ndim - 1) + [(offset, width - offset - v.shape[-1])])


def _rope_tabs(pos, rot_dim, offset):
    inv = 1.0 / (ROPE_THETA ** (jnp.arange(0, rot_dim, 2, dtype=F32) / rot_dim))
    ang = pos.astype(F32)[:, None] * inv[None, :]
    cos, sin = jnp.cos(ang), jnp.sin(ang)
    zero = jnp.zeros_like(sin)
    c = jnp.pad(jnp.concatenate([cos, cos], -1) - 1.0, ((0, 0), (offset, LANES - offset - rot_dim))) + 1.0
    sa = _pad_lanes(jnp.concatenate([-sin, zero], -1), offset)
    sb = _pad_lanes(jnp.concatenate([zero, sin], -1), offset)
    return c, sa, sb


def _pack_table(tab):
    e, d = tab.shape
    bits = lax.bitcast_convert_type(tab.astype(BF16), jnp.uint16).astype(jnp.uint32)
    word = bits[:, :d // 2] | (bits[:, d // 2:] << 16)
    return lax.bitcast_convert_type(word, I32).reshape(e * PEER_WORD_ROWS, LANES)


def _relayout_w_in(w_in):
    d = w_in.shape[0]
    o = 0
    seg = {}
    for name, size in (("cq", 256), ("ckv", 128), ("kpe", 32), ("qn", 512), ("kc", 128), ("vc", 128),
                       ("ks", 128), ("vs", 128), ("kw", 128), ("vw", 128), ("gate", 24)):
        seg[name] = w_in[:, o:o + size]
        o += size
    gates = [_pad_lanes(seg["gate"][:, j::3]) for j in range(3)]
    cols = [_pad_heads(seg["qn"], NSA_HEADS, NSA_DK),
            _pad_heads(seg["ks"], NSA_GROUPS, NSA_DK), _pad_heads(seg["kw"], NSA_GROUPS, NSA_DK),
            seg["cq"], seg["ckv"], _pad_lanes(seg["kpe"], MLA_NOPE),
            _pad_heads(seg["vs"], NSA_GROUPS, NSA_DK), _pad_heads(seg["vw"], NSA_GROUPS, NSA_DK),
            seg["kc"]] + gates + [seg["vc"]]
    w = jnp.concatenate(cols, axis=1)
    assert w.shape == (d, IN_COLS_PAD)
    return w.astype(BF16)


def _layer(h, mem, p):
    b, s, d = h.shape
    t = b * s
    assert s % SEL_CHUNK == 0 and N_SEL <= s // SEL_LEN <= SEL_BLOCKS_PAD and d == 1024
    pos = jnp.arange(s)

    y = _norm_matmul(h.reshape(t, d), p["mix_norm"], _relayout_w_in(p["w_in"]), 512)
    y3 = y.reshape(b, s, IN_COLS_PAD)

    wuq = _pad_heads(p["mla_w_uq"], MLA_HEADS, MLA_QK).astype(BF16)
    wukv = p["mla_w_ukv"].reshape(MLA_KV_LORA, MLA_HEADS, MLA_NOPE + MLA_V)
    wuk = _pad_heads(wukv[..., :MLA_NOPE].reshape(MLA_KV_LORA, -1), MLA_HEADS, MLA_NOPE).astype(BF16)
    wuv = _pad_heads(wukv[..., MLA_NOPE:].reshape(MLA_KV_LORA, -1), MLA_HEADS, MLA_V).astype(BF16)
    q, k, v = _mla_prep(y3, p["mla_q_norm"][None], p["mla_kv_norm"][None], _pad_lanes(p["mla_qk_norm"]),
                        wuq, wuk, wuv, _rope_tabs(pos, MLA_ROPE, MLA_NOPE), 512)
    o_mla = _mla_attn(q, k, v, 512)

    tabs = _rope_tabs(pos, NSA_ROT, 0)
    qn, ks, kw, vs, vw = _nsa_prep(y3, _pad_lanes(p["nsa_q_norm"][None]), _pad_lanes(p["nsa_k_norm"][1:3]),
                                   tabs, 512)
    nrow = s // CMP_STRIDE

    def blocks16(col):
        tkv = y3[:, :, col:col + NSA_GROUPS * NSA_DK].reshape(b, nrow, CMP_STRIDE, NSA_GROUPS, NSA_DK)
        return tkv.transpose(0, 3, 1, 2, 4).reshape(b, NSA_GROUPS, nrow, CMP_STRIDE * NSA_DK)

    pe = p["nsa_cmp_pe"].reshape(2, 2, CMP_STRIDE * NSA_DK)
    w1 = p["nsa_cmp_w1"].reshape(2, 2, CMP_STRIDE * NSA_DK, -1).astype(BF16)
    w2 = _pad_lanes(p["nsa_cmp_w2"]).astype(BF16)
    cmp_end = jnp.arange(nrow) * CMP_STRIDE + CMP_LEN - 1
    kc, vc = _compress(blocks16(COL_KC), blocks16(COL_VC), pe, w1, w2, _pad_lanes(p["nsa_k_norm"][0:1]),
                       _rope_tabs(cmp_end, NSA_ROT, 0))
    o_cmp, sel = _cmp_attn(qn, kc, vc, 128)
    o_sel = _nsa_branch_attn(_sel_attn_kernel, qn, ks, vs, (sel,), NSA_TQ, SEL_CHUNK)
    o_win = _nsa_branch_attn(_win_attn_kernel, qn, kw, vw, (), NSA_TQ, WIN + NSA_TQ)

    width = MLA_HEADS * MLA_V
    g_out = p["mix_out_norm"]
    gm = _pad_heads(g_out[None, :width], MLA_HEADS, MLA_V)
    gn = _pad_heads(g_out[None, width:], NSA_HEADS, NSA_DK)
    w_out = p["w_out"].reshape(2 * MLA_HEADS, MLA_V, d)
    w_out = jnp.pad(w_out, ((0, 0), (0, LANES - MLA_V), (0, 0))).reshape(2 * MLA_HEADS * LANES, d).astype(BF16)
    h = _mix_out(h, o_mla, o_cmp, o_sel, o_win, y3, gm, gn, w_out, 256)

    kx, vx = _mem_kv(mem, p["mem_norm"][None], p["xa_wkv"].astype(BF16), p["xa_qk_norm"][1:2])
    h = _xattn(h, p["xa_norm"][None], p["xa_wq"].astype(BF16), p["xa_qk_norm"][0:1], kx, vx,
               p["xa_wo"].astype(BF16), 256)

    keys = p["peer_keys"]
    nhp = PEER_HEADS * 2
    eye = jnp.eye(nhp, dtype=F32)
    kbd = (keys.reshape(nhp, PEER_KEYS, PEER_HALF)[:, :, None, :] * eye[:, None, :, None])
    kbd = kbd.reshape(nhp * PEER_KEYS, nhp * PEER_HALF).astype(BF16)
    h2d = h.reshape(t, d)
    hn, st = _peer_score(h2d, p["ffn_norm"][None], p["peer_wq"].astype(BF16), kbd, 512)
    eidx, gw = _peer_route(st)
    w = _peer_up(eidx, hn, gw, _pack_table(p["peer_u"]))
    out = _peer_down(eidx, w, h2d, _pack_table(p["peer_v"]))
    return out.reshape(b, s, d)


def kernel(x, mem, mix_norm, w_in, mla_q_norm, mla_kv_norm, mla_w_uq, mla_w_ukv, mla_qk_norm, nsa_q_norm, nsa_k_norm, nsa_cmp_pe, nsa_cmp_w1, nsa_cmp_w2, mix_out_norm, w_out, xa_norm, mem_norm, xa_wq, xa_wkv, xa_qk_norm, xa_wo, ffn_norm, peer_wq, peer_keys, peer_u, peer_v):
    params = dict(mix_norm=mix_norm, w_in=w_in, mla_q_norm=mla_q_norm, mla_kv_norm=mla_kv_norm,
                  mla_w_uq=mla_w_uq, mla_w_ukv=mla_w_ukv, mla_qk_norm=mla_qk_norm, nsa_q_norm=nsa_q_norm,
                  nsa_k_norm=nsa_k_norm, nsa_cmp_pe=nsa_cmp_pe, nsa_cmp_w1=nsa_cmp_w1, nsa_cmp_w2=nsa_cmp_w2,
                  mix_out_norm=mix_out_norm, w_out=w_out, xa_norm=xa_norm, mem_norm=mem_norm, xa_wq=xa_wq,
                  xa_wkv=xa_wkv, xa_qk_norm=xa_qk_norm, xa_wo=xa_wo, ffn_norm=ffn_norm, peer_wq=peer_wq,
                  peer_keys=peer_keys, peer_u=peer_u, peer_v=peer_v)
    h = x
    for layer in range(w_in.shape[0]):
        h = _layer(h, mem, {name: val[layer] for name, val in params.items()})
    return h
```

```python
import functools

import jax
import jax.numpy as jnp
from jax import lax
from jax.experimental import pallas as pl
from jax.experimental.pallas import tpu as pltpu

F32 = jnp.float32
BF16 = jnp.bfloat16
I32 = jnp.int32

LANES = 128
VMEM_LIMIT = 56 * 1024 * 1024

ROPE_THETA = 500000.0
EPS = 1e-6
NEG = -1e30
FORCE = 1e4

MLA_HEADS = 8
MLA_Q_LORA = 256
MLA_KV_LORA = 128
MLA_NOPE = 64
MLA_ROPE = 32
MLA_V = 64
MLA_QK = MLA_NOPE + MLA_ROPE

NSA_HEADS = 8
NSA_GROUPS = 2
NSA_REP = NSA_HEADS // NSA_GROUPS
NSA_DK = 64
NSA_ROT = NSA_DK // 4
CMP_LEN = 32
CMP_STRIDE = 16
SEL_LEN = 64
SEL_SHIFT = SEL_LEN.bit_length() - 1
SEL_BLOCKS_PAD = 64
N_SEL = 16
WIN = 512
SEL_CHUNK = 1024
NSA_TQ = 128
LOG2E = 1.4426950408889634

XA_HEADS = 4
XA_DH = 128

PEER_HEADS = 8
PEER_KEYS = 128
PEER_HALF = 64
PEER_TOPK = 16
PEER_PICKS = PEER_HEADS * PEER_TOPK
PEER_TOK = 128
PEER_WORD_ROWS = 4
PEER_CHUNK_STRIDE = 136
PEER_UNROLL = 16

COL_QN = 0
COL_KS = 1024
COL_KW = 1280
COL_CQ = 1536
COL_CKV = 1792
COL_KPE = 1920
COL_VS = 2048
COL_VW = 2304
COL_KC = 2560
COL_GATE = 2688
COL_VC = 3072
IN_COLS_PAD = 3200


def _cparams(*sem):
    return pltpu.CompilerParams(dimension_semantics=sem, vmem_limit_bytes=VMEM_LIMIT)


def _rms(x, g, n):
    ms = jnp.sum(x * x, axis=-1, keepdims=True) * (1.0 / n)
    return x * lax.rsqrt(ms + EPS) * g


def _rope(x, c, sa, sb, half):
    return x * c + pltpu.roll(x, LANES - half, 1) * sa + pltpu.roll(x, half, 1) * sb


def _gelu(x):
    return 0.5 * x * (1.0 + jnp.tanh(0.7978845608028654 * (x + 0.044715 * (x * x * x))))


def _dot_nt(a, b, **kw):
    return lax.dot_general(a, b, (((1,), (1,)), ((), ())), preferred_element_type=F32, **kw)


def _norm_matmul_kernel(x_ref, g_ref, w_ref, o_ref):
    x = x_ref[...]
    n = _rms(x, g_ref[...], x.shape[-1])
    o_ref[...] = jnp.dot(n.astype(BF16), w_ref[...], preferred_element_type=F32)


def _norm_matmul(x2d, g, w, tm):
    t, d = x2d.shape
    n = w.shape[1]
    return pl.pallas_call(
        _norm_matmul_kernel,
        grid=(t // tm,),
        in_specs=[pl.BlockSpec((tm, d), lambda i: (i, 0)),
                  pl.BlockSpec((1, d), lambda i: (0, 0)),
                  pl.BlockSpec((d, n), lambda i: (0, 0))],
        out_specs=pl.BlockSpec((tm, n), lambda i: (i, 0)),
        out_shape=jax.ShapeDtypeStruct((t, n), F32),
        compiler_params=_cparams("parallel"),
    )(x2d, g.reshape(1, d), w)


def _mla_prep_kernel(y_ref, gq_ref, gkv_ref, gqk_ref, wuq_ref, wuk_ref, wuv_ref,
                     c_ref, sa_ref, sb_ref, q_ref, k_ref, v_ref):
    y = y_ref[0]
    cq = y[:, 0:MLA_Q_LORA]
    ckv = y[:, MLA_Q_LORA:MLA_Q_LORA + MLA_KV_LORA]
    kpe = y[:, MLA_Q_LORA + MLA_KV_LORA:]
    nq = _rms(cq, gq_ref[...], MLA_Q_LORA).astype(BF16)
    nkv = _rms(ckv, gkv_ref[...], MLA_KV_LORA).astype(BF16)
    qf = jnp.dot(nq, wuq_ref[...], preferred_element_type=F32)
    kf = jnp.dot(nkv, wuk_ref[...], preferred_element_type=F32)
    vf = jnp.dot(nkv, wuv_ref[...], preferred_element_type=F32)
    c, sa, sb = c_ref[...], sa_ref[...], sb_ref[...]
    gq = gqk_ref[0:1, :]
    gk = gqk_ref[1:2, :]
    scale = MLA_QK ** -0.5 * LOG2E
    for h in range(MLA_HEADS):
        sl = slice(h * LANES, (h + 1) * LANES)
        qh = _rope(_rms(qf[:, sl], gq, MLA_QK), c, sa, sb, MLA_ROPE // 2)
        kh = _rope(_rms(kf[:, sl] + kpe, gk, MLA_QK), c, sa, sb, MLA_ROPE // 2)
        q_ref[0, h] = (qh * scale).astype(BF16)
        k_ref[0, h] = kh.astype(BF16)
        v_ref[0, h] = vf[:, sl].astype(BF16)


def _mla_prep(y3, gq, gkv, gqk, wuq, wuk, wuv, tabs, tm):
    b, s, _ = y3.shape
    hs = jax.ShapeDtypeStruct((b, MLA_HEADS, s, LANES), BF16)
    full = lambda shape: pl.BlockSpec(shape, lambda bi, i: (0,) * len(shape))
    tab = pl.BlockSpec((tm, LANES), lambda bi, i: (i, 0))
    hspec = pl.BlockSpec((1, MLA_HEADS, tm, LANES), lambda bi, i: (bi, 0, i, 0))
    return pl.pallas_call(
        _mla_prep_kernel,
        grid=(b, s // tm),
        in_specs=[pl.BlockSpec((1, tm, 512), lambda bi, i: (bi, i, COL_CQ // 512)),
                  full((1, MLA_Q_LORA)), full((1, MLA_KV_LORA)), full((2, LANES)),
                  full((MLA_Q_LORA, MLA_HEADS * LANES)), full((MLA_KV_LORA, MLA_HEADS * LANES)),
                  full((MLA_KV_LORA, MLA_HEADS * LANES)), tab, tab, tab],
        out_specs=[hspec, hspec, hspec],
        out_shape=[hs, hs, hs],
        compiler_params=_cparams("parallel", "parallel"),
    )(y3, gq, gkv, gqk, wuq, wuk, wuv, *tabs)


def _nsa_prep_kernel(yq_ref, yv_ref, gq_ref, gk_ref, c_ref, sa_ref, sb_ref,
                     q_ref, ks_ref, kw_ref, vs_ref, vw_ref):
    c, sa, sb = c_ref[...], sa_ref[...], sb_ref[...]
    half = NSA_ROT // 2
    scale = NSA_DK ** -0.5 * LOG2E
    yq = yq_ref[0]
    for h in range(NSA_HEADS):
        qh = _rms(yq[:, h * LANES:(h + 1) * LANES], gq_ref[...], NSA_DK)
        q_ref[0, h] = (_rope(qh, c, sa, sb, half) * scale).astype(BF16)
    for g in range(NSA_GROUPS):
        ks = yq[:, COL_KS + g * LANES:COL_KS + (g + 1) * LANES]
        kw = yq[:, COL_KW + g * LANES:COL_KW + (g + 1) * LANES]
        ks_ref[0, g] = _rope(_rms(ks, gk_ref[0:1, :], NSA_DK), c, sa, sb, half).astype(BF16)
        kw_ref[0, g] = _rope(_rms(kw, gk_ref[1:2, :], NSA_DK), c, sa, sb, half).astype(BF16)
    yv = yv_ref[0]
    for g in range(NSA_GROUPS):
        vs_ref[0, g] = yv[:, g * LANES:(g + 1) * LANES].astype(BF16)
        vw_ref[0, g] = yv[:, (NSA_GROUPS + g) * LANES:(NSA_GROUPS + g + 1) * LANES].astype(BF16)


def _nsa_prep(y3, gq, gk, tabs, tm):
    b, s, _ = y3.shape
    full = lambda shape: pl.BlockSpec(shape, lambda bi, i: (0,) * len(shape))
    tab = pl.BlockSpec((tm, LANES), lambda bi, i: (i, 0))
    qs = jax.ShapeDtypeStruct((b, NSA_HEADS, s, LANES), BF16)
    gs = jax.ShapeDtypeStruct((b, NSA_GROUPS, s, LANES), BF16)
    gspec = pl.BlockSpec((1, NSA_GROUPS, tm, LANES), lambda bi, i: (bi, 0, i, 0))
    return pl.pallas_call(
        _nsa_prep_kernel,
        grid=(b, s // tm),
        in_specs=[pl.BlockSpec((1, tm, 1536), lambda bi, i: (bi, i, 0)),
                  pl.BlockSpec((1, tm, 512), lambda bi, i: (bi, i, COL_VS // 512)),
                  full((1, LANES)), full((2, LANES)), tab, tab, tab],
        out_specs=[pl.BlockSpec((1, NSA_HEADS, tm, LANES), lambda bi, i: (bi, 0, i, 0)),
                   gspec, gspec, gspec, gspec],
        out_shape=[qs, gs, gs, gs, gs],
        compiler_params=_cparams("parallel", "parallel"),
    )(y3, y3, gq, gk, *tabs)


def _compress_kernel(tk_ref, tv_ref, pe_ref, w1_ref, w2_ref, gk_ref, c_ref, sa_ref, sb_ref,
                     kc_ref, vc_ref):
    nrow = tk_ref.shape[2]

    def mlp(t, kv):
        ta = (t + pe_ref[kv, 0:1, :]).astype(BF16)
        tb = (t + pe_ref[kv, 1:2, :]).astype(BF16)
        za = jnp.dot(ta, w1_ref[kv, 0], preferred_element_type=F32)
        zb = jnp.dot(tb, w1_ref[kv, 1], preferred_element_type=F32)
        hid = _gelu(za + pltpu.roll(zb, nrow - 1, 0))
        return jnp.dot(hid.astype(BF16), w2_ref[kv], preferred_element_type=F32)

    kc = mlp(tk_ref[0, 0], 0)
    kc = _rope(_rms(kc, gk_ref[...], NSA_DK), c_ref[...], sa_ref[...], sb_ref[...], NSA_ROT // 2)
    kc_ref[0, 0] = kc.astype(BF16)
    vc_ref[0, 0] = mlp(tv_ref[0, 0], 1).astype(BF16)


def _compress(tk16, tv16, pe, w1, w2, gk, tabs):
    b, g, nrow, width = tk16.shape
    full = lambda shape: pl.BlockSpec(shape, lambda bi, gi: (0,) * len(shape))
    tspec = pl.BlockSpec((1, 1, nrow, width), lambda bi, gi: (bi, gi, 0, 0))
    ospec = pl.BlockSpec((1, 1, nrow, LANES), lambda bi, gi: (bi, gi, 0, 0))
    os_ = jax.ShapeDtypeStruct((b, g, nrow, LANES), BF16)
    return pl.pallas_call(
        _compress_kernel,
        grid=(b, g),
        in_specs=[tspec, tspec, full(pe.shape), full(w1.shape), full(w2.shape), full((1, LANES)),
                  full((nrow, LANES)), full((nrow, LANES)), full((nrow, LANES))],
        out_specs=[ospec, ospec],
        out_shape=[os_, os_],
        compiler_params=_cparams("parallel", "parallel"),
    )(tk16, tv16, pe, w1, w2, gk, *tabs)


def _softmax_init(heads, tq):
    return (jnp.full((heads, tq, 1), NEG, F32), jnp.zeros((heads, tq, 1), F32),
            jnp.zeros((heads * tq, LANES), F32))


def _softmax_step(carry, q, k_ref, v_ref, start, width, mask):
    m, l, acc = carry
    heads, tq, _ = m.shape
    k = k_ref[pl.ds(start, width), :]
    v = v_ref[pl.ds(start, width), :]
    s = _dot_nt(q, k).reshape(heads, tq, width)
    if mask is not None:
        s = jnp.where(mask[None], s, NEG)
    m_new = jnp.maximum(m, jnp.max(s, axis=-1, keepdims=True))
    p = jnp.exp2(s - m_new)
    alpha = jnp.exp2(m - m_new)
    l = alpha * l + jnp.sum(p, axis=-1, keepdims=True)
    pv = jnp.dot(p.reshape(heads * tq, width).astype(BF16), v, preferred_element_type=F32)
    return m_new, l, alpha.reshape(heads * tq, 1) * acc + pv


def _softmax_finish(carry):
    _, l, acc = carry
    return acc * (1.0 / l).reshape(acc.shape[0], 1)


def _mla_attn_kernel(q_ref, k_ref, v_ref, o_ref, *, tq):
    q0 = pl.program_id(2) * tq
    q = q_ref[0, 0]
    kr, vr = k_ref.at[0, 0], v_ref.at[0, 0]
    wide = 2 * tq
    nwide = q0 // wide
    carry = lax.fori_loop(
        0, nwide, lambda j, c: _softmax_step(c, q, kr, vr, pl.multiple_of(j * wide, wide), wide, None),
        _softmax_init(1, tq))
    carry = lax.fori_loop(
        0, (q0 - nwide * wide) // tq,
        lambda j, c: _softmax_step(c, q, kr, vr, pl.multiple_of(nwide * wide, tq), tq, None), carry)
    causal = lax.broadcasted_iota(I32, (tq, tq), 1) <= lax.broadcasted_iota(I32, (tq, tq), 0)
    carry = _softmax_step(carry, q, kr, vr, pl.multiple_of(q0, tq), tq, causal)
    o_ref[0] = _softmax_finish(carry)


def _mla_attn(q, k, v, tq):
    b, h, s, _ = q.shape
    kv = pl.BlockSpec((1, 1, s, LANES), lambda bi, hi, i: (bi, hi, 0, 0))
    return pl.pallas_call(
        functools.partial(_mla_attn_kernel, tq=tq),
        grid=(b, h, s // tq),
        in_specs=[pl.BlockSpec((1, 1, tq, LANES), lambda bi, hi, i: (bi, hi, i, 0)), kv, kv],
        out_specs=pl.BlockSpec((1, tq, LANES), lambda bi, hi, i: (bi, i, hi)),
        out_shape=jax.ShapeDtypeStruct((b, s, h * LANES), F32),
        compiler_params=_cparams("parallel", "parallel", "arbitrary"),
    )(q, k, v)


def _store_heads(o_ref, o, tq):
    for r in range(NSA_REP):
        o_ref[0, :, r * LANES:(r + 1) * LANES] = o[r * tq:(r + 1) * tq]


def _cmp_attn_kernel(q_ref, kc_ref, vc_ref, o_ref, sel_ref, *, tq):
    ncmp = kc_ref.shape[2]
    q0 = pl.program_id(2) * tq
    q = q_ref[0].reshape(NSA_REP * tq, LANES)
    s = _dot_nt(q, kc_ref[0, 0]).reshape(NSA_REP, tq, ncmp)
    qpos = q0 + lax.broadcasted_iota(I32, (tq, ncmp), 0)
    n = lax.broadcasted_iota(I32, (tq, ncmp), 1)
    msk = (n * CMP_STRIDE + (CMP_LEN - 1) <= qpos)[None]
    s = jnp.where(msk, s, NEG)
    m = jnp.max(s, axis=-1, keepdims=True)
    p = jnp.where(msk, jnp.exp2(s - m), 0.0)
    l = jnp.sum(p, axis=-1, keepdims=True)
    p = p * jnp.where(l > 0.0, 1.0 / l, 0.0)
    o = jnp.dot(p.reshape(NSA_REP * tq, ncmp).astype(BF16), vc_ref[0, 0], preferred_element_type=F32)
    _store_heads(o_ref, o, tq)

    nblk = SEL_BLOCKS_PAD
    psum = p[0] + p[1] + p[2] + p[3]
    bm = lax.broadcasted_iota(I32, (nblk, ncmp), 0)
    bn = lax.broadcasted_iota(I32, (nblk, ncmp), 1)
    overlap = ((bn * CMP_STRIDE < bm * SEL_LEN + SEL_LEN)
               & (bn * CMP_STRIDE + CMP_LEN > bm * SEL_LEN)).astype(F32)
    imp = _dot_nt(overlap, psum, precision=lax.Precision.HIGHEST)
    blk = lax.broadcasted_iota(I32, (nblk, tq), 0)
    cur = (q0 + lax.broadcasted_iota(I32, (nblk, tq), 1)) >> SEL_SHIFT
    forced = (blk == 0) | (blk == cur) | (blk == cur - 1)
    score = jnp.where(blk <= cur, jnp.where(forced, FORCE, imp), NEG)
    rank = jnp.zeros((nblk, tq), I32)
    for j in range(nblk):
        row = score[j:j + 1, :]
        ahead = (row > score) | ((row == score) & (blk > j))
        rank = rank + ahead.astype(I32)
    sel = ((rank < N_SEL) & (blk <= cur)).astype(F32)
    sel = jnp.concatenate([sel, jnp.zeros((LANES - nblk, tq), F32)], axis=0)
    sel_ref[0, 0] = sel.T


def _cmp_attn(q, kc, vc, tq):
    b, _, s, _ = q.shape
    ncmp = kc.shape[2]
    cspec = pl.BlockSpec((1, 1, ncmp, LANES), lambda bi, gi, i: (bi, gi, 0, 0))
    return pl.pallas_call(
        functools.partial(_cmp_attn_kernel, tq=tq),
        grid=(b, NSA_GROUPS, s // tq),
        in_specs=[pl.BlockSpec((1, NSA_REP, tq, LANES), lambda bi, gi, i: (bi, gi, i, 0)), cspec, cspec],
        out_specs=[pl.BlockSpec((1, tq, NSA_REP * LANES), lambda bi, gi, i: (bi, i, gi)),
                   pl.BlockSpec((1, 1, tq, LANES), lambda bi, gi, i: (bi, gi, i, 0))],
        out_shape=[jax.ShapeDtypeStruct((b, s, NSA_HEADS * LANES), F32),
                   jax.ShapeDtypeStruct((b, NSA_GROUPS, s, LANES), F32)],
        compiler_params=_cparams("parallel", "parallel", "arbitrary"),
    )(q, kc, vc)


def _sel_attn_kernel(q_ref, k_ref, v_ref, sel_ref, o_ref, *, tq, tk):
    q0 = pl.program_id(2) * tq
    per_chunk = tk // SEL_LEN
    eb = lax.broadcasted_iota(I32, (LANES, tk), 0)
    ec = lax.broadcasted_iota(I32, (LANES, tk), 1)
    expand = ((ec >> SEL_SHIFT) == eb).astype(BF16)
    sel = sel_ref[0, 0]
    q = q_ref[0].reshape(NSA_REP * tq, LANES)
    kr, vr = k_ref.at[0, 0], v_ref.at[0, 0]

    def chosen(j):
        local = pltpu.roll(sel, (LANES - j * per_chunk) % LANES, 1).astype(BF16)
        return jnp.dot(local, expand, preferred_element_type=F32) > 0.5

    nfull = q0 // tk
    carry = lax.fori_loop(
        0, nfull, lambda j, c: _softmax_step(c, q, kr, vr, pl.multiple_of(j * tk, tk), tk, chosen(j)),
        _softmax_init(NSA_REP, tq))
    start = pl.multiple_of(nfull * tk, tk)
    causal = start + lax.broadcasted_iota(I32, (tq, tk), 1) <= q0 + lax.broadcasted_iota(I32, (tq, tk), 0)
    carry = _softmax_step(carry, q, kr, vr, start, tk, chosen(nfull) & causal)
    _store_heads(o_ref, _softmax_finish(carry), tq)


def _win_attn_kernel(q_ref, k_ref, v_ref, o_ref, *, tq, tk):
    q0 = pl.program_id(2) * tq
    start = pl.multiple_of(jnp.maximum(q0 + tq - tk, 0), tq)
    dist = (q0 + lax.broadcasted_iota(I32, (tq, tk), 0)) - (start + lax.broadcasted_iota(I32, (tq, tk), 1))
    q = q_ref[0].reshape(NSA_REP * tq, LANES)
    carry = _softmax_step(_softmax_init(NSA_REP, tq), q, k_ref.at[0, 0], v_ref.at[0, 0], start, tk,
                          (dist >= 0) & (dist < WIN))
    _store_heads(o_ref, _softmax_finish(carry), tq)


def _nsa_branch_attn(kernel, q, k, v, extra, tq, tk):
    b, _, s, _ = q.shape
    assert s % tq == 0 and tk % tq == 0 and s >= tk
    kv = pl.BlockSpec((1, 1, s, LANES), lambda bi, gi, i: (bi, gi, 0, 0))
    in_specs = [pl.BlockSpec((1, NSA_REP, tq, LANES), lambda bi, gi, i: (bi, gi, i, 0)), kv, kv]
    in_specs += [pl.BlockSpec((1, 1, tq, LANES), lambda bi, gi, i: (bi, gi, i, 0)) for _ in extra]
    return pl.pallas_call(
        functools.partial(kernel, tq=tq, tk=tk),
        grid=(b, NSA_GROUPS, s // tq),
        in_specs=in_specs,
        out_specs=pl.BlockSpec((1, tq, NSA_REP * LANES), lambda bi, gi, i: (bi, i, gi)),
        out_shape=jax.ShapeDtypeStruct((b, s, NSA_HEADS * LANES), F32),
        compiler_params=_cparams("parallel", "parallel", "arbitrary"),
    )(q, k, v, *extra)


def _mix_out_kernel(x_ref, om_ref, oc_ref, os_ref, ow_ref, gate_ref, gm_ref, gn_ref, w_ref, o_ref):
    gate = jax.nn.sigmoid(gate_ref[0])
    parts = []
    for h in range(NSA_HEADS):
        sl = slice(h * LANES, (h + 1) * LANES)
        g0 = gate[:, h:h + 1]
        g1 = gate[:, LANES + h:LANES + h + 1]
        g2 = gate[:, 2 * LANES + h:2 * LANES + h + 1]
        parts.append(g0 * oc_ref[0, :, sl] + g1 * os_ref[0, :, sl] + g2 * ow_ref[0, :, sl])
    o_nsa = jnp.concatenate(parts, axis=-1)
    width = MLA_HEADS * MLA_V
    mixed = jnp.concatenate([_rms(om_ref[0], gm_ref[...], width), _rms(o_nsa, gn_ref[...], width)], axis=-1)
    o_ref[0] = x_ref[0] + jnp.dot(mixed.astype(BF16), w_ref[...], preferred_element_type=F32)


def _mix_out(x, om, oc, os_, ow, y3, gm, gn, w, tm):
    b, s, d = x.shape
    wide = om.shape[-1]
    full = lambda shape: pl.BlockSpec(shape, lambda bi, i: (0,) * len(shape))
    row = lambda width: pl.BlockSpec((1, tm, width), lambda bi, i: (bi, i, 0))
    return pl.pallas_call(
        _mix_out_kernel,
        grid=(b, s // tm),
        in_specs=[row(d), row(wide), row(wide), row(wide), row(wide),
                  pl.BlockSpec((1, tm, 3 * LANES), lambda bi, i: (bi, i, COL_GATE // (3 * LANES))),
                  full((1, wide)), full((1, wide)), full(w.shape)],
        out_specs=row(d),
        out_shape=jax.ShapeDtypeStruct((b, s, d), F32),
        compiler_params=_cparams("parallel", "parallel"),
    )(x, om, oc, os_, ow, y3, gm, gn, w)


def _mem_kv_kernel(m_ref, g_ref, w_ref, gk_ref, k_ref, v_ref):
    mem = m_ref[0]
    n = _rms(mem, g_ref[...], mem.shape[-1]).astype(BF16)
    kv = jnp.dot(n, w_ref[...], preferred_element_type=F32)
    width = XA_HEADS * XA_DH
    for h in range(XA_HEADS):
        sl = slice(h * XA_DH, (h + 1) * XA_DH)
        k_ref[0, :, sl] = _rms(kv[:, sl], gk_ref[...], XA_DH).astype(BF16)
    v_ref[0] = kv[:, width:].astype(BF16)


def _mem_kv(mem, g, w, gk):
    b, m, d = mem.shape
    width = XA_HEADS * XA_DH
    full = lambda shape: pl.BlockSpec(shape, lambda bi: (0,) * len(shape))
    ospec = pl.BlockSpec((1, m, width), lambda bi: (bi, 0, 0))
    os_ = jax.ShapeDtypeStruct((b, m, width), BF16)
    return pl.pallas_call(
        _mem_kv_kernel,
        grid=(b,),
        in_specs=[pl.BlockSpec((1, m, d), lambda bi: (bi, 0, 0)), full((1, d)), full(w.shape), full((1, XA_DH))],
        out_specs=[ospec, ospec],
        out_shape=[os_, os_],
        compiler_params=_cparams("parallel"),
    )(mem, g, w, gk)


def _xattn_kernel(h_ref, g_ref, wq_ref, gq_ref, k_ref, v_ref, wo_ref, o_ref):
    h = h_ref[0]
    hn = _rms(h, g_ref[...], h.shape[-1]).astype(BF16)
    q = jnp.dot(hn, wq_ref[...], preferred_element_type=F32)
    outs = []
    for hd in range(XA_HEADS):
        sl = slice(hd * XA_DH, (hd + 1) * XA_DH)
        qh = (_rms(q[:, sl], gq_ref[...], XA_DH) * (XA_DH ** -0.5)).astype(BF16)
        s = _dot_nt(qh, k_ref[0, :, sl])
        p = jnp.exp(s - jnp.max(s, axis=-1, keepdims=True))
        p = p * (1.0 / jnp.sum(p, axis=-1, keepdims=True))
        outs.append(jnp.dot(p.astype(BF16), v_ref[0, :, sl], preferred_element_type=F32))
    o = jnp.concatenate(outs, axis=-1).astype(BF16)
    o_ref[0] = h + jnp.dot(o, wo_ref[...], preferred_element_type=F32)


def _xattn(h, g, wq, gq, k, v, wo, tm):
    b, s, d = h.shape
    m, width = k.shape[1], k.shape[2]
    full = lambda shape: pl.BlockSpec(shape, lambda bi, i: (0,) * len(shape))
    kv = pl.BlockSpec((1, m, width), lambda bi, i: (bi, 0, 0))
    row = pl.BlockSpec((1, tm, d), lambda bi, i: (bi, i, 0))
    return pl.pallas_call(
        _xattn_kernel,
        grid=(b, s // tm),
        in_specs=[row, full((1, d)), full(wq.shape), full((1, XA_DH)), kv, kv, full(wo.shape)],
        out_specs=row,
        out_shape=jax.ShapeDtypeStruct((b, s, d), F32),
        compiler_params=_cparams("parallel", "parallel"),
    )(h, g, wq, gq, k, v, wo)


def _peer_score_kernel(h_ref, g_ref, wq_ref, kbd_ref, hn_ref, s_ref):
    h = h_ref[...]
    hn = _rms(h, g_ref[...], h.shape[-1]).astype(BF16)
    hn_ref[...] = hn
    q = jnp.dot(hn, wq_ref[...], preferred_element_type=F32).astype(BF16)
    s_ref[...] = _dot_nt(kbd_ref[...], q)


def _peer_score(h2d, g, wq, kbd, tm):
    t, d = h2d.shape
    nk = kbd.shape[0]
    full = lambda shape: pl.BlockSpec(shape, lambda i: (0,) * len(shape))
    return pl.pallas_call(
        _peer_score_kernel,
        grid=(t // tm,),
        in_specs=[pl.BlockSpec((tm, d), lambda i: (i, 0)), full((1, d)), full(wq.shape), full(kbd.shape)],
        out_specs=[pl.BlockSpec((tm, d), lambda i: (i, 0)), pl.BlockSpec((nk, tm), lambda i: (0, i))],
        out_shape=[jax.ShapeDtypeStruct((t, d), BF16), jax.ShapeDtypeStruct((nk, t), F32)],
        compiler_params=_cparams("parallel"),
    )(h2d, g, wq, kbd)


def _topk_rows(x, k, ids, payload=None):
    vals, outs = [], []
    for _ in range(k):
        m = jnp.max(x, axis=0, keepdims=True)
        idx = jnp.min(jnp.where(x == m, ids, jnp.iinfo(jnp.int32).max), axis=0, keepdims=True)
        hit = ids == idx
        vals.append(m)
        if payload is None:
            outs.append(idx)
        else:
            outs.append(jnp.max(jnp.where(hit, payload, -1), axis=0, keepdims=True))
        x = jnp.where(hit, -jnp.inf, x)
    return jnp.concatenate(vals, axis=0), jnp.concatenate(outs, axis=0)


PEER_EDGE = 4
assert (PEER_EDGE + 1) ** 2 > PEER_TOPK


def _route_head(s_ref, h, e_scr, g_scr):
    tm = s_ref.shape[1]
    key_ids = lax.broadcasted_iota(I32, (PEER_KEYS, tm), 0)
    a_ids = lax.broadcasted_iota(I32, (PEER_TOPK, tm), 0)
    base = pl.multiple_of(h * (2 * PEER_KEYS), 2 * PEER_KEYS)
    s1, i1 = _topk_rows(s_ref[pl.ds(base, PEER_KEYS), :], PEER_TOPK, key_ids)
    s2, i2 = _topk_rows(s_ref[pl.ds(base + PEER_KEYS, PEER_KEYS), :], PEER_TOPK, key_ids)
    i1 = i1 * PEER_KEYS
    cand, flat, cidx = [], [], []
    for a in range(PEER_EDGE):
        cand.append(s1[a:a + 1, :] + s2)
        flat.append(a * PEER_TOPK + a_ids)
        cidx.append(i1[a:a + 1, :] + i2)
    for b in range(PEER_EDGE):
        cand.append(jnp.where(a_ids >= PEER_EDGE, s1 + s2[b:b + 1, :], -jnp.inf))
        flat.append(a_ids * PEER_TOPK + b)
        cidx.append(i1 + i2[b:b + 1, :])
    top, eidx = _topk_rows(jnp.concatenate(cand, axis=0), PEER_TOPK, jnp.concatenate(flat, axis=0),
                           payload=jnp.concatenate(cidx, axis=0))
    p = jnp.exp(top - top[0:1, :])
    p = p * (1.0 / jnp.sum(p, axis=0, keepdims=True))
    row = pl.multiple_of(h * PEER_TOPK, PEER_TOPK)
    e_scr[pl.ds(row, PEER_TOPK), :] = eidx * PEER_WORD_ROWS
    g_scr[pl.ds(row, PEER_TOPK), :] = p


def _gather_rows(idx_ref, tab_ref, tile_ref, j):
    for k in range(PEER_PICKS):
        r = pl.multiple_of(idx_ref[j, k], PEER_WORD_ROWS)
        tile_ref[pl.ds(k, PEER_WORD_ROWS, stride=PEER_CHUNK_STRIDE), :] = tab_ref[pl.ds(r, PEER_WORD_ROWS), :]


def _tile_rows(tile_ref):
    chunks = [pltpu.bitcast(tile_ref[pl.ds(c * PEER_CHUNK_STRIDE, PEER_PICKS), :], BF16)
              for c in range(PEER_WORD_ROWS)]
    return jnp.concatenate(chunks, axis=1)


def _token_loop(idx_ref, tab_ref, tile_a, tile_b, compute, per_trip=None):
    _gather_rows(idx_ref, tab_ref, tile_a, 0)

    def group(i, carry):
        if per_trip is not None:
            per_trip(i)
        for u in range(PEER_UNROLL):
            j = i * PEER_UNROLL + u
            cur, nxt = (tile_a, tile_b) if u % 2 == 0 else (tile_b, tile_a)
            rows = _tile_rows(cur)
            _gather_rows(idx_ref, tab_ref, nxt, jnp.minimum(j + 1, PEER_TOK - 1))
            compute(j, rows)
        return carry

    lax.fori_loop(0, PEER_TOK // PEER_UNROLL, group, 0)


def _pair_matrix():
    r = lax.broadcasted_iota(I32, (2 * PEER_PICKS, PEER_PICKS), 0)
    c = lax.broadcasted_iota(I32, (2 * PEER_PICKS, PEER_PICKS), 1)
    return (r >> 1) == c


def _peer_route_up_kernel(s_ref, x_ref, tab_ref, w_ref, e_ref, idx_ref, sem, evm_ref, gbuf_ref, e_scr, g_scr,
                          tile_a, tile_b, act_ref, xf_ref):
    c = pl.program_id(0)
    last = pl.num_programs(0) - 1
    cur, nxt = (c + 1) % 2, c % 2
    half = x_ref.shape[1] // 2
    sub = lax.broadcasted_iota(I32, (16, half), 0)
    odd_lane = (lax.broadcasted_iota(I32, (1, 2 * PEER_PICKS), 1) & 1) == 1

    def to_smem(slot):
        return pltpu.make_async_copy(evm_ref, idx_ref.at[slot], sem)

    @pl.when(c == 0)
    def _():
        evm_ref[...] = jnp.zeros(evm_ref.shape, I32)
        gbuf_ref[1] = jnp.zeros(gbuf_ref.shape[1:], F32)
        to_smem(1).start()

    to_smem(cur).wait()
    xf_ref[...] = x_ref[...].astype(F32)

    def compute(j, rows):
        xr = xf_ref[pl.ds(j, 1), :]
        lo = jnp.broadcast_to(xr[:, :half], (16, half))
        hi = jnp.broadcast_to(xr[:, half:], (16, half))
        xsel = jnp.where(sub == 0, lo, jnp.where(sub == 1, hi, 0.0)).astype(BF16)
        r = _dot_nt(xsel, rows)
        act_ref[pl.ds(j, 1), :] = jnp.where(odd_lane, r[1:2, :], r[0:1, :])

    assert PEER_TOK // PEER_UNROLL == PEER_HEADS
    _token_loop(idx_ref.at[cur], tab_ref, tile_a, tile_b, compute,
                per_trip=lambda h: _route_head(s_ref, h, e_scr, g_scr))
    pair = _pair_matrix()
    act = jnp.dot(act_ref[...], pair.astype(F32), preferred_element_type=F32,
                  precision=lax.Precision.HIGHEST)
    w = (gbuf_ref[cur] * _gelu(act)).astype(BF16)
    w_ref[...] = _dot_nt(w, pair.astype(BF16))

    routed = e_scr[...].T
    e_ref[...] = routed
    evm_ref[...] = routed
    gbuf_ref[nxt] = g_scr[...].T

    @pl.when(c < last)
    def _():
        to_smem(nxt).start()


def _peer_down_kernel(idx_ref, w_ref, h_ref, tab_ref, o_ref, tile_a, tile_b):
    half = h_ref.shape[1] // 2
    shape = (16, 2 * PEER_PICKS)
    parity = (lax.broadcasted_iota(I32, shape, 1) & 1) == lax.broadcasted_iota(I32, shape, 0)

    def compute(j, rows):
        wrow = jnp.broadcast_to(w_ref[pl.ds(j, 1), :], shape)
        wm = jnp.where(parity, wrow, 0.0).astype(BF16)
        o2 = jnp.dot(wm, rows, preferred_element_type=F32)
        o_ref[pl.ds(j, 1), 0:half] = h_ref[pl.ds(j, 1), 0:half] + o2[0:1, :]
        o_ref[pl.ds(j, 1), half:] = h_ref[pl.ds(j, 1), half:] + o2[1:2, :]

    _token_loop(idx_ref, tab_ref, tile_a, tile_b, compute)


def _tile_scratch():
    return pltpu.VMEM((PEER_WORD_ROWS * PEER_CHUNK_STRIDE, LANES), I32)


def _table_spec(tab):
    return pl.BlockSpec(tab.shape, lambda c: (0, 0), pipeline_mode=pl.Buffered(1))


def _idx_spec():
    return pl.BlockSpec((PEER_TOK, PEER_PICKS), lambda c: (c, 0), memory_space=pltpu.SMEM)


def _peer_route_up(st, hn, tab):
    t, d = hn.shape
    nk = st.shape[0]
    nblk = t // PEER_TOK
    prev = lambda c: (jnp.maximum(c - 1, 0), 0)
    return pl.pallas_call(
        _peer_route_up_kernel,
        grid=(nblk + 1,),
        in_specs=[pl.BlockSpec((nk, PEER_TOK), lambda c: (0, jnp.minimum(c, nblk - 1))),
                  pl.BlockSpec((PEER_TOK, d), prev),
                  _table_spec(tab)],
        out_specs=[pl.BlockSpec((PEER_TOK, 2 * PEER_PICKS), prev),
                   pl.BlockSpec((PEER_TOK, PEER_PICKS), lambda c: (jnp.minimum(c, nblk - 1), 0))],
        out_shape=[jax.ShapeDtypeStruct((t, 2 * PEER_PICKS), F32), jax.ShapeDtypeStruct((t, PEER_PICKS), I32)],
        scratch_shapes=[pltpu.SMEM((2, PEER_TOK, PEER_PICKS), I32), pltpu.SemaphoreType.DMA(()),
                        pltpu.VMEM((PEER_TOK, PEER_PICKS), I32), pltpu.VMEM((2, PEER_TOK, PEER_PICKS), F32),
                        pltpu.VMEM((PEER_PICKS, PEER_TOK), I32), pltpu.VMEM((PEER_PICKS, PEER_TOK), F32),
                        _tile_scratch(), _tile_scratch(), pltpu.VMEM((PEER_TOK, 2 * PEER_PICKS), F32),
                        pltpu.VMEM((PEER_TOK, d), F32)],
        compiler_params=_cparams("arbitrary"),
    )(st, hn, tab)


def _peer_down(eidx, w, h2d, tab):
    t, d = h2d.shape
    return pl.pallas_call(
        _peer_down_kernel,
        grid=(t // PEER_TOK,),
        in_specs=[_idx_spec(),
                  pl.BlockSpec((PEER_TOK, 2 * PEER_PICKS), lambda c: (c, 0)),
                  pl.BlockSpec((PEER_TOK, d), lambda c: (c, 0)),
                  _table_spec(tab)],
        out_specs=pl.BlockSpec((PEER_TOK, d), lambda c: (c, 0)),
        out_shape=jax.ShapeDtypeStruct((t, d), F32),
        scratch_shapes=[_tile_scratch(), _tile_scratch()],
        compiler_params=_cparams("arbitrary"),
    )(eidx, w, h2d, tab)


def _pad_heads(w, heads, dim, dim_pad=LANES):
    lead = w.shape[:-1]
    w = w.reshape(lead + (heads, dim))
    w = jnp.pad(w, [(0, 0)] * len(lead) + [(0, 0), (0, dim_pad - dim)])
    return w.reshape(lead + (heads * dim_pad,))


def _pad_lanes(v, offset=0, width=LANES):
    return jnp.pad(v, [(0, 0)] * (v.ndim - 1) + [(offset, width - offset - v.shape[-1])])


def _rope_tabs(pos, rot_dim, offset):
    inv = 1.0 / (ROPE_THETA ** (jnp.arange(0, rot_dim, 2, dtype=F32) / rot_dim))
    ang = pos.astype(F32)[:, None] * inv[None, :]
    cos, sin = jnp.cos(ang), jnp.sin(ang)
    zero = jnp.zeros_like(sin)
    c = jnp.pad(jnp.concatenate([cos, cos], -1) - 1.0, ((0, 0), (offset, LANES - offset - rot_dim))) + 1.0
    sa = _pad_lanes(jnp.concatenate([-sin, zero], -1), offset)
    sb = _pad_lanes(jnp.concatenate([zero, sin], -1), offset)
    return c, sa, sb


def _pack_table(tab):
    e, d = tab.shape
    bits = lax.bitcast_convert_type(tab.astype(BF16), jnp.uint16).astype(jnp.uint32)
    word = bits[:, :d // 2] | (bits[:, d // 2:] << 16)
    return lax.bitcast_convert_type(word, I32).reshape(e * PEER_WORD_ROWS, LANES)


def _relayout_w_in(w_in):
    d = w_in.shape[0]
    o = 0
    seg = {}
    for name, size in (("cq", 256), ("ckv", 128), ("kpe", 32), ("qn", 512), ("kc", 128), ("vc", 128),
                       ("ks", 128), ("vs", 128), ("kw", 128), ("vw", 128), ("gate", 24)):
        seg[name] = w_in[:, o:o + size]
        o += size
    gates = [_pad_lanes(seg["gate"][:, j::3]) for j in range(3)]
    cols = [_pad_heads(seg["qn"], NSA_HEADS, NSA_DK),
            _pad_heads(seg["ks"], NSA_GROUPS, NSA_DK), _pad_heads(seg["kw"], NSA_GROUPS, NSA_DK),
            seg["cq"], seg["ckv"], _pad_lanes(seg["kpe"], MLA_NOPE),
            _pad_heads(seg["vs"], NSA_GROUPS, NSA_DK), _pad_heads(seg["vw"], NSA_GROUPS, NSA_DK),
            seg["kc"]] + gates + [seg["vc"]]
    w = jnp.concatenate(cols, axis=1)
    assert w.shape == (d, IN_COLS_PAD)
    return w.astype(BF16)


def _layer(h, mem, p):
    b, s, d = h.shape
    t = b * s
    assert s % SEL_CHUNK == 0 and N_SEL <= s // SEL_LEN <= SEL_BLOCKS_PAD and d == 1024
    pos = jnp.arange(s)

    y = _norm_matmul(h.reshape(t, d), p["mix_norm"], _relayout_w_in(p["w_in"]), 512)
    y3 = y.reshape(b, s, IN_COLS_PAD)

    wuq = _pad_heads(p["mla_w_uq"], MLA_HEADS, MLA_QK).astype(BF16)
    wukv = p["mla_w_ukv"].reshape(MLA_KV_LORA, MLA_HEADS, MLA_NOPE + MLA_V)
    wuk = _pad_heads(wukv[..., :MLA_NOPE].reshape(MLA_KV_LORA, -1), MLA_HEADS, MLA_NOPE).astype(BF16)
    wuv = _pad_heads(wukv[..., MLA_NOPE:].reshape(MLA_KV_LORA, -1), MLA_HEADS, MLA_V).astype(BF16)
    q, k, v = _mla_prep(y3, p["mla_q_norm"][None], p["mla_kv_norm"][None], _pad_lanes(p["mla_qk_norm"]),
                        wuq, wuk, wuv, _rope_tabs(pos, MLA_ROPE, MLA_NOPE), 512)
    o_mla = _mla_attn(q, k, v, 512)

    tabs = _rope_tabs(pos, NSA_ROT, 0)
    qn, ks, kw, vs, vw = _nsa_prep(y3, _pad_lanes(p["nsa_q_norm"][None]), _pad_lanes(p["nsa_k_norm"][1:3]),
                                   tabs, 512)
    nrow = s // CMP_STRIDE

    def blocks16(col):
        tkv = y3[:, :, col:col + NSA_GROUPS * NSA_DK].reshape(b, nrow, CMP_STRIDE, NSA_GROUPS, NSA_DK)
        return tkv.transpose(0, 3, 1, 2, 4).reshape(b, NSA_GROUPS, nrow, CMP_STRIDE * NSA_DK)

    pe = p["nsa_cmp_pe"].reshape(2, 2, CMP_STRIDE * NSA_DK)
    w1 = p["nsa_cmp_w1"].reshape(2, 2, CMP_STRIDE * NSA_DK, -1).astype(BF16)
    w2 = _pad_lanes(p["nsa_cmp_w2"]).astype(BF16)
    cmp_end = jnp.arange(nrow) * CMP_STRIDE + CMP_LEN - 1
    kc, vc = _compress(blocks16(COL_KC), blocks16(COL_VC), pe, w1, w2, _pad_lanes(p["nsa_k_norm"][0:1]),
                       _rope_tabs(cmp_end, NSA_ROT, 0))
    o_cmp, sel = _cmp_attn(qn, kc, vc, 128)
    o_sel = _nsa_branch_attn(_sel_attn_kernel, qn, ks, vs, (sel,), NSA_TQ, SEL_CHUNK)
    o_win = _nsa_branch_attn(_win_attn_kernel, qn, kw, vw, (), NSA_TQ, WIN + NSA_TQ)

    width = MLA_HEADS * MLA_V
    g_out = p["mix_out_norm"]
    gm = _pad_heads(g_out[None, :width], MLA_HEADS, MLA_V)
    gn = _pad_heads(g_out[None, width:], NSA_HEADS, NSA_DK)
    w_out = p["w_out"].reshape(2 * MLA_HEADS, MLA_V, d)
    w_out = jnp.pad(w_out, ((0, 0), (0, LANES - MLA_V), (0, 0))).reshape(2 * MLA_HEADS * LANES, d).astype(BF16)
    h = _mix_out(h, o_mla, o_cmp, o_sel, o_win, y3, gm, gn, w_out, 256)

    kx, vx = _mem_kv(mem, p["mem_norm"][None], p["xa_wkv"].astype(BF16), p["xa_qk_norm"][1:2])
    h = _xattn(h, p["xa_norm"][None], p["xa_wq"].astype(BF16), p["xa_qk_norm"][0:1], kx, vx,
               p["xa_wo"].astype(BF16), 256)

    keys = p["peer_keys"]
    nhp = PEER_HEADS * 2
    eye = jnp.eye(nhp, dtype=F32)
    kbd = (keys.reshape(nhp, PEER_KEYS, PEER_HALF)[:, :, None, :] * eye[:, None, :, None])
    kbd = kbd.reshape(nhp * PEER_KEYS, nhp * PEER_HALF).astype(BF16)
    h2d = h.reshape(t, d)
    hn, st = _peer_score(h2d, p["ffn_norm"][None], p["peer_wq"].astype(BF16), kbd, 512)
    w, eidx = _peer_route_up(st, hn, _pack_table(p["peer_u"]))
    out = _peer_down(eidx, w, h2d, _pack_table(p["peer_v"]))
    return out.reshape(b, s, d)


def kernel(x, mem, mix_norm, w_in, mla_q_norm, mla_kv_norm, mla_w_uq, mla_w_ukv, mla_qk_norm, nsa_q_norm, nsa_k_norm, nsa_cmp_pe, nsa_cmp_w1, nsa_cmp_w2, mix_out_norm, w_out, xa_norm, mem_norm, xa_wq, xa_wkv, xa_qk_norm, xa_wo, ffn_norm, peer_wq, peer_keys, peer_u, peer_v):
    params = dict(mix_norm=mix_norm, w_in=w_in, mla_q_norm=mla_q_norm, mla_kv_norm=mla_kv_norm,
                  mla_w_uq=mla_w_uq, mla_w_ukv=mla_w_ukv, mla_qk_norm=mla_qk_norm, nsa_q_norm=nsa_q_norm,
                  nsa_k_norm=nsa_k_norm, nsa_cmp_pe=nsa_cmp_pe, nsa_cmp_w1=nsa_cmp_w1, nsa_cmp_w2=nsa_cmp_w2,
                  mix_out_norm=mix_out_norm, w_out=w_out, xa_norm=xa_norm, mem_norm=mem_norm, xa_wq=xa_wq,
                  xa_wkv=xa_wkv, xa_qk_norm=xa_qk_norm, xa_wo=xa_wo, ffn_norm=ffn_norm, peer_wq=peer_wq,
                  peer_keys=peer_keys, peer_u=peer_u, peer_v=peer_v)
    h = x
    for layer in range(w_in.shape[0]):
        h = _layer(h, mem, {name: val[layer] for name, val in params.items()})
    return h
```

```python
import functools

import jax
import jax.numpy as jnp
from jax import lax
from jax.experimental import pallas as pl
from jax.experimental.pallas import tpu as pltpu

F32 = jnp.float32
BF16 = jnp.bfloat16
I32 = jnp.int32

LANES = 128
VMEM_LIMIT = 56 * 1024 * 1024

ROPE_THETA = 500000.0
EPS = 1e-6
NEG = -1e30
FORCE = 1e4

MLA_HEADS = 8
MLA_Q_LORA = 256
MLA_KV_LORA = 128
MLA_NOPE = 64
MLA_ROPE = 32
MLA_V = 64
MLA_QK = MLA_NOPE + MLA_ROPE

NSA_HEADS = 8
NSA_GROUPS = 2
NSA_REP = NSA_HEADS // NSA_GROUPS
NSA_DK = 64
NSA_ROT = NSA_DK // 4
CMP_LEN = 32
CMP_STRIDE = 16
SEL_LEN = 64
SEL_SHIFT = SEL_LEN.bit_length() - 1
SEL_BLOCKS_PAD = 64
N_SEL = 16
WIN = 512
SEL_CHUNK = 1024
NSA_TQ = 128
LOG2E = 1.4426950408889634

XA_HEADS = 4
XA_DH = 128

PEER_HEADS = 8
PEER_KEYS = 128
PEER_HALF = 64
PEER_TOPK = 16
PEER_PICKS = PEER_HEADS * PEER_TOPK
PEER_TOK = 128
PEER_WORD_ROWS = 4
PEER_UNROLL = 16

COL_QN = 0
COL_KS = 1024
COL_KW = 1280
COL_CQ = 1536
COL_CKV = 1792
COL_KPE = 1920
COL_VS = 2048
COL_VW = 2304
COL_KC = 2560
COL_GATE = 2688
COL_VC = 3072
IN_COLS_PAD = 3200


def _cparams(*sem):
    return pltpu.CompilerParams(dimension_semantics=sem, vmem_limit_bytes=VMEM_LIMIT)


def _rms(x, g, n):
    ms = jnp.sum(x * x, axis=-1, keepdims=True) * (1.0 / n)
    return x * lax.rsqrt(ms + EPS) * g


def _rope(x, c, sa, sb, half):
    return x * c + pltpu.roll(x, LANES - half, 1) * sa + pltpu.roll(x, half, 1) * sb


def _gelu(x):
    return 0.5 * x * (1.0 + jnp.tanh(0.7978845608028654 * (x + 0.044715 * (x * x * x))))


def _dot_nt(a, b, **kw):
    return lax.dot_general(a, b, (((1,), (1,)), ((), ())), preferred_element_type=F32, **kw)


def _norm_matmul_kernel(x_ref, g_ref, w_ref, o_ref):
    x = x_ref[...]
    n = _rms(x, g_ref[...], x.shape[-1])
    o_ref[...] = jnp.dot(n.astype(BF16), w_ref[...], preferred_element_type=F32)


def _norm_matmul(x2d, g, w, tm):
    t, d = x2d.shape
    n = w.shape[1]
    return pl.pallas_call(
        _norm_matmul_kernel,
        grid=(t // tm,),
        in_specs=[pl.BlockSpec((tm, d), lambda i: (i, 0)),
                  pl.BlockSpec((1, d), lambda i: (0, 0)),
                  pl.BlockSpec((d, n), lambda i: (0, 0))],
        out_specs=pl.BlockSpec((tm, n), lambda i: (i, 0)),
        out_shape=jax.ShapeDtypeStruct((t, n), F32),
        compiler_params=_cparams("parallel"),
    )(x2d, g.reshape(1, d), w)


def _mla_prep_kernel(y_ref, gq_ref, gkv_ref, gqk_ref, wuq_ref, wuk_ref, wuv_ref,
                     c_ref, sa_ref, sb_ref, q_ref, k_ref, v_ref):
    y = y_ref[0]
    cq = y[:, 0:MLA_Q_LORA]
    ckv = y[:, MLA_Q_LORA:MLA_Q_LORA + MLA_KV_LORA]
    kpe = y[:, MLA_Q_LORA + MLA_KV_LORA:]
    nq = _rms(cq, gq_ref[...], MLA_Q_LORA).astype(BF16)
    nkv = _rms(ckv, gkv_ref[...], MLA_KV_LORA).astype(BF16)
    qf = jnp.dot(nq, wuq_ref[...], preferred_element_type=F32)
    kf = jnp.dot(nkv, wuk_ref[...], preferred_element_type=F32)
    vf = jnp.dot(nkv, wuv_ref[...], preferred_element_type=F32)
    c, sa, sb = c_ref[...], sa_ref[...], sb_ref[...]
    gq = gqk_ref[0:1, :]
    gk = gqk_ref[1:2, :]
    scale = MLA_QK ** -0.5 * LOG2E
    for h in range(MLA_HEADS):
        sl = slice(h * LANES, (h + 1) * LANES)
        qh = _rope(_rms(qf[:, sl], gq, MLA_QK), c, sa, sb, MLA_ROPE // 2)
        kh = _rope(_rms(kf[:, sl] + kpe, gk, MLA_QK), c, sa, sb, MLA_ROPE // 2)
        q_ref[0, h] = (qh * scale).astype(BF16)
        k_ref[0, h] = kh.astype(BF16)
        v_ref[0, h] = vf[:, sl].astype(BF16)


def _mla_prep(y3, gq, gkv, gqk, wuq, wuk, wuv, tabs, tm):
    b, s, _ = y3.shape
    hs = jax.ShapeDtypeStruct((b, MLA_HEADS, s, LANES), BF16)
    full = lambda shape: pl.BlockSpec(shape, lambda bi, i: (0,) * len(shape))
    tab = pl.BlockSpec((tm, LANES), lambda bi, i: (i, 0))
    hspec = pl.BlockSpec((1, MLA_HEADS, tm, LANES), lambda bi, i: (bi, 0, i, 0))
    return pl.pallas_call(
        _mla_prep_kernel,
        grid=(b, s // tm),
        in_specs=[pl.BlockSpec((1, tm, 512), lambda bi, i: (bi, i, COL_CQ // 512)),
                  full((1, MLA_Q_LORA)), full((1, MLA_KV_LORA)), full((2, LANES)),
                  full((MLA_Q_LORA, MLA_HEADS * LANES)), full((MLA_KV_LORA, MLA_HEADS * LANES)),
                  full((MLA_KV_LORA, MLA_HEADS * LANES)), tab, tab, tab],
        out_specs=[hspec, hspec, hspec],
        out_shape=[hs, hs, hs],
        compiler_params=_cparams("parallel", "parallel"),
    )(y3, gq, gkv, gqk, wuq, wuk, wuv, *tabs)


def _nsa_prep_kernel(yq_ref, yv_ref, gq_ref, gk_ref, c_ref, sa_ref, sb_ref,
                     q_ref, ks_ref, kw_ref, vs_ref, vw_ref):
    c, sa, sb = c_ref[...], sa_ref[...], sb_ref[...]
    half = NSA_ROT // 2
    scale = NSA_DK ** -0.5 * LOG2E
    yq = yq_ref[0]
    for h in range(NSA_HEADS):
        qh = _rms(yq[:, h * LANES:(h + 1) * LANES], gq_ref[...], NSA_DK)
        q_ref[0, h] = (_rope(qh, c, sa, sb, half) * scale).astype(BF16)
    for g in range(NSA_GROUPS):
        ks = yq[:, COL_KS + g * LANES:COL_KS + (g + 1) * LANES]
        kw = yq[:, COL_KW + g * LANES:COL_KW + (g + 1) * LANES]
        ks_ref[0, g] = _rope(_rms(ks, gk_ref[0:1, :], NSA_DK), c, sa, sb, half).astype(BF16)
        kw_ref[0, g] = _rope(_rms(kw, gk_ref[1:2, :], NSA_DK), c, sa, sb, half).astype(BF16)
    yv = yv_ref[0]
    for g in range(NSA_GROUPS):
        vs_ref[0, g] = yv[:, g * LANES:(g + 1) * LANES].astype(BF16)
        vw_ref[0, g] = yv[:, (NSA_GROUPS + g) * LANES:(NSA_GROUPS + g + 1) * LANES].astype(BF16)


def _nsa_prep(y3, gq, gk, tabs, tm):
    b, s, _ = y3.shape
    full = lambda shape: pl.BlockSpec(shape, lambda bi, i: (0,) * len(shape))
    tab = pl.BlockSpec((tm, LANES), lambda bi, i: (i, 0))
    qs = jax.ShapeDtypeStruct((b, NSA_HEADS, s, LANES), BF16)
    gs = jax.ShapeDtypeStruct((b, NSA_GROUPS, s, LANES), BF16)
    gspec = pl.BlockSpec((1, NSA_GROUPS, tm, LANES), lambda bi, i: (bi, 0, i, 0))
    return pl.pallas_call(
        _nsa_prep_kernel,
        grid=(b, s // tm),
        in_specs=[pl.BlockSpec((1, tm, 1536), lambda bi, i: (bi, i, 0)),
                  pl.BlockSpec((1, tm, 512), lambda bi, i: (bi, i, COL_VS // 512)),
                  full((1, LANES)), full((2, LANES)), tab, tab, tab],
        out_specs=[pl.BlockSpec((1, NSA_HEADS, tm, LANES), lambda bi, i: (bi, 0, i, 0)),
                   gspec, gspec, gspec, gspec],
        out_shape=[qs, gs, gs, gs, gs],
        compiler_params=_cparams("parallel", "parallel"),
    )(y3, y3, gq, gk, *tabs)


def _compress_kernel(tk_ref, tv_ref, pe_ref, w1_ref, w2_ref, gk_ref, c_ref, sa_ref, sb_ref,
                     kc_ref, vc_ref):
    nrow = tk_ref.shape[2]

    def mlp(t, kv):
        ta = (t + pe_ref[kv, 0:1, :]).astype(BF16)
        tb = (t + pe_ref[kv, 1:2, :]).astype(BF16)
        za = jnp.dot(ta, w1_ref[kv, 0], preferred_element_type=F32)
        zb = jnp.dot(tb, w1_ref[kv, 1], preferred_element_type=F32)
        hid = _gelu(za + pltpu.roll(zb, nrow - 1, 0))
        return jnp.dot(hid.astype(BF16), w2_ref[kv], preferred_element_type=F32)

    kc = mlp(tk_ref[0, 0], 0)
    kc = _rope(_rms(kc, gk_ref[...], NSA_DK), c_ref[...], sa_ref[...], sb_ref[...], NSA_ROT // 2)
    kc_ref[0, 0] = kc.astype(BF16)
    vc_ref[0, 0] = mlp(tv_ref[0, 0], 1).astype(BF16)


def _compress(tk16, tv16, pe, w1, w2, gk, tabs):
    b, g, nrow, width = tk16.shape
    full = lambda shape: pl.BlockSpec(shape, lambda bi, gi: (0,) * len(shape))
    tspec = pl.BlockSpec((1, 1, nrow, width), lambda bi, gi: (bi, gi, 0, 0))
    ospec = pl.BlockSpec((1, 1, nrow, LANES), lambda bi, gi: (bi, gi, 0, 0))
    os_ = jax.ShapeDtypeStruct((b, g, nrow, LANES), BF16)
    return pl.pallas_call(
        _compress_kernel,
        grid=(b, g),
        in_specs=[tspec, tspec, full(pe.shape), full(w1.shape), full(w2.shape), full((1, LANES)),
                  full((nrow, LANES)), full((nrow, LANES)), full((nrow, LANES))],
        out_specs=[ospec, ospec],
        out_shape=[os_, os_],
        compiler_params=_cparams("parallel", "parallel"),
    )(tk16, tv16, pe, w1, w2, gk, *tabs)


def _softmax_init(heads, tq):
    return (jnp.full((heads, tq, 1), NEG, F32), jnp.zeros((heads, tq, 1), F32),
            jnp.zeros((heads * tq, LANES), F32))


def _softmax_step(carry, q, k_ref, v_ref, start, width, mask):
    m, l, acc = carry
    heads, tq, _ = m.shape
    k = k_ref[pl.ds(start, width), :]
    v = v_ref[pl.ds(start, width), :]
    s = _dot_nt(q, k).reshape(heads, tq, width)
    if mask is not None:
        s = jnp.where(mask[None], s, NEG)
    m_new = jnp.maximum(m, jnp.max(s, axis=-1, keepdims=True))
    p = jnp.exp2(s - m_new)
    alpha = jnp.exp2(m - m_new)
    l = alpha * l + jnp.sum(p, axis=-1, keepdims=True)
    pv = jnp.dot(p.reshape(heads * tq, width).astype(BF16), v, preferred_element_type=F32)
    return m_new, l, alpha.reshape(heads * tq, 1) * acc + pv


def _softmax_finish(carry):
    _, l, acc = carry
    return acc * (1.0 / l).reshape(acc.shape[0], 1)


def _mla_attn_kernel(q_ref, k_ref, v_ref, o_ref, *, tq):
    q0 = pl.program_id(2) * tq
    heads = range(q_ref.shape[1])
    wide = 2 * tq
    nwide = q0 // wide

    def step(carry, start, width, mask):
        return tuple(_softmax_step(carry[h], q_ref[0, h], k_ref.at[0, h], v_ref.at[0, h], start, width, mask)
                     for h in heads)

    carry = lax.fori_loop(0, nwide, lambda j, c: step(c, pl.multiple_of(j * wide, wide), wide, None),
                          tuple(_softmax_init(1, tq) for _ in heads))
    carry = lax.fori_loop(0, (q0 - nwide * wide) // tq,
                          lambda j, c: step(c, pl.multiple_of(nwide * wide, tq), tq, None), carry)
    causal = lax.broadcasted_iota(I32, (tq, tq), 1) <= lax.broadcasted_iota(I32, (tq, tq), 0)
    carry = step(carry, pl.multiple_of(q0, tq), tq, causal)
    for h in heads:
        o_ref[0, :, h * LANES:(h + 1) * LANES] = _softmax_finish(carry[h])


def _mla_attn(q, k, v, tq):
    b, h, s, _ = q.shape
    pair = 2
    kv = pl.BlockSpec((1, pair, s, LANES), lambda bi, hi, i: (bi, hi, 0, 0))
    return pl.pallas_call(
        functools.partial(_mla_attn_kernel, tq=tq),
        grid=(b, h // pair, s // tq),
        in_specs=[pl.BlockSpec((1, pair, tq, LANES), lambda bi, hi, i: (bi, hi, i, 0)), kv, kv],
        out_specs=pl.BlockSpec((1, tq, pair * LANES), lambda bi, hi, i: (bi, i, hi)),
        out_shape=jax.ShapeDtypeStruct((b, s, h * LANES), F32),
        compiler_params=_cparams("parallel", "parallel", "arbitrary"),
    )(q, k, v)


def _compressed_branch(q, kc, vc, q0, tq):
    ncmp = kc.shape[0]
    s = _dot_nt(q, kc).reshape(NSA_REP, tq, ncmp)
    qpos = q0 + lax.broadcasted_iota(I32, (tq, ncmp), 0)
    n = lax.broadcasted_iota(I32, (tq, ncmp), 1)
    msk = (n * CMP_STRIDE + (CMP_LEN - 1) <= qpos)[None]
    s = jnp.where(msk, s, NEG)
    m = jnp.max(s, axis=-1, keepdims=True)
    p = jnp.where(msk, jnp.exp2(s - m), 0.0)
    l = jnp.sum(p, axis=-1, keepdims=True)
    p = p * jnp.where(l > 0.0, 1.0 / l, 0.0)
    o = jnp.dot(p.reshape(NSA_REP * tq, ncmp).astype(BF16), vc, preferred_element_type=F32)

    nblk = SEL_BLOCKS_PAD
    psum = p[0] + p[1] + p[2] + p[3]
    bm = lax.broadcasted_iota(I32, (nblk, ncmp), 0)
    bn = lax.broadcasted_iota(I32, (nblk, ncmp), 1)
    overlap = ((bn * CMP_STRIDE < bm * SEL_LEN + SEL_LEN)
               & (bn * CMP_STRIDE + CMP_LEN > bm * SEL_LEN)).astype(F32)
    imp = _dot_nt(overlap, psum, precision=lax.Precision.HIGHEST)
    blk = lax.broadcasted_iota(I32, (nblk, tq), 0)
    cur = (q0 + lax.broadcasted_iota(I32, (nblk, tq), 1)) >> SEL_SHIFT
    forced = (blk == 0) | (blk == cur) | (blk == cur - 1)
    score = jnp.where(blk <= cur, jnp.where(forced, FORCE, imp), NEG)
    rank = jnp.zeros((nblk, tq), I32)
    for j in range(nblk):
        row = score[j:j + 1, :]
        rank = rank + jnp.where(row > score, 1, jnp.where(row == score, (blk > j).astype(I32), 0))
    sel = jnp.where(rank < N_SEL, (blk <= cur).astype(F32), 0.0)
    sel = jnp.concatenate([sel, jnp.zeros((LANES - nblk, tq), F32)], axis=0)
    return o, sel.T


def _selected_branch(q, kr, vr, sel, q0, tq, tk):
    per_chunk = tk // SEL_LEN
    eb = lax.broadcasted_iota(I32, (LANES, tk), 0)
    ec = lax.broadcasted_iota(I32, (LANES, tk), 1)
    expand = ((ec >> SEL_SHIFT) == eb).astype(BF16)

    def chosen(j):
        local = pltpu.roll(sel, (LANES - j * per_chunk) % LANES, 1).astype(BF16)
        return jnp.dot(local, expand, preferred_element_type=F32) > 0.5

    nfull = q0 // tk
    carry = lax.fori_loop(
        0, nfull, lambda j, c: _softmax_step(c, q, kr, vr, pl.multiple_of(j * tk, tk), tk, chosen(j)),
        _softmax_init(NSA_REP, tq))
    start = pl.multiple_of(nfull * tk, tk)
    causal = start + lax.broadcasted_iota(I32, (tq, tk), 1) <= q0 + lax.broadcasted_iota(I32, (tq, tk), 0)
    carry = _softmax_step(carry, q, kr, vr, start, tk, chosen(nfull) & causal)
    return _softmax_finish(carry)


def _window_branch(q, kr, vr, q0, tq, tk):
    start = pl.multiple_of(jnp.maximum(q0 + tq - tk, 0), tq)
    dist = (q0 + lax.broadcasted_iota(I32, (tq, tk), 0)) - (start + lax.broadcasted_iota(I32, (tq, tk), 1))
    carry = _softmax_step(_softmax_init(NSA_REP, tq), q, kr, vr, start, tk, (dist >= 0) & (dist < WIN))
    return _softmax_finish(carry)


def _nsa_attn_kernel(q_ref, kc_ref, vc_ref, ks_ref, vs_ref, kw_ref, vw_ref, gate_ref, o_ref, *, tq):
    group = pl.program_id(1)
    q0 = pl.program_id(2) * tq
    q = q_ref[0].reshape(NSA_REP * tq, LANES)
    o_win = _window_branch(q, kw_ref.at[0, 0], vw_ref.at[0, 0], q0, tq, WIN + tq)
    o_cmp, sel = _compressed_branch(q, kc_ref[0, 0], vc_ref[0, 0], q0, tq)
    o_sel = _selected_branch(q, ks_ref.at[0, 0], vs_ref.at[0, 0], sel, q0, tq, SEL_CHUNK)
    gate = jax.nn.sigmoid(gate_ref[0])
    shift = (LANES - NSA_REP * group) % LANES
    gates = [pltpu.roll(gate[:, j * LANES:(j + 1) * LANES], shift, 1) for j in range(3)]
    for r in range(NSA_REP):
        rows = slice(r * tq, (r + 1) * tq)
        o_ref[0, :, r * LANES:(r + 1) * LANES] = (gates[0][:, r:r + 1] * o_cmp[rows]
                                                  + gates[1][:, r:r + 1] * o_sel[rows]
                                                  + gates[2][:, r:r + 1] * o_win[rows])


def _nsa_attn(q, kc, vc, ks, vs, kw, vw, y3, tq):
    b, _, s, _ = q.shape
    ncmp = kc.shape[2]
    assert s % SEL_CHUNK == 0 and SEL_CHUNK % tq == 0 and s >= WIN + tq
    kv = pl.BlockSpec((1, 1, s, LANES), lambda bi, gi, i: (bi, gi, 0, 0))
    cspec = pl.BlockSpec((1, 1, ncmp, LANES), lambda bi, gi, i: (bi, gi, 0, 0))
    return pl.pallas_call(
        functools.partial(_nsa_attn_kernel, tq=tq),
        grid=(b, NSA_GROUPS, s // tq),
        in_specs=[pl.BlockSpec((1, NSA_REP, tq, LANES), lambda bi, gi, i: (bi, gi, i, 0)),
                  cspec, cspec, kv, kv, kv, kv,
                  pl.BlockSpec((1, tq, 3 * LANES), lambda bi, gi, i: (bi, i, COL_GATE // (3 * LANES)))],
        out_specs=pl.BlockSpec((1, tq, NSA_REP * LANES), lambda bi, gi, i: (bi, i, gi)),
        out_shape=jax.ShapeDtypeStruct((b, s, NSA_HEADS * LANES), F32),
        compiler_params=_cparams("parallel", "parallel", "arbitrary"),
    )(q, kc, vc, ks, vs, kw, vw, y3)


def _mix_out_kernel(x_ref, om_ref, on_ref, gm_ref, gn_ref, w_ref, o_ref):
    width = MLA_HEADS * MLA_V
    mixed = jnp.concatenate([_rms(om_ref[0], gm_ref[...], width), _rms(on_ref[0], gn_ref[...], width)], axis=-1)
    o_ref[0] = x_ref[0] + jnp.dot(mixed.astype(BF16), w_ref[...], preferred_element_type=F32)


def _mix_out(x, om, on, gm, gn, w, tm):
    b, s, d = x.shape
    wide = om.shape[-1]
    full = lambda shape: pl.BlockSpec(shape, lambda bi, i: (0,) * len(shape))
    row = lambda width: pl.BlockSpec((1, tm, width), lambda bi, i: (bi, i, 0))
    return pl.pallas_call(
        _mix_out_kernel,
        grid=(b, s // tm),
        in_specs=[row(d), row(wide), row(wide), full((1, wide)), full((1, wide)), full(w.shape)],
        out_specs=row(d),
        out_shape=jax.ShapeDtypeStruct((b, s, d), F32),
        compiler_params=_cparams("parallel", "parallel"),
    )(x, om, on, gm, gn, w)


def _mem_kv_kernel(m_ref, g_ref, w_ref, gk_ref, k_ref, v_ref):
    mem = m_ref[0]
    n = _rms(mem, g_ref[...], mem.shape[-1]).astype(BF16)
    kv = jnp.dot(n, w_ref[...], preferred_element_type=F32)
    width = XA_HEADS * XA_DH
    for h in range(XA_HEADS):
        sl = slice(h * XA_DH, (h + 1) * XA_DH)
        k_ref[0, :, sl] = _rms(kv[:, sl], gk_ref[...], XA_DH).astype(BF16)
    v_ref[0] = kv[:, width:].astype(BF16)


def _mem_kv(mem, g, w, gk):
    b, m, d = mem.shape
    width = XA_HEADS * XA_DH
    full = lambda shape: pl.BlockSpec(shape, lambda bi: (0,) * len(shape))
    ospec = pl.BlockSpec((1, m, width), lambda bi: (bi, 0, 0))
    os_ = jax.ShapeDtypeStruct((b, m, width), BF16)
    return pl.pallas_call(
        _mem_kv_kernel,
        grid=(b,),
        in_specs=[pl.BlockSpec((1, m, d), lambda bi: (bi, 0, 0)), full((1, d)), full(w.shape), full((1, XA_DH))],
        out_specs=[ospec, ospec],
        out_shape=[os_, os_],
        compiler_params=_cparams("parallel"),
    )(mem, g, w, gk)


def _xattn_kernel(h_ref, g_ref, wq_ref, gq_ref, k_ref, v_ref, wo_ref, o_ref):
    h = h_ref[0]
    hn = _rms(h, g_ref[...], h.shape[-1]).astype(BF16)
    q = jnp.dot(hn, wq_ref[...], preferred_element_type=F32)
    outs = []
    for hd in range(XA_HEADS):
        sl = slice(hd * XA_DH, (hd + 1) * XA_DH)
        qh = (_rms(q[:, sl], gq_ref[...], XA_DH) * (XA_DH ** -0.5)).astype(BF16)
        s = _dot_nt(qh, k_ref[0, :, sl])
        p = jnp.exp(s - jnp.max(s, axis=-1, keepdims=True))
        p = p * (1.0 / jnp.sum(p, axis=-1, keepdims=True))
        outs.append(jnp.dot(p.astype(BF16), v_ref[0, :, sl], preferred_element_type=F32))
    o = jnp.concatenate(outs, axis=-1).astype(BF16)
    o_ref[0] = h + jnp.dot(o, wo_ref[...], preferred_element_type=F32)


def _xattn(h, g, wq, gq, k, v, wo, tm):
    b, s, d = h.shape
    m, width = k.shape[1], k.shape[2]
    full = lambda shape: pl.BlockSpec(shape, lambda bi, i: (0,) * len(shape))
    kv = pl.BlockSpec((1, m, width), lambda bi, i: (bi, 0, 0))
    row = pl.BlockSpec((1, tm, d), lambda bi, i: (bi, i, 0))
    return pl.pallas_call(
        _xattn_kernel,
        grid=(b, s // tm),
        in_specs=[row, full((1, d)), full(wq.shape), full((1, XA_DH)), kv, kv, full(wo.shape)],
        out_specs=row,
        out_shape=jax.ShapeDtypeStruct((b, s, d), F32),
        compiler_params=_cparams("parallel", "parallel"),
    )(h, g, wq, gq, k, v, wo)


def _peer_score_kernel(h_ref, g_ref, wq_ref, kbd_ref, hn_ref, s_ref):
    h = h_ref[...]
    hn = _rms(h, g_ref[...], h.shape[-1]).astype(BF16)
    hn_ref[...] = hn
    q = jnp.dot(hn, wq_ref[...], preferred_element_type=F32).astype(BF16)
    s_ref[...] = _dot_nt(kbd_ref[...], q)


def _peer_score(h2d, g, wq, kbd, tm):
    t, d = h2d.shape
    nk = kbd.shape[0]
    full = lambda shape: pl.BlockSpec(shape, lambda i: (0,) * len(shape))
    return pl.pallas_call(
        _peer_score_kernel,
        grid=(t // tm,),
        in_specs=[pl.BlockSpec((tm, d), lambda i: (i, 0)), full((1, d)), full(wq.shape), full(kbd.shape)],
        out_specs=[pl.BlockSpec((tm, d), lambda i: (i, 0)), pl.BlockSpec((nk, tm), lambda i: (0, i))],
        out_shape=[jax.ShapeDtypeStruct((t, d), BF16), jax.ShapeDtypeStruct((nk, t), F32)],
        compiler_params=_cparams("parallel"),
    )(h2d, g, wq, kbd)


def _topk_rows(x, k, ids, payload=None):
    vals, outs = [], []
    for _ in range(k):
        m = jnp.max(x, axis=0, keepdims=True)
        idx = jnp.min(jnp.where(x == m, ids, jnp.iinfo(jnp.int32).max), axis=0, keepdims=True)
        hit = ids == idx
        vals.append(m)
        if payload is None:
            outs.append(idx)
        else:
            outs.append(jnp.max(jnp.where(hit, payload, -1), axis=0, keepdims=True))
        x = jnp.where(hit, -jnp.inf, x)
    return jnp.concatenate(vals, axis=0), jnp.concatenate(outs, axis=0)


PEER_EDGE = 4
assert (PEER_EDGE + 1) ** 2 > PEER_TOPK


def _route_head(s_ref, h, e_scr, g_scr):
    tm = s_ref.shape[1]
    key_ids = lax.broadcasted_iota(I32, (PEER_KEYS, tm), 0)
    a_ids = lax.broadcasted_iota(I32, (PEER_TOPK, tm), 0)
    base = pl.multiple_of(h * (2 * PEER_KEYS), 2 * PEER_KEYS)
    s1, i1 = _topk_rows(s_ref[pl.ds(base, PEER_KEYS), :], PEER_TOPK, key_ids)
    s2, i2 = _topk_rows(s_ref[pl.ds(base + PEER_KEYS, PEER_KEYS), :], PEER_TOPK, key_ids)
    i1 = i1 * PEER_KEYS
    cand, flat, cidx = [], [], []
    for a in range(PEER_EDGE):
        cand.append(s1[a:a + 1, :] + s2)
        flat.append(a * PEER_TOPK + a_ids)
        cidx.append(i1[a:a + 1, :] + i2)
    for b in range(PEER_EDGE):
        cand.append(jnp.where(a_ids >= PEER_EDGE, s1 + s2[b:b + 1, :], -jnp.inf))
        flat.append(a_ids * PEER_TOPK + b)
        cidx.append(i1 + i2[b:b + 1, :])
    top, eidx = _topk_rows(jnp.concatenate(cand, axis=0), PEER_TOPK, jnp.concatenate(flat, axis=0),
                           payload=jnp.concatenate(cidx, axis=0))
    p = jnp.exp(top - top[0:1, :])
    p = p * (1.0 / jnp.sum(p, axis=0, keepdims=True))
    row = pl.multiple_of(h * PEER_TOPK, PEER_TOPK)
    e_scr[pl.ds(row, PEER_TOPK), :] = eidx * PEER_WORD_ROWS
    g_scr[pl.ds(row, PEER_TOPK), :] = p


def _gather_rows(idx_ref, tab_ref, tile_ref, j):
    for k in range(PEER_PICKS):
        r = pl.multiple_of(idx_ref[j, k], PEER_WORD_ROWS)
        tile_ref[pl.ds(k * PEER_WORD_ROWS, PEER_WORD_ROWS), :] = tab_ref[pl.ds(r, PEER_WORD_ROWS), :]


def _tile_rows(tile_ref):
    chunks = [pltpu.bitcast(tile_ref[pl.ds(c, PEER_PICKS, stride=PEER_WORD_ROWS), :], BF16)
              for c in range(PEER_WORD_ROWS)]
    return jnp.concatenate(chunks, axis=1)


def _token_loop(idx_ref, tab_ref, tile_a, tile_b, compute, per_trip=None):
    _gather_rows(idx_ref, tab_ref, tile_a, 0)

    def group(i, carry):
        if per_trip is not None:
            per_trip(i)
        for u in range(PEER_UNROLL):
            j = i * PEER_UNROLL + u
            cur, nxt = (tile_a, tile_b) if u % 2 == 0 else (tile_b, tile_a)
            rows = _tile_rows(cur)
            _gather_rows(idx_ref, tab_ref, nxt, jnp.minimum(j + 1, PEER_TOK - 1))
            compute(j, rows)
        return carry

    lax.fori_loop(0, PEER_TOK // PEER_UNROLL, group, 0)


def _pair_matrix():
    r = lax.broadcasted_iota(I32, (2 * PEER_PICKS, PEER_PICKS), 0)
    c = lax.broadcasted_iota(I32, (2 * PEER_PICKS, PEER_PICKS), 1)
    return (r >> 1) == c


def _peer_route_up_kernel(s_ref, x_ref, tab_ref, w_ref, e_ref, idx_ref, sem, evm_ref, gbuf_ref, e_scr, g_scr,
                          tile_a, tile_b, act_ref, xf_ref):
    c = pl.program_id(0)
    last = pl.num_programs(0) - 1
    cur, nxt = (c + 1) % 2, c % 2
    half = x_ref.shape[1] // 2
    sub = lax.broadcasted_iota(I32, (16, half), 0)
    odd_lane = (lax.broadcasted_iota(I32, (1, 2 * PEER_PICKS), 1) & 1) == 1

    def to_smem(slot):
        return pltpu.make_async_copy(evm_ref, idx_ref.at[slot], sem)

    @pl.when(c == 0)
    def _():
        evm_ref[...] = jnp.zeros(evm_ref.shape, I32)
        gbuf_ref[1] = jnp.zeros(gbuf_ref.shape[1:], F32)
        to_smem(1).start()

    to_smem(cur).wait()
    xf_ref[...] = x_ref[...].astype(F32)

    def compute(j, rows):
        xr = xf_ref[pl.ds(j, 1), :]
        lo = jnp.broadcast_to(xr[:, :half], (16, half))
        hi = jnp.broadcast_to(xr[:, half:], (16, half))
        xsel = jnp.where(sub == 0, lo, jnp.where(sub == 1, hi, 0.0)).astype(BF16)
        r = _dot_nt(xsel, rows)
        act_ref[pl.ds(j, 1), :] = jnp.where(odd_lane, r[1:2, :], r[0:1, :])

    assert PEER_TOK // PEER_UNROLL == PEER_HEADS
    _token_loop(idx_ref.at[cur], tab_ref, tile_a, tile_b, compute,
                per_trip=lambda h: _route_head(s_ref, h, e_scr, g_scr))
    pair = _pair_matrix()
    act = jnp.dot(act_ref[...], pair.astype(F32), preferred_element_type=F32,
                  precision=lax.Precision.HIGHEST)
    w = (gbuf_ref[cur] * _gelu(act)).astype(BF16)
    w_ref[...] = _dot_nt(w, pair.astype(BF16))

    routed = e_scr[...].T
    e_ref[...] = routed
    evm_ref[...] = routed
    gbuf_ref[nxt] = g_scr[...].T

    @pl.when(c < last)
    def _():
        to_smem(nxt).start()


def _peer_down_kernel(idx_ref, w_ref, h_ref, tab_ref, o_ref, tile_a, tile_b):
    half = h_ref.shape[1] // 2
    shape = (16, 2 * PEER_PICKS)
    parity = (lax.broadcasted_iota(I32, shape, 1) & 1) == lax.broadcasted_iota(I32, shape, 0)

    def compute(j, rows):
        wrow = jnp.broadcast_to(w_ref[pl.ds(j, 1), :], shape)
        wm = jnp.where(parity, wrow, 0.0).astype(BF16)
        o2 = jnp.dot(wm, rows, preferred_element_type=F32)
        o_ref[pl.ds(j, 1), 0:half] = h_ref[pl.ds(j, 1), 0:half] + o2[0:1, :]
        o_ref[pl.ds(j, 1), half:] = h_ref[pl.ds(j, 1), half:] + o2[1:2, :]

    _token_loop(idx_ref, tab_ref, tile_a, tile_b, compute)


def _tile_scratch():
    return pltpu.VMEM((PEER_WORD_ROWS * PEER_PICKS, LANES), I32)


def _table_spec(tab):
    return pl.BlockSpec(tab.shape, lambda c: (0, 0), pipeline_mode=pl.Buffered(1))


def _idx_spec():
    return pl.BlockSpec((PEER_TOK, PEER_PICKS), lambda c: (c, 0), memory_space=pltpu.SMEM)


def _peer_route_up(st, hn, tab):
    t, d = hn.shape
    nk = st.shape[0]
    nblk = t // PEER_TOK
    prev = lambda c: (jnp.maximum(c - 1, 0), 0)
    return pl.pallas_call(
        _peer_route_up_kernel,
        grid=(nblk + 1,),
        in_specs=[pl.BlockSpec((nk, PEER_TOK), lambda c: (0, jnp.minimum(c, nblk - 1))),
                  pl.BlockSpec((PEER_TOK, d), prev),
                  _table_spec(tab)],
        out_specs=[pl.BlockSpec((PEER_TOK, 2 * PEER_PICKS), prev),
                   pl.BlockSpec((PEER_TOK, PEER_PICKS), lambda c: (jnp.minimum(c, nblk - 1), 0))],
        out_shape=[jax.ShapeDtypeStruct((t, 2 * PEER_PICKS), F32), jax.ShapeDtypeStruct((t, PEER_PICKS), I32)],
        scratch_shapes=[pltpu.SMEM((2, PEER_TOK, PEER_PICKS), I32), pltpu.SemaphoreType.DMA(()),
                        pltpu.VMEM((PEER_TOK, PEER_PICKS), I32), pltpu.VMEM((2, PEER_TOK, PEER_PICKS), F32),
                        pltpu.VMEM((PEER_PICKS, PEER_TOK), I32), pltpu.VMEM((PEER_PICKS, PEER_TOK), F32),
                        _tile_scratch(), _tile_scratch(), pltpu.VMEM((PEER_TOK, 2 * PEER_PICKS), F32),
                        pltpu.VMEM((PEER_TOK, d), F32)],
        compiler_params=_cparams("arbitrary"),
    )(st, hn, tab)


def _peer_down(eidx, w, h2d, tab):
    t, d = h2d.shape
    return pl.pallas_call(
        _peer_down_kernel,
        grid=(t // PEER_TOK,),
        in_specs=[_idx_spec(),
                  pl.BlockSpec((PEER_TOK, 2 * PEER_PICKS), lambda c: (c, 0)),
                  pl.BlockSpec((PEER_TOK, d), lambda c: (c, 0)),
                  _table_spec(tab)],
        out_specs=pl.BlockSpec((PEER_TOK, d), lambda c: (c, 0)),
        out_shape=jax.ShapeDtypeStruct((t, d), F32),
        scratch_shapes=[_tile_scratch(), _tile_scratch()],
        compiler_params=_cparams("arbitrary"),
    )(eidx, w, h2d, tab)


def _pad_heads(w, heads, dim, dim_pad=LANES):
    lead = w.shape[:-1]
    w = w.reshape(lead + (heads, dim))
    w = jnp.pad(w, [(0, 0)] * len(lead) + [(0, 0), (0, dim_pad - dim)])
    return w.reshape(lead + (heads * dim_pad,))


def _pad_lanes(v, offset=0, width=LANES):
    return jnp.pad(v, [(0, 0)] * (v.ndim - 1) + [(offset, width - offset - v.shape[-1])])


def _rope_tabs(pos, rot_dim, offset):
    inv = 1.0 / (ROPE_THETA ** (jnp.arange(0, rot_dim, 2, dtype=F32) / rot_dim))
    ang = pos.astype(F32)[:, None] * inv[None, :]
    cos, sin = jnp.cos(ang), jnp.sin(ang)
    zero = jnp.zeros_like(sin)
    c = jnp.pad(jnp.concatenate([cos, cos], -1) - 1.0, ((0, 0), (offset, LANES - offset - rot_dim))) + 1.0
    sa = _pad_lanes(jnp.concatenate([-sin, zero], -1), offset)
    sb = _pad_lanes(jnp.concatenate([zero, sin], -1), offset)
    return c, sa, sb


def _pack_table(tab):
    e, d = tab.shape
    bits = lax.bitcast_convert_type(tab.astype(BF16), jnp.uint16).astype(jnp.uint32)
    word = bits[:, :d // 2] | (bits[:, d // 2:] << 16)
    return lax.bitcast_convert_type(word, I32).reshape(e * PEER_WORD_ROWS, LANES)


def _relayout_w_in(w_in):
    d = w_in.shape[0]
    o = 0
    seg = {}
    for name, size in (("cq", 256), ("ckv", 128), ("kpe", 32), ("qn", 512), ("kc", 128), ("vc", 128),
                       ("ks", 128), ("vs", 128), ("kw", 128), ("vw", 128), ("gate", 24)):
        seg[name] = w_in[:, o:o + size]
        o += size
    gates = [_pad_lanes(seg["gate"][:, j::3]) for j in range(3)]
    cols = [_pad_heads(seg["qn"], NSA_HEADS, NSA_DK),
            _pad_heads(seg["ks"], NSA_GROUPS, NSA_DK), _pad_heads(seg["kw"], NSA_GROUPS, NSA_DK),
            seg["cq"], seg["ckv"], _pad_lanes(seg["kpe"], MLA_NOPE),
            _pad_heads(seg["vs"], NSA_GROUPS, NSA_DK), _pad_heads(seg["vw"], NSA_GROUPS, NSA_DK),
            seg["kc"]] + gates + [seg["vc"]]
    w = jnp.concatenate(cols, axis=1)
    assert w.shape == (d, IN_COLS_PAD)
    return w.astype(BF16)


def _layer(h, mem, p):
    b, s, d = h.shape
    t = b * s
    assert s % SEL_CHUNK == 0 and N_SEL <= s // SEL_LEN <= SEL_BLOCKS_PAD and d == 1024
    pos = jnp.arange(s)

    y = _norm_matmul(h.reshape(t, d), p["mix_norm"], _relayout_w_in(p["w_in"]), 512)
    y3 = y.reshape(b, s, IN_COLS_PAD)

    wuq = _pad_heads(p["mla_w_uq"], MLA_HEADS, MLA_QK).astype(BF16)
    wukv = p["mla_w_ukv"].reshape(MLA_KV_LORA, MLA_HEADS, MLA_NOPE + MLA_V)
    wuk = _pad_heads(wukv[..., :MLA_NOPE].reshape(MLA_KV_LORA, -1), MLA_HEADS, MLA_NOPE).astype(BF16)
    wuv = _pad_heads(wukv[..., MLA_NOPE:].reshape(MLA_KV_LORA, -1), MLA_HEADS, MLA_V).astype(BF16)
    q, k, v = _mla_prep(y3, p["mla_q_norm"][None], p["mla_kv_norm"][None], _pad_lanes(p["mla_qk_norm"]),
                        wuq, wuk, wuv, _rope_tabs(pos, MLA_ROPE, MLA_NOPE), 512)
    o_mla = _mla_attn(q, k, v, 512)

    tabs = _rope_tabs(pos, NSA_ROT, 0)
    qn, ks, kw, vs, vw = _nsa_prep(y3, _pad_lanes(p["nsa_q_norm"][None]), _pad_lanes(p["nsa_k_norm"][1:3]),
                                   tabs, 512)
    nrow = s // CMP_STRIDE

    def blocks16(col):
        tkv = y3[:, :, col:col + NSA_GROUPS * NSA_DK].reshape(b, nrow, CMP_STRIDE, NSA_GROUPS, NSA_DK)
        return tkv.transpose(0, 3, 1, 2, 4).reshape(b, NSA_GROUPS, nrow, CMP_STRIDE * NSA_DK)

    pe = p["nsa_cmp_pe"].reshape(2, 2, CMP_STRIDE * NSA_DK)
    w1 = p["nsa_cmp_w1"].reshape(2, 2, CMP_STRIDE * NSA_DK, -1).astype(BF16)
    w2 = _pad_lanes(p["nsa_cmp_w2"]).astype(BF16)
    cmp_end = jnp.arange(nrow) * CMP_STRIDE + CMP_LEN - 1
    kc, vc = _compress(blocks16(COL_KC), blocks16(COL_VC), pe, w1, w2, _pad_lanes(p["nsa_k_norm"][0:1]),
                       _rope_tabs(cmp_end, NSA_ROT, 0))
    o_nsa = _nsa_attn(qn, kc, vc, ks, vs, kw, vw, y3, NSA_TQ)

    width = MLA_HEADS * MLA_V
    g_out = p["mix_out_norm"]
    gm = _pad_heads(g_out[None, :width], MLA_HEADS, MLA_V)
    gn = _pad_heads(g_out[None, width:], NSA_HEADS, NSA_DK)
    w_out = p["w_out"].reshape(2 * MLA_HEADS, MLA_V, d)
    w_out = jnp.pad(w_out, ((0, 0), (0, LANES - MLA_V), (0, 0))).reshape(2 * MLA_HEADS * LANES, d).astype(BF16)
    h = _mix_out(h, o_mla, o_nsa, gm, gn, w_out, 256)

    kx, vx = _mem_kv(mem, p["mem_norm"][None], p["xa_wkv"].astype(BF16), p["xa_qk_norm"][1:2])
    h = _xattn(h, p["xa_norm"][None], p["xa_wq"].astype(BF16), p["xa_qk_norm"][0:1], kx, vx,
               p["xa_wo"].astype(BF16), 256)

    keys = p["peer_keys"]
    nhp = PEER_HEADS * 2
    eye = jnp.eye(nhp, dtype=F32)
    kbd = (keys.reshape(nhp, PEER_KEYS, PEER_HALF)[:, :, None, :] * eye[:, None, :, None])
    kbd = kbd.reshape(nhp * PEER_KEYS, nhp * PEER_HALF).astype(BF16)
    h2d = h.reshape(t, d)
    hn, st = _peer_score(h2d, p["ffn_norm"][None], p["peer_wq"].astype(BF16), kbd, 512)
    w, eidx = _peer_route_up(st, hn, _pack_table(p["peer_u"]))
    out = _peer_down(eidx, w, h2d, _pack_table(p["peer_v"]))
    return out.reshape(b, s, d)


def kernel(x, mem, mix_norm, w_in, mla_q_norm, mla_kv_norm, mla_w_uq, mla_w_ukv, mla_qk_norm, nsa_q_norm, nsa_k_norm, nsa_cmp_pe, nsa_cmp_w1, nsa_cmp_w2, mix_out_norm, w_out, xa_norm, mem_norm, xa_wq, xa_wkv, xa_qk_norm, xa_wo, ffn_norm, peer_wq, peer_keys, peer_u, peer_v):
    params = dict(mix_norm=mix_norm, w_in=w_in, mla_q_norm=mla_q_norm, mla_kv_norm=mla_kv_norm,
                  mla_w_uq=mla_w_uq, mla_w_ukv=mla_w_ukv, mla_qk_norm=mla_qk_norm, nsa_q_norm=nsa_q_norm,
                  nsa_k_norm=nsa_k_norm, nsa_cmp_pe=nsa_cmp_pe, nsa_cmp_w1=nsa_cmp_w1, nsa_cmp_w2=nsa_cmp_w2,
                  mix_out_norm=mix_out_norm, w_out=w_out, xa_norm=xa_norm, mem_norm=mem_norm, xa_wq=xa_wq,
                  xa_wkv=xa_wkv, xa_qk_norm=xa_qk_norm, xa_wo=xa_wo, ffn_norm=ffn_norm, peer_wq=peer_wq,
                  peer_keys=peer_keys, peer_u=peer_u, peer_v=peer_v)
    h = x
    for layer in range(w_in.shape[0]):
        h = _layer(h, mem, {name: val[layer] for name, val in params.items()})
    return h
```

```python
import functools

import jax
import jax.numpy as jnp
from jax import lax
from jax.experimental import pallas as pl
from jax.experimental.pallas import tpu as pltpu

F32 = jnp.float32
BF16 = jnp.bfloat16
I32 = jnp.int32

LANES = 128
VMEM_LIMIT = 56 * 1024 * 1024

ROPE_THETA = 500000.0
EPS = 1e-6
NEG = -1e30
FORCE = 1e4

MLA_HEADS = 8
MLA_Q_LORA = 256
MLA_KV_LORA = 128
MLA_NOPE = 64
MLA_ROPE = 32
MLA_V = 64
MLA_QK = MLA_NOPE + MLA_ROPE

NSA_HEADS = 8
NSA_GROUPS = 2
NSA_REP = NSA_HEADS // NSA_GROUPS
NSA_DK = 64
NSA_ROT = NSA_DK // 4
CMP_LEN = 32
CMP_STRIDE = 16
SEL_LEN = 64
SEL_SHIFT = SEL_LEN.bit_length() - 1
SEL_BLOCKS_PAD = 64
N_SEL = 16
WIN = 512
SEL_CHUNK = 1024
NSA_TQ = 128
LOG2E = 1.4426950408889634

XA_HEADS = 4
XA_DH = 128

PEER_HEADS = 8
PEER_KEYS = 128
PEER_HALF = 64
PEER_TOPK = 16
PEER_PICKS = PEER_HEADS * PEER_TOPK
PEER_TOK = 128
PEER_WORD_ROWS = 4
PEER_UNROLL = 16

COL_QN = 0
COL_KS = 1024
COL_KW = 1280
COL_CQ = 1536
COL_CKV = 1792
COL_KPE = 1920
COL_VS = 2048
COL_VW = 2304
COL_KC = 2560
COL_GATE = 2688
COL_VC = 3072
IN_COLS_PAD = 3200


def _cparams(*sem):
    return pltpu.CompilerParams(dimension_semantics=sem, vmem_limit_bytes=VMEM_LIMIT)


def _rms(x, g, n):
    ms = jnp.sum(x * x, axis=-1, keepdims=True) * (1.0 / n)
    return x * lax.rsqrt(ms + EPS) * g


def _rope(x, c, sa, sb, half):
    return x * c + pltpu.roll(x, LANES - half, 1) * sa + pltpu.roll(x, half, 1) * sb


def _gelu(x):
    return 0.5 * x * (1.0 + jnp.tanh(0.7978845608028654 * (x + 0.044715 * (x * x * x))))


def _dot_nt(a, b, **kw):
    return lax.dot_general(a, b, (((1,), (1,)), ((), ())), preferred_element_type=F32, **kw)


def _norm_matmul_kernel(x_ref, g_ref, w_ref, o_ref):
    x = x_ref[...]
    n = _rms(x, g_ref[...], x.shape[-1])
    o_ref[...] = jnp.dot(n.astype(BF16), w_ref[...], preferred_element_type=F32)


def _norm_matmul(x2d, g, w, tm):
    t, d = x2d.shape
    n = w.shape[1]
    return pl.pallas_call(
        _norm_matmul_kernel,
        grid=(t // tm,),
        in_specs=[pl.BlockSpec((tm, d), lambda i: (i, 0)),
                  pl.BlockSpec((1, d), lambda i: (0, 0)),
                  pl.BlockSpec((d, n), lambda i: (0, 0))],
        out_specs=pl.BlockSpec((tm, n), lambda i: (i, 0)),
        out_shape=jax.ShapeDtypeStruct((t, n), F32),
        compiler_params=_cparams("parallel"),
    )(x2d, g.reshape(1, d), w)


def _mla_prep_kernel(y_ref, gq_ref, gkv_ref, gqk_ref, wuq_ref, wuk_ref, wuv_ref,
                     c_ref, sa_ref, sb_ref, q_ref, k_ref, v_ref):
    y = y_ref[0]
    cq = y[:, 0:MLA_Q_LORA]
    ckv = y[:, MLA_Q_LORA:MLA_Q_LORA + MLA_KV_LORA]
    kpe = y[:, MLA_Q_LORA + MLA_KV_LORA:]
    nq = _rms(cq, gq_ref[...], MLA_Q_LORA).astype(BF16)
    nkv = _rms(ckv, gkv_ref[...], MLA_KV_LORA).astype(BF16)
    qf = jnp.dot(nq, wuq_ref[...], preferred_element_type=F32)
    kf = jnp.dot(nkv, wuk_ref[...], preferred_element_type=F32)
    vf = jnp.dot(nkv, wuv_ref[...], preferred_element_type=F32)
    c, sa, sb = c_ref[...], sa_ref[...], sb_ref[...]
    gq = gqk_ref[0:1, :]
    gk = gqk_ref[1:2, :]
    scale = MLA_QK ** -0.5 * LOG2E
    for h in range(MLA_HEADS):
        sl = slice(h * LANES, (h + 1) * LANES)
        qh = _rope(_rms(qf[:, sl], gq, MLA_QK), c, sa, sb, MLA_ROPE // 2)
        kh = _rope(_rms(kf[:, sl] + kpe, gk, MLA_QK), c, sa, sb, MLA_ROPE // 2)
        q_ref[0, h] = (qh * scale).astype(BF16)
        k_ref[0, h] = kh.astype(BF16)
        v_ref[0, h] = vf[:, sl].astype(BF16)


def _mla_prep(y3, gq, gkv, gqk, wuq, wuk, wuv, tabs, tm):
    b, s, _ = y3.shape
    hs = jax.ShapeDtypeStruct((b, MLA_HEADS, s, LANES), BF16)
    full = lambda shape: pl.BlockSpec(shape, lambda bi, i: (0,) * len(shape))
    tab = pl.BlockSpec((tm, LANES), lambda bi, i: (i, 0))
    hspec = pl.BlockSpec((1, MLA_HEADS, tm, LANES), lambda bi, i: (bi, 0, i, 0))
    return pl.pallas_call(
        _mla_prep_kernel,
        grid=(b, s // tm),
        in_specs=[pl.BlockSpec((1, tm, 512), lambda bi, i: (bi, i, COL_CQ // 512)),
                  full((1, MLA_Q_LORA)), full((1, MLA_KV_LORA)), full((2, LANES)),
                  full((MLA_Q_LORA, MLA_HEADS * LANES)), full((MLA_KV_LORA, MLA_HEADS * LANES)),
                  full((MLA_KV_LORA, MLA_HEADS * LANES)), tab, tab, tab],
        out_specs=[hspec, hspec, hspec],
        out_shape=[hs, hs, hs],
        compiler_params=_cparams("parallel", "parallel"),
    )(y3, gq, gkv, gqk, wuq, wuk, wuv, *tabs)


def _nsa_prep_kernel(yq_ref, yv_ref, gq_ref, gk_ref, c_ref, sa_ref, sb_ref,
                     q_ref, ks_ref, kw_ref, vs_ref, vw_ref):
    c, sa, sb = c_ref[...], sa_ref[...], sb_ref[...]
    half = NSA_ROT // 2
    scale = NSA_DK ** -0.5 * LOG2E
    yq = yq_ref[0]
    for h in range(NSA_HEADS):
        qh = _rms(yq[:, h * LANES:(h + 1) * LANES], gq_ref[...], NSA_DK)
        q_ref[0, h] = (_rope(qh, c, sa, sb, half) * scale).astype(BF16)
    for g in range(NSA_GROUPS):
        ks = yq[:, COL_KS + g * LANES:COL_KS + (g + 1) * LANES]
        kw = yq[:, COL_KW + g * LANES:COL_KW + (g + 1) * LANES]
        ks_ref[0, g] = _rope(_rms(ks, gk_ref[0:1, :], NSA_DK), c, sa, sb, half).astype(BF16)
        kw_ref[0, g] = _rope(_rms(kw, gk_ref[1:2, :], NSA_DK), c, sa, sb, half).astype(BF16)
    yv = yv_ref[0]
    for g in range(NSA_GROUPS):
        vs_ref[0, g] = yv[:, g * LANES:(g + 1) * LANES].astype(BF16)
        vw_ref[0, g] = yv[:, (NSA_GROUPS + g) * LANES:(NSA_GROUPS + g + 1) * LANES].astype(BF16)


def _nsa_prep(y3, gq, gk, tabs, tm):
    b, s, _ = y3.shape
    full = lambda shape: pl.BlockSpec(shape, lambda bi, i: (0,) * len(shape))
    tab = pl.BlockSpec((tm, LANES), lambda bi, i: (i, 0))
    qs = jax.ShapeDtypeStruct((b, NSA_HEADS, s, LANES), BF16)
    gs = jax.ShapeDtypeStruct((b, NSA_GROUPS, s, LANES), BF16)
    gspec = pl.BlockSpec((1, NSA_GROUPS, tm, LANES), lambda bi, i: (bi, 0, i, 0))
    return pl.pallas_call(
        _nsa_prep_kernel,
        grid=(b, s // tm),
        in_specs=[pl.BlockSpec((1, tm, 1536), lambda bi, i: (bi, i, 0)),
                  pl.BlockSpec((1, tm, 512), lambda bi, i: (bi, i, COL_VS // 512)),
                  full((1, LANES)), full((2, LANES)), tab, tab, tab],
        out_specs=[pl.BlockSpec((1, NSA_HEADS, tm, LANES), lambda bi, i: (bi, 0, i, 0)),
                   gspec, gspec, gspec, gspec],
        out_shape=[qs, gs, gs, gs, gs],
        compiler_params=_cparams("parallel", "parallel"),
    )(y3, y3, gq, gk, *tabs)


def _compress_kernel(tk_ref, tv_ref, pe_ref, w1_ref, w2_ref, gk_ref, c_ref, sa_ref, sb_ref,
                     kc_ref, vc_ref):
    nrow = tk_ref.shape[2]

    def mlp(t, kv):
        ta = (t + pe_ref[kv, 0:1, :]).astype(BF16)
        tb = (t + pe_ref[kv, 1:2, :]).astype(BF16)
        za = jnp.dot(ta, w1_ref[kv, 0], preferred_element_type=F32)
        zb = jnp.dot(tb, w1_ref[kv, 1], preferred_element_type=F32)
        hid = _gelu(za + pltpu.roll(zb, nrow - 1, 0))
        return jnp.dot(hid.astype(BF16), w2_ref[kv], preferred_element_type=F32)

    kc = mlp(tk_ref[0, 0], 0)
    kc = _rope(_rms(kc, gk_ref[...], NSA_DK), c_ref[...], sa_ref[...], sb_ref[...], NSA_ROT // 2)
    kc_ref[0, 0] = kc.astype(BF16)
    vc_ref[0, 0] = mlp(tv_ref[0, 0], 1).astype(BF16)


def _compress(tk16, tv16, pe, w1, w2, gk, tabs):
    b, g, nrow, width = tk16.shape
    full = lambda shape: pl.BlockSpec(shape, lambda bi, gi: (0,) * len(shape))
    tspec = pl.BlockSpec((1, 1, nrow, width), lambda bi, gi: (bi, gi, 0, 0))
    ospec = pl.BlockSpec((1, 1, nrow, LANES), lambda bi, gi: (bi, gi, 0, 0))
    os_ = jax.ShapeDtypeStruct((b, g, nrow, LANES), BF16)
    return pl.pallas_call(
        _compress_kernel,
        grid=(b, g),
        in_specs=[tspec, tspec, full(pe.shape), full(w1.shape), full(w2.shape), full((1, LANES)),
                  full((nrow, LANES)), full((nrow, LANES)), full((nrow, LANES))],
        out_specs=[ospec, ospec],
        out_shape=[os_, os_],
        compiler_params=_cparams("parallel", "parallel"),
    )(tk16, tv16, pe, w1, w2, gk, *tabs)


def _softmax_init(heads, tq):
    return (jnp.full((heads, tq, 1), NEG, F32), jnp.zeros((heads, tq, 1), F32),
            jnp.zeros((heads * tq, LANES), F32))


def _softmax_step(carry, q, k_ref, v_ref, start, width, mask):
    m, l, acc = carry
    heads, tq, _ = m.shape
    k = k_ref[pl.ds(start, width), :]
    v = v_ref[pl.ds(start, width), :]
    s = _dot_nt(q, k).reshape(heads, tq, width)
    if mask is not None:
        s = jnp.where(mask[None], s, NEG)
    m_new = jnp.maximum(m, jnp.max(s, axis=-1, keepdims=True))
    p = jnp.exp2(s - m_new)
    alpha = jnp.exp2(m - m_new)
    l = alpha * l + jnp.sum(p, axis=-1, keepdims=True)
    pv = jnp.dot(p.reshape(heads * tq, width).astype(BF16), v, preferred_element_type=F32)
    return m_new, l, alpha.reshape(heads * tq, 1) * acc + pv


def _softmax_finish(carry):
    _, l, acc = carry
    return acc * (1.0 / l).reshape(acc.shape[0], 1)


def _mla_attn_kernel(q_ref, k_ref, v_ref, o_ref, *, tq):
    q0 = pl.program_id(2) * tq
    heads = range(q_ref.shape[1])
    wide = 2 * tq
    nwide = q0 // wide

    def step(carry, start, width, mask):
        return tuple(_softmax_step(carry[h], q_ref[0, h], k_ref.at[0, h], v_ref.at[0, h], start, width, mask)
                     for h in heads)

    carry = lax.fori_loop(0, nwide, lambda j, c: step(c, pl.multiple_of(j * wide, wide), wide, None),
                          tuple(_softmax_init(1, tq) for _ in heads))
    carry = lax.fori_loop(0, (q0 - nwide * wide) // tq,
                          lambda j, c: step(c, pl.multiple_of(nwide * wide, tq), tq, None), carry)
    causal = lax.broadcasted_iota(I32, (tq, tq), 1) <= lax.broadcasted_iota(I32, (tq, tq), 0)
    carry = step(carry, pl.multiple_of(q0, tq), tq, causal)
    for h in heads:
        o_ref[0, :, h * LANES:(h + 1) * LANES] = _softmax_finish(carry[h])


def _mla_attn(q, k, v, tq):
    b, h, s, _ = q.shape
    pair = 2
    kv = pl.BlockSpec((1, pair, s, LANES), lambda bi, hi, i: (bi, hi, 0, 0))
    return pl.pallas_call(
        functools.partial(_mla_attn_kernel, tq=tq),
        grid=(b, h // pair, s // tq),
        in_specs=[pl.BlockSpec((1, pair, tq, LANES), lambda bi, hi, i: (bi, hi, i, 0)), kv, kv],
        out_specs=pl.BlockSpec((1, tq, pair * LANES), lambda bi, hi, i: (bi, i, hi)),
        out_shape=jax.ShapeDtypeStruct((b, s, h * LANES), F32),
        compiler_params=_cparams("parallel", "parallel", "arbitrary"),
    )(q, k, v)


def _compressed_branch(q, kc, vc, q0, tq):
    ncmp = kc.shape[0]
    s = _dot_nt(q, kc).reshape(NSA_REP, tq, ncmp)
    qpos = q0 + lax.broadcasted_iota(I32, (tq, ncmp), 0)
    n = lax.broadcasted_iota(I32, (tq, ncmp), 1)
    msk = (n * CMP_STRIDE + (CMP_LEN - 1) <= qpos)[None]
    s = jnp.where(msk, s, NEG)
    m = jnp.max(s, axis=-1, keepdims=True)
    p = jnp.where(msk, jnp.exp2(s - m), 0.0)
    l = jnp.sum(p, axis=-1, keepdims=True)
    p = p * jnp.where(l > 0.0, 1.0 / l, 0.0)
    o = jnp.dot(p.reshape(NSA_REP * tq, ncmp).astype(BF16), vc, preferred_element_type=F32)

    nblk = SEL_BLOCKS_PAD
    psum = p[0] + p[1] + p[2] + p[3]
    bm = lax.broadcasted_iota(I32, (nblk, ncmp), 0)
    bn = lax.broadcasted_iota(I32, (nblk, ncmp), 1)
    overlap = ((bn * CMP_STRIDE < bm * SEL_LEN + SEL_LEN)
               & (bn * CMP_STRIDE + CMP_LEN > bm * SEL_LEN)).astype(F32)
    imp = _dot_nt(overlap, psum, precision=lax.Precision.HIGHEST)
    blk = lax.broadcasted_iota(I32, (nblk, tq), 0)
    cur = (q0 + lax.broadcasted_iota(I32, (nblk, tq), 1)) >> SEL_SHIFT
    forced = (blk == 0) | (blk == cur) | (blk == cur - 1)
    score = jnp.where(blk <= cur, jnp.where(forced, FORCE, imp), NEG)
    rank = jnp.zeros((nblk, tq), I32)
    for j in range(nblk):
        row = score[j:j + 1, :]
        rank = rank + jnp.where(row > score, 1, jnp.where(row == score, (blk > j).astype(I32), 0))
    sel = jnp.where(rank < N_SEL, (blk <= cur).astype(F32), 0.0)
    sel = jnp.concatenate([sel, jnp.zeros((LANES - nblk, tq), F32)], axis=0)
    return o, sel.T


def _selected_branch(q, kr, vr, sel, q0, tq, tk):
    per_chunk = tk // SEL_LEN
    eb = lax.broadcasted_iota(I32, (LANES, tk), 0)
    ec = lax.broadcasted_iota(I32, (LANES, tk), 1)
    expand = ((ec >> SEL_SHIFT) == eb).astype(BF16)

    def chosen(j):
        local = pltpu.roll(sel, (LANES - j * per_chunk) % LANES, 1).astype(BF16)
        return jnp.dot(local, expand, preferred_element_type=F32) > 0.5

    nfull = q0 // tk
    carry = lax.fori_loop(
        0, nfull, lambda j, c: _softmax_step(c, q, kr, vr, pl.multiple_of(j * tk, tk), tk, chosen(j)),
        _softmax_init(NSA_REP, tq))
    start = pl.multiple_of(nfull * tk, tk)
    causal = start + lax.broadcasted_iota(I32, (tq, tk), 1) <= q0 + lax.broadcasted_iota(I32, (tq, tk), 0)
    carry = _softmax_step(carry, q, kr, vr, start, tk, chosen(nfull) & causal)
    return _softmax_finish(carry)


def _window_branch(q, kr, vr, q0, tq, tk):
    start = pl.multiple_of(jnp.maximum(q0 + tq - tk, 0), tq)
    dist = (q0 + lax.broadcasted_iota(I32, (tq, tk), 0)) - (start + lax.broadcasted_iota(I32, (tq, tk), 1))
    carry = _softmax_step(_softmax_init(NSA_REP, tq), q, kr, vr, start, tk, (dist >= 0) & (dist < WIN))
    return _softmax_finish(carry)


def _nsa_attn_kernel(q_ref, kc_ref, vc_ref, ks_ref, vs_ref, kw_ref, vw_ref, gate_ref, o_ref, *, tq):
    group = pl.program_id(1)
    q0 = pl.program_id(2) * tq
    q = q_ref[0].reshape(NSA_REP * tq, LANES)
    o_win = _window_branch(q, kw_ref.at[0, 0], vw_ref.at[0, 0], q0, tq, WIN + tq)
    o_cmp, sel = _compressed_branch(q, kc_ref[0, 0], vc_ref[0, 0], q0, tq)
    o_sel = _selected_branch(q, ks_ref.at[0, 0], vs_ref.at[0, 0], sel, q0, tq, SEL_CHUNK)
    gate = jax.nn.sigmoid(gate_ref[0])
    shift = (LANES - NSA_REP * group) % LANES
    gates = [pltpu.roll(gate[:, j * LANES:(j + 1) * LANES], shift, 1) for j in range(3)]
    for r in range(NSA_REP):
        rows = slice(r * tq, (r + 1) * tq)
        o_ref[0, :, r * LANES:(r + 1) * LANES] = (gates[0][:, r:r + 1] * o_cmp[rows]
                                                  + gates[1][:, r:r + 1] * o_sel[rows]
                                                  + gates[2][:, r:r + 1] * o_win[rows])


def _nsa_attn(q, kc, vc, ks, vs, kw, vw, y3, tq):
    b, _, s, _ = q.shape
    ncmp = kc.shape[2]
    assert s % SEL_CHUNK == 0 and SEL_CHUNK % tq == 0 and s >= WIN + tq
    kv = pl.BlockSpec((1, 1, s, LANES), lambda bi, gi, i: (bi, gi, 0, 0))
    cspec = pl.BlockSpec((1, 1, ncmp, LANES), lambda bi, gi, i: (bi, gi, 0, 0))
    return pl.pallas_call(
        functools.partial(_nsa_attn_kernel, tq=tq),
        grid=(b, NSA_GROUPS, s // tq),
        in_specs=[pl.BlockSpec((1, NSA_REP, tq, LANES), lambda bi, gi, i: (bi, gi, i, 0)),
                  cspec, cspec, kv, kv, kv, kv,
                  pl.BlockSpec((1, tq, 3 * LANES), lambda bi, gi, i: (bi, i, COL_GATE // (3 * LANES)))],
        out_specs=pl.BlockSpec((1, tq, NSA_REP * LANES), lambda bi, gi, i: (bi, i, gi)),
        out_shape=jax.ShapeDtypeStruct((b, s, NSA_HEADS * LANES), F32),
        compiler_params=_cparams("parallel", "parallel", "arbitrary"),
    )(q, kc, vc, ks, vs, kw, vw, y3)


def _mix_out_kernel(x_ref, om_ref, on_ref, gm_ref, gn_ref, w_ref, o_ref):
    width = MLA_HEADS * MLA_V
    mixed = jnp.concatenate([_rms(om_ref[0], gm_ref[...], width), _rms(on_ref[0], gn_ref[...], width)], axis=-1)
    o_ref[0] = x_ref[0] + jnp.dot(mixed.astype(BF16), w_ref[...], preferred_element_type=F32)


def _mix_out(x, om, on, gm, gn, w, tm):
    b, s, d = x.shape
    wide = om.shape[-1]
    full = lambda shape: pl.BlockSpec(shape, lambda bi, i: (0,) * len(shape))
    row = lambda width: pl.BlockSpec((1, tm, width), lambda bi, i: (bi, i, 0))
    return pl.pallas_call(
        _mix_out_kernel,
        grid=(b, s // tm),
        in_specs=[row(d), row(wide), row(wide), full((1, wide)), full((1, wide)), full(w.shape)],
        out_specs=row(d),
        out_shape=jax.ShapeDtypeStruct((b, s, d), F32),
        compiler_params=_cparams("parallel", "parallel"),
    )(x, om, on, gm, gn, w)


def _mem_kv_kernel(m_ref, g_ref, w_ref, gk_ref, k_ref, v_ref):
    mem = m_ref[0]
    n = _rms(mem, g_ref[...], mem.shape[-1]).astype(BF16)
    kv = jnp.dot(n, w_ref[...], preferred_element_type=F32)
    width = XA_HEADS * XA_DH
    for h in range(XA_HEADS):
        sl = slice(h * XA_DH, (h + 1) * XA_DH)
        k_ref[0, :, sl] = _rms(kv[:, sl], gk_ref[...], XA_DH).astype(BF16)
    v_ref[0] = kv[:, width:].astype(BF16)


def _mem_kv(mem, g, w, gk):
    b, m, d = mem.shape
    width = XA_HEADS * XA_DH
    full = lambda shape: pl.BlockSpec(shape, lambda bi: (0,) * len(shape))
    ospec = pl.BlockSpec((1, m, width), lambda bi: (bi, 0, 0))
    os_ = jax.ShapeDtypeStruct((b, m, width), BF16)
    return pl.pallas_call(
        _mem_kv_kernel,
        grid=(b,),
        in_specs=[pl.BlockSpec((1, m, d), lambda bi: (bi, 0, 0)), full((1, d)), full(w.shape), full((1, XA_DH))],
        out_specs=[ospec, ospec],
        out_shape=[os_, os_],
        compiler_params=_cparams("parallel"),
    )(mem, g, w, gk)


def _xattn_kernel(h_ref, g_ref, wq_ref, gq_ref, k_ref, v_ref, wo_ref, o_ref):
    h = h_ref[0]
    hn = _rms(h, g_ref[...], h.shape[-1]).astype(BF16)
    q = jnp.dot(hn, wq_ref[...], preferred_element_type=F32)
    outs = []
    for hd in range(XA_HEADS):
        sl = slice(hd * XA_DH, (hd + 1) * XA_DH)
        qh = (_rms(q[:, sl], gq_ref[...], XA_DH) * (XA_DH ** -0.5)).astype(BF16)
        s = _dot_nt(qh, k_ref[0, :, sl])
        p = jnp.exp(s - jnp.max(s, axis=-1, keepdims=True))
        p = p * (1.0 / jnp.sum(p, axis=-1, keepdims=True))
        outs.append(jnp.dot(p.astype(BF16), v_ref[0, :, sl], preferred_element_type=F32))
    o = jnp.concatenate(outs, axis=-1).astype(BF16)
    o_ref[0] = h + jnp.dot(o, wo_ref[...], preferred_element_type=F32)


def _xattn(h, g, wq, gq, k, v, wo, tm):
    b, s, d = h.shape
    m, width = k.shape[1], k.shape[2]
    full = lambda shape: pl.BlockSpec(shape, lambda bi, i: (0,) * len(shape))
    kv = pl.BlockSpec((1, m, width), lambda bi, i: (bi, 0, 0))
    row = pl.BlockSpec((1, tm, d), lambda bi, i: (bi, i, 0))
    return pl.pallas_call(
        _xattn_kernel,
        grid=(b, s // tm),
        in_specs=[row, full((1, d)), full(wq.shape), full((1, XA_DH)), kv, kv, full(wo.shape)],
        out_specs=row,
        out_shape=jax.ShapeDtypeStruct((b, s, d), F32),
        compiler_params=_cparams("parallel", "parallel"),
    )(h, g, wq, gq, k, v, wo)


def _peer_score_kernel(h_ref, g_ref, wq_ref, kbd_ref, hn_ref, s_ref):
    h = h_ref[...]
    hn = _rms(h, g_ref[...], h.shape[-1]).astype(BF16)
    hn_ref[...] = hn
    q = jnp.dot(hn, wq_ref[...], preferred_element_type=F32).astype(BF16)
    s_ref[...] = _dot_nt(kbd_ref[...], q)


def _peer_score(h2d, g, wq, kbd, tm):
    t, d = h2d.shape
    nk = kbd.shape[0]
    full = lambda shape: pl.BlockSpec(shape, lambda i: (0,) * len(shape))
    return pl.pallas_call(
        _peer_score_kernel,
        grid=(t // tm,),
        in_specs=[pl.BlockSpec((tm, d), lambda i: (i, 0)), full((1, d)), full(wq.shape), full(kbd.shape)],
        out_specs=[pl.BlockSpec((tm, d), lambda i: (i, 0)), pl.BlockSpec((nk, tm), lambda i: (0, i))],
        out_shape=[jax.ShapeDtypeStruct((t, d), BF16), jax.ShapeDtypeStruct((nk, t), F32)],
        compiler_params=_cparams("parallel"),
    )(h2d, g, wq, kbd)


def _topk_rows(x, k, ids, payload=None):
    vals, outs = [], []
    for _ in range(k):
        m = jnp.max(x, axis=0, keepdims=True)
        idx = jnp.min(jnp.where(x == m, ids, jnp.iinfo(jnp.int32).max), axis=0, keepdims=True)
        hit = ids == idx
        vals.append(m)
        if payload is None:
            outs.append(idx)
        else:
            outs.append(jnp.max(jnp.where(hit, payload, -1), axis=0, keepdims=True))
        x = jnp.where(hit, -jnp.inf, x)
    return jnp.concatenate(vals, axis=0), jnp.concatenate(outs, axis=0)


PEER_EDGE = 4
assert (PEER_EDGE + 1) ** 2 > PEER_TOPK


def _route_head(s_ref, h, e_scr, g_scr):
    tm = s_ref.shape[1]
    key_ids = lax.broadcasted_iota(I32, (PEER_KEYS, tm), 0)
    a_ids = lax.broadcasted_iota(I32, (PEER_TOPK, tm), 0)
    base = pl.multiple_of(h * (2 * PEER_KEYS), 2 * PEER_KEYS)
    s1, i1 = _topk_rows(s_ref[pl.ds(base, PEER_KEYS), :], PEER_TOPK, key_ids)
    s2, i2 = _topk_rows(s_ref[pl.ds(base + PEER_KEYS, PEER_KEYS), :], PEER_TOPK, key_ids)
    i1 = i1 * PEER_KEYS
    cand, flat, cidx = [], [], []
    for a in range(PEER_EDGE):
        cand.append(s1[a:a + 1, :] + s2)
        flat.append(a * PEER_TOPK + a_ids)
        cidx.append(i1[a:a + 1, :] + i2)
    for b in range(PEER_EDGE):
        cand.append(jnp.where(a_ids >= PEER_EDGE, s1 + s2[b:b + 1, :], -jnp.inf))
        flat.append(a_ids * PEER_TOPK + b)
        cidx.append(i1 + i2[b:b + 1, :])
    top, eidx = _topk_rows(jnp.concatenate(cand, axis=0), PEER_TOPK, jnp.concatenate(flat, axis=0),
                           payload=jnp.concatenate(cidx, axis=0))
    p = jnp.exp(top - top[0:1, :])
    p = p * (1.0 / jnp.sum(p, axis=0, keepdims=True))
    row = pl.multiple_of(h * PEER_TOPK, PEER_TOPK)
    e_scr[pl.ds(row, PEER_TOPK), :] = eidx * PEER_WORD_ROWS
    g_scr[pl.ds(row, PEER_TOPK), :] = p


def _gather_rows(idx_ref, tab_ref, tile_ref, j):
    for k in range(PEER_PICKS):
        r = pl.multiple_of(idx_ref[j, k], PEER_WORD_ROWS)
        tile_ref[pl.ds(k * PEER_WORD_ROWS, PEER_WORD_ROWS), :] = tab_ref[pl.ds(r, PEER_WORD_ROWS), :]


def _tile_rows(tile_ref):
    chunks = [pltpu.bitcast(tile_ref[pl.ds(c, PEER_PICKS, stride=PEER_WORD_ROWS), :], BF16)
              for c in range(PEER_WORD_ROWS)]
    return jnp.concatenate(chunks, axis=1)


def _token_loop(idx_vm, idx_sm, sem, tab_ref, tile_a, tile_b, compute, per_trip=None):
    ntrip = PEER_TOK // PEER_UNROLL

    def fetch(t, slot):
        rows = pl.ds(pl.multiple_of(t * PEER_UNROLL, PEER_UNROLL), PEER_UNROLL)
        return pltpu.make_async_copy(idx_vm.at[rows], idx_sm.at[slot], sem.at[slot])

    fetch(0, 0).start()
    fetch(0, 0).wait()
    _gather_rows(idx_sm.at[0], tab_ref, tile_a, 0)

    def group(i, carry):
        for slot in range(2):
            t = 2 * i + slot
            ahead = jnp.minimum(t + 1, ntrip - 1)
            fetch(ahead, 1 - slot).start()
            if per_trip is not None:
                per_trip(t)
            for u in range(PEER_UNROLL):
                cur, nxt = (tile_a, tile_b) if u % 2 == 0 else (tile_b, tile_a)
                rows = _tile_rows(cur)
                if u + 1 < PEER_UNROLL:
                    _gather_rows(idx_sm.at[slot], tab_ref, nxt, u + 1)
                else:
                    fetch(ahead, 1 - slot).wait()
                    _gather_rows(idx_sm.at[1 - slot], tab_ref, nxt, 0)
                compute(t * PEER_UNROLL + u, rows)
        return carry

    assert PEER_UNROLL % 2 == 0 and ntrip % 2 == 0
    lax.fori_loop(0, ntrip // 2, group, 0)


def _idx_scratch():
    return [pltpu.SMEM((2, PEER_UNROLL, PEER_PICKS), I32), pltpu.SemaphoreType.DMA((2,))]


def _pair_matrix():
    r = lax.broadcasted_iota(I32, (2 * PEER_PICKS, PEER_PICKS), 0)
    c = lax.broadcasted_iota(I32, (2 * PEER_PICKS, PEER_PICKS), 1)
    return (r >> 1) == c


def _peer_route_up_kernel(s_ref, x_ref, tab_ref, w_ref, e_ref, ebuf_ref, gbuf_ref, e_scr, g_scr,
                          tile_a, tile_b, act_ref, xf_ref, idx_sm, sem):
    c = pl.program_id(0)
    cur, nxt = (c + 1) % 2, c % 2
    half = x_ref.shape[1] // 2
    sub = lax.broadcasted_iota(I32, (16, half), 0)
    odd_lane = (lax.broadcasted_iota(I32, (1, 2 * PEER_PICKS), 1) & 1) == 1

    @pl.when(c == 0)
    def _():
        ebuf_ref[1] = jnp.zeros(ebuf_ref.shape[1:], I32)
        gbuf_ref[1] = jnp.zeros(gbuf_ref.shape[1:], F32)

    xf_ref[...] = x_ref[...].astype(F32)

    def compute(j, rows):
        xr = xf_ref[pl.ds(j, 1), :]
        lo = jnp.broadcast_to(xr[:, :half], (16, half))
        hi = jnp.broadcast_to(xr[:, half:], (16, half))
        xsel = jnp.where(sub == 0, lo, jnp.where(sub == 1, hi, 0.0)).astype(BF16)
        r = _dot_nt(xsel, rows)
        act_ref[pl.ds(j, 1), :] = jnp.where(odd_lane, r[1:2, :], r[0:1, :])

    assert PEER_TOK // PEER_UNROLL == PEER_HEADS
    _token_loop(ebuf_ref.at[cur], idx_sm, sem, tab_ref, tile_a, tile_b, compute,
                per_trip=lambda h: _route_head(s_ref, h, e_scr, g_scr))
    pair = _pair_matrix()
    act = jnp.dot(act_ref[...], pair.astype(F32), preferred_element_type=F32,
                  precision=lax.Precision.HIGHEST)
    w = (gbuf_ref[cur] * _gelu(act)).astype(BF16)
    w_ref[...] = _dot_nt(w, pair.astype(BF16))

    routed = e_scr[...].T
    e_ref[...] = routed
    ebuf_ref[nxt] = routed
    gbuf_ref[nxt] = g_scr[...].T


def _peer_down_kernel(idx_ref, w_ref, h_ref, tab_ref, o_ref, tile_a, tile_b, idx_sm, sem):
    half = h_ref.shape[1] // 2
    shape = (16, 2 * PEER_PICKS)
    parity = (lax.broadcasted_iota(I32, shape, 1) & 1) == lax.broadcasted_iota(I32, shape, 0)

    def compute(j, rows):
        wrow = jnp.broadcast_to(w_ref[pl.ds(j, 1), :], shape)
        wm = jnp.where(parity, wrow, 0.0).astype(BF16)
        o2 = jnp.dot(wm, rows, preferred_element_type=F32)
        o_ref[pl.ds(j, 1), 0:half] = h_ref[pl.ds(j, 1), 0:half] + o2[0:1, :]
        o_ref[pl.ds(j, 1), half:] = h_ref[pl.ds(j, 1), half:] + o2[1:2, :]

    _token_loop(idx_ref, idx_sm, sem, tab_ref, tile_a, tile_b, compute)


def _tile_scratch():
    return pltpu.VMEM((PEER_WORD_ROWS * PEER_PICKS, LANES), I32)


def _table_spec(tab):
    return pl.BlockSpec(tab.shape, lambda c: (0, 0), pipeline_mode=pl.Buffered(1))


def _idx_spec():
    return pl.BlockSpec((PEER_TOK, PEER_PICKS), lambda c: (c, 0), memory_space=pltpu.SMEM)


def _peer_route_up(st, hn, tab):
    t, d = hn.shape
    nk = st.shape[0]
    nblk = t // PEER_TOK
    prev = lambda c: (jnp.maximum(c - 1, 0), 0)
    return pl.pallas_call(
        _peer_route_up_kernel,
        grid=(nblk + 1,),
        in_specs=[pl.BlockSpec((nk, PEER_TOK), lambda c: (0, jnp.minimum(c, nblk - 1))),
                  pl.BlockSpec((PEER_TOK, d), prev),
                  _table_spec(tab)],
        out_specs=[pl.BlockSpec((PEER_TOK, 2 * PEER_PICKS), prev),
                   pl.BlockSpec((PEER_TOK, PEER_PICKS), lambda c: (jnp.minimum(c, nblk - 1), 0))],
        out_shape=[jax.ShapeDtypeStruct((t, 2 * PEER_PICKS), F32), jax.ShapeDtypeStruct((t, PEER_PICKS), I32)],
        scratch_shapes=[pltpu.VMEM((2, PEER_TOK, PEER_PICKS), I32), pltpu.VMEM((2, PEER_TOK, PEER_PICKS), F32),
                        pltpu.VMEM((PEER_PICKS, PEER_TOK), I32), pltpu.VMEM((PEER_PICKS, PEER_TOK), F32),
                        _tile_scratch(), _tile_scratch(), pltpu.VMEM((PEER_TOK, 2 * PEER_PICKS), F32),
                        pltpu.VMEM((PEER_TOK, d), F32)] + _idx_scratch(),
        compiler_params=_cparams("arbitrary"),
    )(st, hn, tab)


def _peer_down(eidx, w, h2d, tab):
    t, d = h2d.shape
    return pl.pallas_call(
        _peer_down_kernel,
        grid=(t // PEER_TOK,),
        in_specs=[pl.BlockSpec((PEER_TOK, PEER_PICKS), lambda c: (c, 0)),
                  pl.BlockSpec((PEER_TOK, 2 * PEER_PICKS), lambda c: (c, 0)),
                  pl.BlockSpec((PEER_TOK, d), lambda c: (c, 0)),
                  _table_spec(tab)],
        out_specs=pl.BlockSpec((PEER_TOK, d), lambda c: (c, 0)),
        out_shape=jax.ShapeDtypeStruct((t, d), F32),
        scratch_shapes=[_tile_scratch(), _tile_scratch()] + _idx_scratch(),
        compiler_params=_cparams("arbitrary"),
    )(eidx, w, h2d, tab)


def _pad_heads(w, heads, dim, dim_pad=LANES):
    lead = w.shape[:-1]
    w = w.reshape(lead + (heads, dim))
    w = jnp.pad(w, [(0, 0)] * len(lead) + [(0, 0), (0, dim_pad - dim)])
    return w.reshape(lead + (heads * dim_pad,))


def _pad_lanes(v, offset=0, width=LANES):
    return jnp.pad(v, [(0, 0)] * (v.ndim - 1) + [(offset, width - offset - v.shape[-1])])


def _rope_tabs(pos, rot_dim, offset):
    inv = 1.0 / (ROPE_THETA ** (jnp.arange(0, rot_dim, 2, dtype=F32) / rot_dim))
    ang = pos.astype(F32)[:, None] * inv[None, :]
    cos, sin = jnp.cos(ang), jnp.sin(ang)
    zero = jnp.zeros_like(sin)
    c = jnp.pad(jnp.concatenate([cos, cos], -1) - 1.0, ((0, 0), (offset, LANES - offset - rot_dim))) + 1.0
    sa = _pad_lanes(jnp.concatenate([-sin, zero], -1), offset)
    sb = _pad_lanes(jnp.concatenate([zero, sin], -1), offset)
    return c, sa, sb


def _pack_table(tab):
    e, d = tab.shape
    bits = lax.bitcast_convert_type(tab.astype(BF16), jnp.uint16).astype(jnp.uint32)
    word = bits[:, :d // 2] | (bits[:, d // 2:] << 16)
    return lax.bitcast_convert_type(word, I32).reshape(e * PEER_WORD_ROWS, LANES)


def _relayout_w_in(w_in):
    d = w_in.shape[0]
    o = 0
    seg = {}
    for name, size in (("cq", 256), ("ckv", 128), ("kpe", 32), ("qn", 512), ("kc", 128), ("vc", 128),
                       ("ks", 128), ("vs", 128), ("kw", 128), ("vw", 128), ("gate", 24)):
        seg[name] = w_in[:, o:o + size]
        o += size
    gates = [_pad_lanes(seg["gate"][:, j::3]) for j in range(3)]
    cols = [_pad_heads(seg["qn"], NSA_HEADS, NSA_DK),
            _pad_heads(seg["ks"], NSA_GROUPS, NSA_DK), _pad_heads(seg["kw"], NSA_GROUPS, NSA_DK),
            seg["cq"], seg["ckv"], _pad_lanes(seg["kpe"], MLA_NOPE),
            _pad_heads(seg["vs"], NSA_GROUPS, NSA_DK), _pad_heads(seg["vw"], NSA_GROUPS, NSA_DK),
            seg["kc"]] + gates + [seg["vc"]]
    w = jnp.concatenate(cols, axis=1)
    assert w.shape == (d, IN_COLS_PAD)
    return w.astype(BF16)


def _layer(h, mem, p):
    b, s, d = h.shape
    t = b * s
    assert s % SEL_CHUNK == 0 and N_SEL <= s // SEL_LEN <= SEL_BLOCKS_PAD and d == 1024
    pos = jnp.arange(s)

    y = _norm_matmul(h.reshape(t, d), p["mix_norm"], _relayout_w_in(p["w_in"]), 512)
    y3 = y.reshape(b, s, IN_COLS_PAD)

    wuq = _pad_heads(p["mla_w_uq"], MLA_HEADS, MLA_QK).astype(BF16)
    wukv = p["mla_w_ukv"].reshape(MLA_KV_LORA, MLA_HEADS, MLA_NOPE + MLA_V)
    wuk = _pad_heads(wukv[..., :MLA_NOPE].reshape(MLA_KV_LORA, -1), MLA_HEADS, MLA_NOPE).astype(BF16)
    wuv = _pad_heads(wukv[..., MLA_NOPE:].reshape(MLA_KV_LORA, -1), MLA_HEADS, MLA_V).astype(BF16)
    q, k, v = _mla_prep(y3, p["mla_q_norm"][None], p["mla_kv_norm"][None], _pad_lanes(p["mla_qk_norm"]),
                        wuq, wuk, wuv, _rope_tabs(pos, MLA_ROPE, MLA_NOPE), 512)
    o_mla = _mla_attn(q, k, v, 512)

    tabs = _rope_tabs(pos, NSA_ROT, 0)
    qn, ks, kw, vs, vw = _nsa_prep(y3, _pad_lanes(p["nsa_q_norm"][None]), _pad_lanes(p["nsa_k_norm"][1:3]),
                                   tabs, 512)
    nrow = s // CMP_STRIDE

    def blocks16(col):
        tkv = y3[:, :, col:col + NSA_GROUPS * NSA_DK].reshape(b, nrow, CMP_STRIDE, NSA_GROUPS, NSA_DK)
        return tkv.transpose(0, 3, 1, 2, 4).reshape(b, NSA_GROUPS, nrow, CMP_STRIDE * NSA_DK)

    pe = p["nsa_cmp_pe"].reshape(2, 2, CMP_STRIDE * NSA_DK)
    w1 = p["nsa_cmp_w1"].reshape(2, 2, CMP_STRIDE * NSA_DK, -1).astype(BF16)
    w2 = _pad_lanes(p["nsa_cmp_w2"]).astype(BF16)
    cmp_end = jnp.arange(nrow) * CMP_STRIDE + CMP_LEN - 1
    kc, vc = _compress(blocks16(COL_KC), blocks16(COL_VC), pe, w1, w2, _pad_lanes(p["nsa_k_norm"][0:1]),
                       _rope_tabs(cmp_end, NSA_ROT, 0))
    o_nsa = _nsa_attn(qn, kc, vc, ks, vs, kw, vw, y3, NSA_TQ)

    width = MLA_HEADS * MLA_V
    g_out = p["mix_out_norm"]
    gm = _pad_heads(g_out[None, :width], MLA_HEADS, MLA_V)
    gn = _pad_heads(g_out[None, width:], NSA_HEADS, NSA_DK)
    w_out = p["w_out"].reshape(2 * MLA_HEADS, MLA_V, d)
    w_out = jnp.pad(w_out, ((0, 0), (0, LANES - MLA_V), (0, 0))).reshape(2 * MLA_HEADS * LANES, d).astype(BF16)
    h = _mix_out(h, o_mla, o_nsa, gm, gn, w_out, 256)

    kx, vx = _mem_kv(mem, p["mem_norm"][None], p["xa_wkv"].astype(BF16), p["xa_qk_norm"][1:2])
    h = _xattn(h, p["xa_norm"][None], p["xa_wq"].astype(BF16), p["xa_qk_norm"][0:1], kx, vx,
               p["xa_wo"].astype(BF16), 256)

    keys = p["peer_keys"]
    nhp = PEER_HEADS * 2
    eye = jnp.eye(nhp, dtype=F32)
    kbd = (keys.reshape(nhp, PEER_KEYS, PEER_HALF)[:, :, None, :] * eye[:, None, :, None])
    kbd = kbd.reshape(nhp * PEER_KEYS, nhp * PEER_HALF).astype(BF16)
    h2d = h.reshape(t, d)
    hn, st = _peer_score(h2d, p["ffn_norm"][None], p["peer_wq"].astype(BF16), kbd, 512)
    w, eidx = _peer_route_up(st, hn, _pack_table(p["peer_u"]))
    out = _peer_down(eidx, w, h2d, _pack_table(p["peer_v"]))
    return out.reshape(b, s, d)


def kernel(x, mem, mix_norm, w_in, mla_q_norm, mla_kv_norm, mla_w_uq, mla_w_ukv, mla_qk_norm, nsa_q_norm, nsa_k_norm, nsa_cmp_pe, nsa_cmp_w1, nsa_cmp_w2, mix_out_norm, w_out, xa_norm, mem_norm, xa_wq, xa_wkv, xa_qk_norm, xa_wo, ffn_norm, peer_wq, peer_keys, peer_u, peer_v):
    params = dict(mix_norm=mix_norm, w_in=w_in, mla_q_norm=mla_q_norm, mla_kv_norm=mla_kv_norm,
                  mla_w_uq=mla_w_uq, mla_w_ukv=mla_w_ukv, mla_qk_norm=mla_qk_norm, nsa_q_norm=nsa_q_norm,
                  nsa_k_norm=nsa_k_norm, nsa_cmp_pe=nsa_cmp_pe, nsa_cmp_w1=nsa_cmp_w1, nsa_cmp_w2=nsa_cmp_w2,
                  mix_out_norm=mix_out_norm, w_out=w_out, xa_norm=xa_norm, mem_norm=mem_norm, xa_wq=xa_wq,
                  xa_wkv=xa_wkv, xa_qk_norm=xa_qk_norm, xa_wo=xa_wo, ffn_norm=ffn_norm, peer_wq=peer_wq,
                  peer_keys=peer_keys, peer_u=peer_u, peer_v=peer_v)
    h = x
    for layer in range(w_in.shape[0]):
        h = _layer(h, mem, {name: val[layer] for name, val in params.items()})
    return h
```

```python
import functools

import jax
import jax.numpy as jnp
from jax import lax
from jax.experimental import pallas as pl
from jax.experimental.pallas import tpu as pltpu

F32 = jnp.float32
BF16 = jnp.bfloat16
I32 = jnp.int32

LANES = 128
VMEM_LIMIT = 56 * 1024 * 1024

ROPE_THETA = 500000.0
EPS = 1e-6
NEG = -1e30
FORCE = 1e4

MLA_HEADS = 8
MLA_Q_LORA = 256
MLA_KV_LORA = 128
MLA_NOPE = 64
MLA_ROPE = 32
MLA_V = 64
MLA_QK = MLA_NOPE + MLA_ROPE

NSA_HEADS = 8
NSA_GROUPS = 2
NSA_REP = NSA_HEADS // NSA_GROUPS
NSA_DK = 64
NSA_ROT = NSA_DK // 4
CMP_LEN = 32
CMP_STRIDE = 16
SEL_LEN = 64
SEL_SHIFT = SEL_LEN.bit_length() - 1
SEL_BLOCKS_PAD = 64
N_SEL = 16
WIN = 512
SEL_CHUNK = 1024
NSA_TQ = 256
MLA_TQ = 512
MLA_HEADS_PER_STEP = 4
ROW_TILE = 512
MIX_TILE = 256
LOG2E = 1.4426950408889634

XA_HEADS = 4
XA_DH = 128

PEER_HEADS = 8
PEER_KEYS = 128
PEER_HALF = 64
PEER_TOPK = 16
PEER_PICKS = PEER_HEADS * PEER_TOPK
PEER_TOK = 128
PEER_WORD_ROWS = 4
PEER_UNROLL = 16

COL_QN = 0
COL_KS = 1024
COL_KW = 1280
COL_CQ = 1536
COL_CKV = 1792
COL_KPE = 1920
COL_VS = 2048
COL_VW = 2304
COL_KC = 2560
COL_GATE = 2688
COL_VC = 3072
IN_COLS_PAD = 3200


def _cparams(*sem):
    return pltpu.CompilerParams(dimension_semantics=sem, vmem_limit_bytes=VMEM_LIMIT)


def _rms(x, g, n):
    ms = jnp.sum(x * x, axis=-1, keepdims=True) * (1.0 / n)
    return x * lax.rsqrt(ms + EPS) * g


def _rope(x, c, sa, sb, half):
    return x * c + pltpu.roll(x, LANES - half, 1) * sa + pltpu.roll(x, half, 1) * sb


def _gelu(x):
    return 0.5 * x * (1.0 + jnp.tanh(0.7978845608028654 * (x + 0.044715 * (x * x * x))))


def _dot_nt(a, b, **kw):
    return lax.dot_general(a, b, (((1,), (1,)), ((), ())), preferred_element_type=F32, **kw)


def _norm_matmul_kernel(x_ref, g_ref, w_ref, o_ref):
    x = x_ref[...]
    n = _rms(x, g_ref[...], x.shape[-1])
    o_ref[...] = jnp.dot(n.astype(BF16), w_ref[...], preferred_element_type=F32)


def _norm_matmul(x2d, g, w, tm):
    t, d = x2d.shape
    n = w.shape[1]
    return pl.pallas_call(
        _norm_matmul_kernel,
        grid=(t // tm,),
        in_specs=[pl.BlockSpec((tm, d), lambda i: (i, 0)),
                  pl.BlockSpec((1, d), lambda i: (0, 0)),
                  pl.BlockSpec((d, n), lambda i: (0, 0))],
        out_specs=pl.BlockSpec((tm, n), lambda i: (i, 0)),
        out_shape=jax.ShapeDtypeStruct((t, n), F32),
        compiler_params=_cparams("parallel"),
    )(x2d, g.reshape(1, d), w)


def _mla_prep_kernel(y_ref, gq_ref, gkv_ref, gqk_ref, wuq_ref, wuk_ref, wuv_ref,
                     c_ref, sa_ref, sb_ref, q_ref, k_ref, v_ref):
    y = y_ref[0]
    cq = y[:, 0:MLA_Q_LORA]
    ckv = y[:, MLA_Q_LORA:MLA_Q_LORA + MLA_KV_LORA]
    kpe = y[:, MLA_Q_LORA + MLA_KV_LORA:]
    nq = _rms(cq, gq_ref[...], MLA_Q_LORA).astype(BF16)
    nkv = _rms(ckv, gkv_ref[...], MLA_KV_LORA).astype(BF16)
    qf = jnp.dot(nq, wuq_ref[...], preferred_element_type=F32)
    kf = jnp.dot(nkv, wuk_ref[...], preferred_element_type=F32)
    vf = jnp.dot(nkv, wuv_ref[...], preferred_element_type=F32)
    c, sa, sb = c_ref[...], sa_ref[...], sb_ref[...]
    gq = gqk_ref[0:1, :]
    gk = gqk_ref[1:2, :]
    scale = MLA_QK ** -0.5 * LOG2E
    for h in range(MLA_HEADS):
        sl = slice(h * LANES, (h + 1) * LANES)
        qh = _rope(_rms(qf[:, sl], gq, MLA_QK), c, sa, sb, MLA_ROPE // 2)
        kh = _rope(_rms(kf[:, sl] + kpe, gk, MLA_QK), c, sa, sb, MLA_ROPE // 2)
        q_ref[0, h] = (qh * scale).astype(BF16)
        k_ref[0, h] = kh.astype(BF16)
        v_ref[0, h] = vf[:, sl].astype(BF16)


def _mla_prep(y3, gq, gkv, gqk, wuq, wuk, wuv, tabs, tm):
    b, s, _ = y3.shape
    hs = jax.ShapeDtypeStruct((b, MLA_HEADS, s, LANES), BF16)
    full = lambda shape: pl.BlockSpec(shape, lambda bi, i: (0,) * len(shape))
    tab = pl.BlockSpec((tm, LANES), lambda bi, i: (i, 0))
    hspec = pl.BlockSpec((1, MLA_HEADS, tm, LANES), lambda bi, i: (bi, 0, i, 0))
    return pl.pallas_call(
        _mla_prep_kernel,
        grid=(b, s // tm),
        in_specs=[pl.BlockSpec((1, tm, 512), lambda bi, i: (bi, i, COL_CQ // 512)),
                  full((1, MLA_Q_LORA)), full((1, MLA_KV_LORA)), full((2, LANES)),
                  full((MLA_Q_LORA, MLA_HEADS * LANES)), full((MLA_KV_LORA, MLA_HEADS * LANES)),
                  full((MLA_KV_LORA, MLA_HEADS * LANES)), tab, tab, tab],
        out_specs=[hspec, hspec, hspec],
        out_shape=[hs, hs, hs],
        compiler_params=_cparams("parallel", "parallel"),
    )(y3, gq, gkv, gqk, wuq, wuk, wuv, *tabs)


def _nsa_prep_kernel(yq_ref, yv_ref, gq_ref, gk_ref, c_ref, sa_ref, sb_ref,
                     q_ref, ks_ref, kw_ref, vs_ref, vw_ref):
    c, sa, sb = c_ref[...], sa_ref[...], sb_ref[...]
    half = NSA_ROT // 2
    scale = NSA_DK ** -0.5 * LOG2E
    yq = yq_ref[0]
    for h in range(NSA_HEADS):
        qh = _rms(yq[:, h * LANES:(h + 1) * LANES], gq_ref[...], NSA_DK)
        q_ref[0, h] = (_rope(qh, c, sa, sb, half) * scale).astype(BF16)
    for g in range(NSA_GROUPS):
        ks = yq[:, COL_KS + g * LANES:COL_KS + (g + 1) * LANES]
        kw = yq[:, COL_KW + g * LANES:COL_KW + (g + 1) * LANES]
        ks_ref[0, g] = _rope(_rms(ks, gk_ref[0:1, :], NSA_DK), c, sa, sb, half).astype(BF16)
        kw_ref[0, g] = _rope(_rms(kw, gk_ref[1:2, :], NSA_DK), c, sa, sb, half).astype(BF16)
    yv = yv_ref[0]
    for g in range(NSA_GROUPS):
        vs_ref[0, g] = yv[:, g * LANES:(g + 1) * LANES].astype(BF16)
        vw_ref[0, g] = yv[:, (NSA_GROUPS + g) * LANES:(NSA_GROUPS + g + 1) * LANES].astype(BF16)


def _nsa_prep(y3, gq, gk, tabs, tm):
    b, s, _ = y3.shape
    full = lambda shape: pl.BlockSpec(shape, lambda bi, i: (0,) * len(shape))
    tab = pl.BlockSpec((tm, LANES), lambda bi, i: (i, 0))
    qs = jax.ShapeDtypeStruct((b, NSA_HEADS, s, LANES), BF16)
    gs = jax.ShapeDtypeStruct((b, NSA_GROUPS, s, LANES), BF16)
    gspec = pl.BlockSpec((1, NSA_GROUPS, tm, LANES), lambda bi, i: (bi, 0, i, 0))
    return pl.pallas_call(
        _nsa_prep_kernel,
        grid=(b, s // tm),
        in_specs=[pl.BlockSpec((1, tm, 1536), lambda bi, i: (bi, i, 0)),
                  pl.BlockSpec((1, tm, 512), lambda bi, i: (bi, i, COL_VS // 512)),
                  full((1, LANES)), full((2, LANES)), tab, tab, tab],
        out_specs=[pl.BlockSpec((1, NSA_HEADS, tm, LANES), lambda bi, i: (bi, 0, i, 0)),
                   gspec, gspec, gspec, gspec],
        out_shape=[qs, gs, gs, gs, gs],
        compiler_params=_cparams("parallel", "parallel"),
    )(y3, y3, gq, gk, *tabs)


def _compress_kernel(tk_ref, tv_ref, pe_ref, w1_ref, w2_ref, gk_ref, c_ref, sa_ref, sb_ref,
                     kc_ref, vc_ref):
    nrow = tk_ref.shape[2]

    def mlp(t, kv):
        ta = (t + pe_ref[kv, 0:1, :]).astype(BF16)
        tb = (t + pe_ref[kv, 1:2, :]).astype(BF16)
        za = jnp.dot(ta, w1_ref[kv, 0], preferred_element_type=F32)
        zb = jnp.dot(tb, w1_ref[kv, 1], preferred_element_type=F32)
        hid = _gelu(za + pltpu.roll(zb, nrow - 1, 0))
        return jnp.dot(hid.astype(BF16), w2_ref[kv], preferred_element_type=F32)

    kc = mlp(tk_ref[0, 0], 0)
    kc = _rope(_rms(kc, gk_ref[...], NSA_DK), c_ref[...], sa_ref[...], sb_ref[...], NSA_ROT // 2)
    kc_ref[0, 0] = kc.astype(BF16)
    vc_ref[0, 0] = mlp(tv_ref[0, 0], 1).astype(BF16)


def _compress(tk16, tv16, pe, w1, w2, gk, tabs):
    b, g, nrow, width = tk16.shape
    full = lambda shape: pl.BlockSpec(shape, lambda bi, gi: (0,) * len(shape))
    tspec = pl.BlockSpec((1, 1, nrow, width), lambda bi, gi: (bi, gi, 0, 0))
    ospec = pl.BlockSpec((1, 1, nrow, LANES), lambda bi, gi: (bi, gi, 0, 0))
    os_ = jax.ShapeDtypeStruct((b, g, nrow, LANES), BF16)
    return pl.pallas_call(
        _compress_kernel,
        grid=(b, g),
        in_specs=[tspec, tspec, full(pe.shape), full(w1.shape), full(w2.shape), full((1, LANES)),
                  full((nrow, LANES)), full((nrow, LANES)), full((nrow, LANES))],
        out_specs=[ospec, ospec],
        out_shape=[os_, os_],
        compiler_params=_cparams("parallel", "parallel"),
    )(tk16, tv16, pe, w1, w2, gk, *tabs)


def _softmax_init(heads, tq):
    return (jnp.full((heads, tq, 1), NEG, F32), jnp.zeros((heads, tq, 1), F32),
            jnp.zeros((heads * tq, LANES), F32))


def _softmax_step(carry, q, k_ref, v_ref, start, width, mask):
    m, l, acc = carry
    heads, tq, _ = m.shape
    k = k_ref[pl.ds(start, width), :]
    v = v_ref[pl.ds(start, width), :]
    s = _dot_nt(q, k).reshape(heads, tq, width)
    if mask is not None:
        s = jnp.where(mask[None], s, NEG)
    m_new = jnp.maximum(m, jnp.max(s, axis=-1, keepdims=True))
    p = jnp.exp2(s - m_new)
    alpha = jnp.exp2(m - m_new)
    l = alpha * l + jnp.sum(p, axis=-1, keepdims=True)
    pv = jnp.dot(p.reshape(heads * tq, width).astype(BF16), v, preferred_element_type=F32)
    return m_new, l, alpha.reshape(heads * tq, 1) * acc + pv


def _softmax_finish(carry):
    _, l, acc = carry
    return acc * (1.0 / l).reshape(acc.shape[0], 1)


def _mla_attn_kernel(q_ref, k_ref, v_ref, o_ref, *, tq):
    q0 = pl.program_id(2) * tq
    heads = range(q_ref.shape[1])
    wide = 2 * tq
    nwide = q0 // wide

    def step(carry, start, width, mask):
        return tuple(_softmax_step(carry[h], q_ref[0, h], k_ref.at[0, h], v_ref.at[0, h], start, width, mask)
                     for h in heads)

    carry = lax.fori_loop(0, nwide, lambda j, c: step(c, pl.multiple_of(j * wide, wide), wide, None),
                          tuple(_softmax_init(1, tq) for _ in heads))
    carry = lax.fori_loop(0, (q0 - nwide * wide) // tq,
                          lambda j, c: step(c, pl.multiple_of(nwide * wide, tq), tq, None), carry)
    causal = lax.broadcasted_iota(I32, (tq, tq), 1) <= lax.broadcasted_iota(I32, (tq, tq), 0)
    carry = step(carry, pl.multiple_of(q0, tq), tq, causal)
    for h in heads:
        o_ref[0, :, h * LANES:(h + 1) * LANES] = _softmax_finish(carry[h])


def _mla_attn(q, k, v, tq):
    b, h, s, _ = q.shape
    hg = MLA_HEADS_PER_STEP
    kv = pl.BlockSpec((1, hg, s, LANES), lambda bi, hi, i: (bi, hi, 0, 0))
    return pl.pallas_call(
        functools.partial(_mla_attn_kernel, tq=tq),
        grid=(b, h // hg, s // tq),
        in_specs=[pl.BlockSpec((1, hg, tq, LANES), lambda bi, hi, i: (bi, hi, i, 0)), kv, kv],
        out_specs=pl.BlockSpec((1, tq, hg * LANES), lambda bi, hi, i: (bi, i, hi)),
        out_shape=jax.ShapeDtypeStruct((b, s, h * LANES), F32),
        compiler_params=_cparams("parallel", "parallel", "arbitrary"),
    )(q, k, v)


def _compressed_branch(q, kc, vc, q0, tq):
    ncmp = kc.shape[0]
    s = _dot_nt(q, kc).reshape(NSA_REP, tq, ncmp)
    qpos = q0 + lax.broadcasted_iota(I32, (tq, ncmp), 0)
    n = lax.broadcasted_iota(I32, (tq, ncmp), 1)
    msk = (n * CMP_STRIDE + (CMP_LEN - 1) <= qpos)[None]
    s = jnp.where(msk, s, NEG)
    m = jnp.max(s, axis=-1, keepdims=True)
    p = jnp.where(msk, jnp.exp2(s - m), 0.0)
    l = jnp.sum(p, axis=-1, keepdims=True)
    p = p * jnp.where(l > 0.0, 1.0 / l, 0.0)
    o = jnp.dot(p.reshape(NSA_REP * tq, ncmp).astype(BF16), vc, preferred_element_type=F32)

    nblk = SEL_BLOCKS_PAD
    psum = p[0] + p[1] + p[2] + p[3]
    bm = lax.broadcasted_iota(I32, (nblk, ncmp), 0)
    bn = lax.broadcasted_iota(I32, (nblk, ncmp), 1)
    overlap = ((bn * CMP_STRIDE < bm * SEL_LEN + SEL_LEN)
               & (bn * CMP_STRIDE + CMP_LEN > bm * SEL_LEN)).astype(F32)
    imp = _dot_nt(overlap, psum, precision=lax.Precision.HIGHEST)
    blk = lax.broadcasted_iota(I32, (nblk, tq), 0)
    cur = (q0 + lax.broadcasted_iota(I32, (nblk, tq), 1)) >> SEL_SHIFT
    forced = (blk == 0) | (blk == cur) | (blk == cur - 1)
    score = jnp.where(blk <= cur, jnp.where(forced, FORCE, imp), NEG)
    rank = jnp.zeros((nblk, tq), I32)
    for j in range(nblk):
        row = score[j:j + 1, :]
        rank = rank + jnp.where(row > score, 1, jnp.where(row == score, (blk > j).astype(I32), 0))
    sel = jnp.where(rank < N_SEL, (blk <= cur).astype(F32), 0.0)
    sel = jnp.concatenate([sel, jnp.zeros((LANES - nblk, tq), F32)], axis=0)
    return o, sel.T


def _selected_branch(q, kr, vr, sel, q0, tq, tk):
    per_chunk = tk // SEL_LEN
    eb = lax.broadcasted_iota(I32, (LANES, tk), 0)
    ec = lax.broadcasted_iota(I32, (LANES, tk), 1)
    expand = ((ec >> SEL_SHIFT) == eb).astype(BF16)

    def chosen(j):
        local = pltpu.roll(sel, (LANES - j * per_chunk) % LANES, 1).astype(BF16)
        return jnp.dot(local, expand, preferred_element_type=F32) > 0.5

    nfull = q0 // tk
    carry = lax.fori_loop(
        0, nfull, lambda j, c: _softmax_step(c, q, kr, vr, pl.multiple_of(j * tk, tk), tk, chosen(j)),
        _softmax_init(NSA_REP, tq))
    start = pl.multiple_of(nfull * tk, tk)
    causal = start + lax.broadcasted_iota(I32, (tq, tk), 1) <= q0 + lax.broadcasted_iota(I32, (tq, tk), 0)
    carry = _softmax_step(carry, q, kr, vr, start, tk, chosen(nfull) & causal)
    return _softmax_finish(carry)


def _window_branch(q, kr, vr, q0, tq, tk):
    start = pl.multiple_of(jnp.maximum(q0 + tq - tk, 0), tq)
    dist = (q0 + lax.broadcasted_iota(I32, (tq, tk), 0)) - (start + lax.broadcasted_iota(I32, (tq, tk), 1))
    carry = _softmax_step(_softmax_init(NSA_REP, tq), q, kr, vr, start, tk, (dist >= 0) & (dist < WIN))
    return _softmax_finish(carry)


def _nsa_attn_kernel(q_ref, kc_ref, vc_ref, ks_ref, vs_ref, kw_ref, vw_ref, gate_ref, o_ref, *, tq):
    group = pl.program_id(1)
    q0 = pl.program_id(2) * tq
    q = q_ref[0].reshape(NSA_REP * tq, LANES)
    o_win = _window_branch(q, kw_ref.at[0, 0], vw_ref.at[0, 0], q0, tq, WIN + tq)
    o_cmp, sel = _compressed_branch(q, kc_ref[0, 0], vc_ref[0, 0], q0, tq)
    o_sel = _selected_branch(q, ks_ref.at[0, 0], vs_ref.at[0, 0], sel, q0, tq, SEL_CHUNK)
    gate = jax.nn.sigmoid(gate_ref[0])
    shift = (LANES - NSA_REP * group) % LANES
    gates = [pltpu.roll(gate[:, j * LANES:(j + 1) * LANES], shift, 1) for j in range(3)]
    for r in range(NSA_REP):
        rows = slice(r * tq, (r + 1) * tq)
        o_ref[0, :, r * LANES:(r + 1) * LANES] = (gates[0][:, r:r + 1] * o_cmp[rows]
                                                  + gates[1][:, r:r + 1] * o_sel[rows]
                                                  + gates[2][:, r:r + 1] * o_win[rows])


def _nsa_attn(q, kc, vc, ks, vs, kw, vw, gates, tq):
    b, _, s, _ = q.shape
    ncmp = kc.shape[2]
    assert s % SEL_CHUNK == 0 and SEL_CHUNK % tq == 0 and s >= WIN + tq
    kv = pl.BlockSpec((1, 1, s, LANES), lambda bi, gi, i: (bi, gi, 0, 0))
    cspec = pl.BlockSpec((1, 1, ncmp, LANES), lambda bi, gi, i: (bi, gi, 0, 0))
    return pl.pallas_call(
        functools.partial(_nsa_attn_kernel, tq=tq),
        grid=(b, NSA_GROUPS, s // tq),
        in_specs=[pl.BlockSpec((1, NSA_REP, tq, LANES), lambda bi, gi, i: (bi, gi, i, 0)),
                  cspec, cspec, kv, kv, kv, kv,
                  pl.BlockSpec((1, tq, 3 * LANES), lambda bi, gi, i: (bi, i, COL_GATE // (3 * LANES)))],
        out_specs=pl.BlockSpec((1, tq, NSA_REP * LANES), lambda bi, gi, i: (bi, i, gi)),
        out_shape=jax.ShapeDtypeStruct((b, s, NSA_HEADS * LANES), F32),
        compiler_params=_cparams("parallel", "parallel", "arbitrary"),
    )(q, kc, vc, ks, vs, kw, vw, gates)


def _mix_out_kernel(x_ref, om_ref, on_ref, gm_ref, gn_ref, w_ref, o_ref):
    width = MLA_HEADS * MLA_V
    mixed = jnp.concatenate([_rms(om_ref[0], gm_ref[...], width), _rms(on_ref[0], gn_ref[...], width)], axis=-1)
    o_ref[0] = x_ref[0] + jnp.dot(mixed.astype(BF16), w_ref[...], preferred_element_type=F32)


def _mix_out(x, om, on, gm, gn, w, tm):
    b, s, d = x.shape
    wide = om.shape[-1]
    full = lambda shape: pl.BlockSpec(shape, lambda bi, i: (0,) * len(shape))
    row = lambda width: pl.BlockSpec((1, tm, width), lambda bi, i: (bi, i, 0))
    return pl.pallas_call(
        _mix_out_kernel,
        grid=(b, s // tm),
        in_specs=[row(d), row(wide), row(wide), full((1, wide)), full((1, wide)), full(w.shape)],
        out_specs=row(d),
        out_shape=jax.ShapeDtypeStruct((b, s, d), F32),
        compiler_params=_cparams("parallel", "parallel"),
    )(x, om, on, gm, gn, w)


def _mem_kv_kernel(m_ref, g_ref, w_ref, gk_ref, k_ref, v_ref):
    mem = m_ref[0]
    n = _rms(mem, g_ref[...], mem.shape[-1]).astype(BF16)
    kv = jnp.dot(n, w_ref[...], preferred_element_type=F32)
    width = XA_HEADS * XA_DH
    for h in range(XA_HEADS):
        sl = slice(h * XA_DH, (h + 1) * XA_DH)
        k_ref[0, :, sl] = _rms(kv[:, sl], gk_ref[...], XA_DH).astype(BF16)
    v_ref[0] = kv[:, width:].astype(BF16)


def _mem_kv(mem, g, w, gk):
    b, m, d = mem.shape
    width = XA_HEADS * XA_DH
    full = lambda shape: pl.BlockSpec(shape, lambda bi: (0,) * len(shape))
    ospec = pl.BlockSpec((1, m, width), lambda bi: (bi, 0, 0))
    os_ = jax.ShapeDtypeStruct((b, m, width), BF16)
    return pl.pallas_call(
        _mem_kv_kernel,
        grid=(b,),
        in_specs=[pl.BlockSpec((1, m, d), lambda bi: (bi, 0, 0)), full((1, d)), full(w.shape), full((1, XA_DH))],
        out_specs=[ospec, ospec],
        out_shape=[os_, os_],
        compiler_params=_cparams("parallel"),
    )(mem, g, w, gk)


def _xattn_kernel(h_ref, g_ref, wq_ref, gq_ref, k_ref, v_ref, wo_ref, o_ref):
    h = h_ref[0]
    hn = _rms(h, g_ref[...], h.shape[-1]).astype(BF16)
    q = jnp.dot(hn, wq_ref[...], preferred_element_type=F32)
    outs = []
    for hd in range(XA_HEADS):
        sl = slice(hd * XA_DH, (hd + 1) * XA_DH)
        qh = (_rms(q[:, sl], gq_ref[...], XA_DH) * (XA_DH ** -0.5)).astype(BF16)
        s = _dot_nt(qh, k_ref[0, :, sl])
        p = jnp.exp(s - jnp.max(s, axis=-1, keepdims=True))
        p = p * (1.0 / jnp.sum(p, axis=-1, keepdims=True))
        outs.append(jnp.dot(p.astype(BF16), v_ref[0, :, sl], preferred_element_type=F32))
    o = jnp.concatenate(outs, axis=-1).astype(BF16)
    o_ref[0] = h + jnp.dot(o, wo_ref[...], preferred_element_type=F32)


def _xattn(h, g, wq, gq, k, v, wo, tm):
    b, s, d = h.shape
    m, width = k.shape[1], k.shape[2]
    full = lambda shape: pl.BlockSpec(shape, lambda bi, i: (0,) * len(shape))
    kv = pl.BlockSpec((1, m, width), lambda bi, i: (bi, 0, 0))
    row = pl.BlockSpec((1, tm, d), lambda bi, i: (bi, i, 0))
    return pl.pallas_call(
        _xattn_kernel,
        grid=(b, s // tm),
        in_specs=[row, full((1, d)), full(wq.shape), full((1, XA_DH)), kv, kv, full(wo.shape)],
        out_specs=row,
        out_shape=jax.ShapeDtypeStruct((b, s, d), F32),
        compiler_params=_cparams("parallel", "parallel"),
    )(h, g, wq, gq, k, v, wo)


def _peer_score_kernel(h_ref, g_ref, wq_ref, kbd_ref, hn_ref, s_ref):
    h = h_ref[...]
    hn = _rms(h, g_ref[...], h.shape[-1]).astype(BF16)
    hn_ref[...] = hn
    q = jnp.dot(hn, wq_ref[...], preferred_element_type=F32).astype(BF16)
    s_ref[...] = _dot_nt(kbd_ref[...], q)


def _peer_score(h2d, g, wq, kbd, tm):
    t, d = h2d.shape
    nk = kbd.shape[0]
    full = lambda shape: pl.BlockSpec(shape, lambda i: (0,) * len(shape))
    return pl.pallas_call(
        _peer_score_kernel,
        grid=(t // tm,),
        in_specs=[pl.BlockSpec((tm, d), lambda i: (i, 0)), full((1, d)), full(wq.shape), full(kbd.shape)],
        out_specs=[pl.BlockSpec((tm, d), lambda i: (i, 0)), pl.BlockSpec((nk, tm), lambda i: (0, i))],
        out_shape=[jax.ShapeDtypeStruct((t, d), BF16), jax.ShapeDtypeStruct((nk, t), F32)],
        compiler_params=_cparams("parallel"),
    )(h2d, g, wq, kbd)


def _topk_rows(x, k, ids, payload=None):
    vals, outs = [], []
    for _ in range(k):
        m = jnp.max(x, axis=0, keepdims=True)
        idx = jnp.min(jnp.where(x == m, ids, jnp.iinfo(jnp.int32).max), axis=0, keepdims=True)
        hit = ids == idx
        vals.append(m)
        if payload is None:
            outs.append(idx)
        else:
            outs.append(jnp.max(jnp.where(hit, payload, -1), axis=0, keepdims=True))
        x = jnp.where(hit, -jnp.inf, x)
    return jnp.concatenate(vals, axis=0), jnp.concatenate(outs, axis=0)


PEER_EDGE = 4
assert (PEER_EDGE + 1) ** 2 > PEER_TOPK


def _route_head(s_ref, h, e_scr, g_scr):
    tm = s_ref.shape[1]
    key_ids = lax.broadcasted_iota(I32, (PEER_KEYS, tm), 0)
    a_ids = lax.broadcasted_iota(I32, (PEER_TOPK, tm), 0)
    base = pl.multiple_of(h * (2 * PEER_KEYS), 2 * PEER_KEYS)
    s1, i1 = _topk_rows(s_ref[pl.ds(base, PEER_KEYS), :], PEER_TOPK, key_ids)
    s2, i2 = _topk_rows(s_ref[pl.ds(base + PEER_KEYS, PEER_KEYS), :], PEER_TOPK, key_ids)
    i1 = i1 * PEER_KEYS
    cand, flat, cidx = [], [], []
    for a in range(PEER_EDGE):
        cand.append(s1[a:a + 1, :] + s2)
        flat.append(a * PEER_TOPK + a_ids)
        cidx.append(i1[a:a + 1, :] + i2)
    for b in range(PEER_EDGE):
        cand.append(jnp.where(a_ids >= PEER_EDGE, s1 + s2[b:b + 1, :], -jnp.inf))
        flat.append(a_ids * PEER_TOPK + b)
        cidx.append(i1 + i2[b:b + 1, :])
    top, eidx = _topk_rows(jnp.concatenate(cand, axis=0), PEER_TOPK, jnp.concatenate(flat, axis=0),
                           payload=jnp.concatenate(cidx, axis=0))
    p = jnp.exp(top - top[0:1, :])
    p = p * (1.0 / jnp.sum(p, axis=0, keepdims=True))
    row = pl.multiple_of(h * PEER_TOPK, PEER_TOPK)
    e_scr[pl.ds(row, PEER_TOPK), :] = eidx * PEER_WORD_ROWS
    g_scr[pl.ds(row, PEER_TOPK), :] = p


def _gather_rows(idx_ref, tab_ref, tile_ref, j):
    for k in range(PEER_PICKS):
        r = pl.multiple_of(idx_ref[j, k], PEER_WORD_ROWS)
        tile_ref[pl.ds(k * PEER_WORD_ROWS, PEER_WORD_ROWS), :] = tab_ref[pl.ds(r, PEER_WORD_ROWS), :]


def _tile_rows(tile_ref):
    chunks = [pltpu.bitcast(tile_ref[pl.ds(c, PEER_PICKS, stride=PEER_WORD_ROWS), :], BF16)
              for c in range(PEER_WORD_ROWS)]
    return jnp.concatenate(chunks, axis=1)


def _token_loop(idx_vm, idx_sm, sem, tab_ref, tile_a, tile_b, compute, per_trip=None):
    ntrip = PEER_TOK // PEER_UNROLL

    def fetch(t, slot):
        rows = pl.ds(pl.multiple_of(t * PEER_UNROLL, PEER_UNROLL), PEER_UNROLL)
        return pltpu.make_async_copy(idx_vm.at[rows], idx_sm.at[slot], sem.at[slot])

    fetch(0, 0).start()
    fetch(0, 0).wait()
    _gather_rows(idx_sm.at[0], tab_ref, tile_a, 0)

    def group(i, carry):
        for slot in range(2):
            t = 2 * i + slot
            ahead = jnp.minimum(t + 1, ntrip - 1)
            fetch(ahead, 1 - slot).start()
            if per_trip is not None:
                per_trip(t)
            for u in range(PEER_UNROLL):
                cur, nxt = (tile_a, tile_b) if u % 2 == 0 else (tile_b, tile_a)
                rows = _tile_rows(cur)
                if u + 1 < PEER_UNROLL:
                    _gather_rows(idx_sm.at[slot], tab_ref, nxt, u + 1)
                else:
                    fetch(ahead, 1 - slot).wait()
                    _gather_rows(idx_sm.at[1 - slot], tab_ref, nxt, 0)
                compute(t * PEER_UNROLL + u, rows)
        return carry

    assert PEER_UNROLL % 2 == 0 and ntrip % 2 == 0
    lax.fori_loop(0, ntrip // 2, group, 0)


def _idx_scratch():
    return [pltpu.SMEM((2, PEER_UNROLL, PEER_PICKS), I32), pltpu.SemaphoreType.DMA((2,))]


def _pair_matrix():
    r = lax.broadcasted_iota(I32, (2 * PEER_PICKS, PEER_PICKS), 0)
    c = lax.broadcasted_iota(I32, (2 * PEER_PICKS, PEER_PICKS), 1)
    return (r >> 1) == c


def _peer_route_up_kernel(s_ref, x_ref, tab_ref, w_ref, e_ref, ebuf_ref, gbuf_ref, e_scr, g_scr,
                          tile_a, tile_b, act_ref, xf_ref, idx_sm, sem):
    c = pl.program_id(0)
    cur, nxt = (c + 1) % 2, c % 2
    half = x_ref.shape[1] // 2
    sub = lax.broadcasted_iota(I32, (16, half), 0)
    odd_lane = (lax.broadcasted_iota(I32, (1, 2 * PEER_PICKS), 1) & 1) == 1

    @pl.when(c == 0)
    def _():
        ebuf_ref[1] = jnp.zeros(ebuf_ref.shape[1:], I32)
        gbuf_ref[1] = jnp.zeros(gbuf_ref.shape[1:], F32)

    xf_ref[...] = x_ref[...].astype(F32)

    def compute(j, rows):
        xr = xf_ref[pl.ds(j, 1), :]
        lo = jnp.broadcast_to(xr[:, :half], (16, half))
        hi = jnp.broadcast_to(xr[:, half:], (16, half))
        xsel = jnp.where(sub == 0, lo, jnp.where(sub == 1, hi, 0.0)).astype(BF16)
        r = _dot_nt(xsel, rows)
        act_ref[pl.ds(j, 1), :] = jnp.where(odd_lane, r[1:2, :], r[0:1, :])

    assert PEER_TOK // PEER_UNROLL == PEER_HEADS
    _token_loop(ebuf_ref.at[cur], idx_sm, sem, tab_ref, tile_a, tile_b, compute,
                per_trip=lambda h: _route_head(s_ref, h, e_scr, g_scr))
    pair = _pair_matrix()
    act = jnp.dot(act_ref[...], pair.astype(F32), preferred_element_type=F32,
                  precision=lax.Precision.HIGHEST)
    w = (gbuf_ref[cur] * _gelu(act)).astype(BF16)
    w_ref[...] = _dot_nt(w, pair.astype(BF16))

    routed = e_scr[...].T
    e_ref[...] = routed
    ebuf_ref[nxt] = routed
    gbuf_ref[nxt] = g_scr[...].T


def _peer_down_kernel(idx_ref, w_ref, h_ref, tab_ref, o_ref, tile_a, tile_b, idx_sm, sem):
    half = h_ref.shape[1] // 2
    shape = (16, 2 * PEER_PICKS)
    parity = (lax.broadcasted_iota(I32, shape, 1) & 1) == lax.broadcasted_iota(I32, shape, 0)

    def compute(j, rows):
        wrow = jnp.broadcast_to(w_ref[pl.ds(j, 1), :], shape)
        wm = jnp.where(parity, wrow, 0.0).astype(BF16)
        o2 = jnp.dot(wm, rows, preferred_element_type=F32)
        o_ref[pl.ds(j, 1), 0:half] = h_ref[pl.ds(j, 1), 0:half] + o2[0:1, :]
        o_ref[pl.ds(j, 1), half:] = h_ref[pl.ds(j, 1), half:] + o2[1:2, :]

    _token_loop(idx_ref, idx_sm, sem, tab_ref, tile_a, tile_b, compute)


def _tile_scratch():
    return pltpu.VMEM((PEER_WORD_ROWS * PEER_PICKS, LANES), I32)


def _table_spec(tab):
    return pl.BlockSpec(tab.shape, lambda c: (0, 0), pipeline_mode=pl.Buffered(1))


def _peer_route_up(st, hn, tab):
    t, d = hn.shape
    nk = st.shape[0]
    nblk = t // PEER_TOK
    prev = lambda c: (jnp.maximum(c - 1, 0), 0)
    return pl.pallas_call(
        _peer_route_up_kernel,
        grid=(nblk + 1,),
        in_specs=[pl.BlockSpec((nk, PEER_TOK), lambda c: (0, jnp.minimum(c, nblk - 1))),
                  pl.BlockSpec((PEER_TOK, d), prev),
                  _table_spec(tab)],
        out_specs=[pl.BlockSpec((PEER_TOK, 2 * PEER_PICKS), prev),
                   pl.BlockSpec((PEER_TOK, PEER_PICKS), lambda c: (jnp.minimum(c, nblk - 1), 0))],
        out_shape=[jax.ShapeDtypeStruct((t, 2 * PEER_PICKS), F32), jax.ShapeDtypeStruct((t, PEER_PICKS), I32)],
        scratch_shapes=[pltpu.VMEM((2, PEER_TOK, PEER_PICKS), I32), pltpu.VMEM((2, PEER_TOK, PEER_PICKS), F32),
                        pltpu.VMEM((PEER_PICKS, PEER_TOK), I32), pltpu.VMEM((PEER_PICKS, PEER_TOK), F32),
                        _tile_scratch(), _tile_scratch(), pltpu.VMEM((PEER_TOK, 2 * PEER_PICKS), F32),
                        pltpu.VMEM((PEER_TOK, d), F32)] + _idx_scratch(),
        compiler_params=_cparams("arbitrary"),
    )(st, hn, tab)


def _peer_down(eidx, w, h2d, tab):
    t, d = h2d.shape
    return pl.pallas_call(
        _peer_down_kernel,
        grid=(t // PEER_TOK,),
        in_specs=[pl.BlockSpec((PEER_TOK, PEER_PICKS), lambda c: (c, 0)),
                  pl.BlockSpec((PEER_TOK, 2 * PEER_PICKS), lambda c: (c, 0)),
                  pl.BlockSpec((PEER_TOK, d), lambda c: (c, 0)),
                  _table_spec(tab)],
        out_specs=pl.BlockSpec((PEER_TOK, d), lambda c: (c, 0)),
        out_shape=jax.ShapeDtypeStruct((t, d), F32),
        scratch_shapes=[_tile_scratch(), _tile_scratch()] + _idx_scratch(),
        compiler_params=_cparams("arbitrary"),
    )(eidx, w, h2d, tab)


def _pad_heads(w, heads, dim, dim_pad=LANES):
    lead = w.shape[:-1]
    w = w.reshape(lead + (heads, dim))
    w = jnp.pad(w, [(0, 0)] * len(lead) + [(0, 0), (0, dim_pad - dim)])
    return w.reshape(lead + (heads * dim_pad,))


def _pad_lanes(v, offset=0, width=LANES):
    return jnp.pad(v, [(0, 0)] * (v.ndim - 1) + [(offset, width - offset - v.shape[-1])])


def _rope_tabs(pos, rot_dim, offset):
    inv = 1.0 / (ROPE_THETA ** (jnp.arange(0, rot_dim, 2, dtype=F32) / rot_dim))
    ang = pos.astype(F32)[:, None] * inv[None, :]
    cos, sin = jnp.cos(ang), jnp.sin(ang)
    zero = jnp.zeros_like(sin)
    c = jnp.pad(jnp.concatenate([cos, cos], -1) - 1.0, ((0, 0), (offset, LANES - offset - rot_dim))) + 1.0
    sa = _pad_lanes(jnp.concatenate([-sin, zero], -1), offset)
    sb = _pad_lanes(jnp.concatenate([zero, sin], -1), offset)
    return c, sa, sb


def _pack_table(tab):
    e, d = tab.shape
    bits = lax.bitcast_convert_type(tab.astype(BF16), jnp.uint16).astype(jnp.uint32)
    word = bits[:, :d // 2] | (bits[:, d // 2:] << 16)
    return lax.bitcast_convert_type(word, I32).reshape(e * PEER_WORD_ROWS, LANES)


def _relayout_w_in(w_in):
    d = w_in.shape[0]
    o = 0
    seg = {}
    for name, size in (("cq", 256), ("ckv", 128), ("kpe", 32), ("qn", 512), ("kc", 128), ("vc", 128),
                       ("ks", 128), ("vs", 128), ("kw", 128), ("vw", 128), ("gate", 24)):
        seg[name] = w_in[:, o:o + size]
        o += size
    gates = [_pad_lanes(seg["gate"][:, j::3]) for j in range(3)]
    cols = [_pad_heads(seg["qn"], NSA_HEADS, NSA_DK),
            _pad_heads(seg["ks"], NSA_GROUPS, NSA_DK), _pad_heads(seg["kw"], NSA_GROUPS, NSA_DK),
            seg["cq"], seg["ckv"], _pad_lanes(seg["kpe"], MLA_NOPE),
            _pad_heads(seg["vs"], NSA_GROUPS, NSA_DK), _pad_heads(seg["vw"], NSA_GROUPS, NSA_DK),
            seg["kc"]] + gates + [seg["vc"]]
    w = jnp.concatenate(cols, axis=1)
    assert w.shape == (d, IN_COLS_PAD)
    return w.astype(BF16)


def _layer(h, mem, p):
    b, s, d = h.shape
    t = b * s
    assert s % SEL_CHUNK == 0 and N_SEL <= s // SEL_LEN <= SEL_BLOCKS_PAD and d == 1024
    pos = jnp.arange(s)

    y = _norm_matmul(h.reshape(t, d), p["mix_norm"], _relayout_w_in(p["w_in"]), ROW_TILE)
    y3 = y.reshape(b, s, IN_COLS_PAD)

    wuq = _pad_heads(p["mla_w_uq"], MLA_HEADS, MLA_QK).astype(BF16)
    wukv = p["mla_w_ukv"].reshape(MLA_KV_LORA, MLA_HEADS, MLA_NOPE + MLA_V)
    wuk = _pad_heads(wukv[..., :MLA_NOPE].reshape(MLA_KV_LORA, -1), MLA_HEADS, MLA_NOPE).astype(BF16)
    wuv = _pad_heads(wukv[..., MLA_NOPE:].reshape(MLA_KV_LORA, -1), MLA_HEADS, MLA_V).astype(BF16)
    q, k, v = _mla_prep(y3, p["mla_q_norm"][None], p["mla_kv_norm"][None], _pad_lanes(p["mla_qk_norm"]),
                        wuq, wuk, wuv, _rope_tabs(pos, MLA_ROPE, MLA_NOPE), ROW_TILE)
    o_mla = _mla_attn(q, k, v, MLA_TQ)

    tabs = _rope_tabs(pos, NSA_ROT, 0)
    qn, ks, kw, vs, vw = _nsa_prep(y3, _pad_lanes(p["nsa_q_norm"][None]), _pad_lanes(p["nsa_k_norm"][1:3]),
                                   tabs, ROW_TILE)
    nrow = s // CMP_STRIDE

    def blocks16(col):
        tkv = y3[:, :, col:col + NSA_GROUPS * NSA_DK].reshape(b, nrow, CMP_STRIDE, NSA_GROUPS, NSA_DK)
        return tkv.transpose(0, 3, 1, 2, 4).reshape(b, NSA_GROUPS, nrow, CMP_STRIDE * NSA_DK)

    pe = p["nsa_cmp_pe"].reshape(2, 2, CMP_STRIDE * NSA_DK)
    w1 = p["nsa_cmp_w1"].reshape(2, 2, CMP_STRIDE * NSA_DK, -1).astype(BF16)
    w2 = _pad_lanes(p["nsa_cmp_w2"]).astype(BF16)
    cmp_end = jnp.arange(nrow) * CMP_STRIDE + CMP_LEN - 1
    kc, vc = _compress(blocks16(COL_KC), blocks16(COL_VC), pe, w1, w2, _pad_lanes(p["nsa_k_norm"][0:1]),
                       _rope_tabs(cmp_end, NSA_ROT, 0))
    o_nsa = _nsa_attn(qn, kc, vc, ks, vs, kw, vw, y3, NSA_TQ)

    width = MLA_HEADS * MLA_V
    g_out = p["mix_out_norm"]
    gm = _pad_heads(g_out[None, :width], MLA_HEADS, MLA_V)
    gn = _pad_heads(g_out[None, width:], NSA_HEADS, NSA_DK)
    w_out = p["w_out"].reshape(2 * MLA_HEADS, MLA_V, d)
    w_out = jnp.pad(w_out, ((0, 0), (0, LANES - MLA_V), (0, 0))).reshape(2 * MLA_HEADS * LANES, d).astype(BF16)
    h = _mix_out(h, o_mla, o_nsa, gm, gn, w_out, MIX_TILE)

    kx, vx = _mem_kv(mem, p["mem_norm"][None], p["xa_wkv"].astype(BF16), p["xa_qk_norm"][1:2])
    h = _xattn(h, p["xa_norm"][None], p["xa_wq"].astype(BF16), p["xa_qk_norm"][0:1], kx, vx,
               p["xa_wo"].astype(BF16), MIX_TILE)

    keys = p["peer_keys"]
    nhp = PEER_HEADS * 2
    eye = jnp.eye(nhp, dtype=F32)
    kbd = (keys.reshape(nhp, PEER_KEYS, PEER_HALF)[:, :, None, :] * eye[:, None, :, None])
    kbd = kbd.reshape(nhp * PEER_KEYS, nhp * PEER_HALF).astype(BF16)
    h2d = h.reshape(t, d)
    hn, st = _peer_score(h2d, p["ffn_norm"][None], p["peer_wq"].astype(BF16), kbd, ROW_TILE)
    w, eidx = _peer_route_up(st, hn, _pack_table(p["peer_u"]))
    out = _peer_down(eidx, w, h2d, _pack_table(p["peer_v"]))
    return out.reshape(b, s, d)


def kernel(x, mem, mix_norm, w_in, mla_q_norm, mla_kv_norm, mla_w_uq, mla_w_ukv, mla_qk_norm, nsa_q_norm, nsa_k_norm, nsa_cmp_pe, nsa_cmp_w1, nsa_cmp_w2, mix_out_norm, w_out, xa_norm, mem_norm, xa_wq, xa_wkv, xa_qk_norm, xa_wo, ffn_norm, peer_wq, peer_keys, peer_u, peer_v):
    params = dict(mix_norm=mix_norm, w_in=w_in, mla_q_norm=mla_q_norm, mla_kv_norm=mla_kv_norm,
                  mla_w_uq=mla_w_uq, mla_w_ukv=mla_w_ukv, mla_qk_norm=mla_qk_norm, nsa_q_norm=nsa_q_norm,
                  nsa_k_norm=nsa_k_norm, nsa_cmp_pe=nsa_cmp_pe, nsa_cmp_w1=nsa_cmp_w1, nsa_cmp_w2=nsa_cmp_w2,
                  mix_out_norm=mix_out_norm, w_out=w_out, xa_norm=xa_norm, mem_norm=mem_norm, xa_wq=xa_wq,
                  xa_wkv=xa_wkv, xa_qk_norm=xa_qk_norm, xa_wo=xa_wo, ffn_norm=ffn_norm, peer_wq=peer_wq,
                  peer_keys=peer_keys, peer_u=peer_u, peer_v=peer_v)
    h = x
    for layer in range(w_in.shape[0]):
        h = _layer(h, mem, {name: val[layer] for name, val in params.items()})
    return h
```

```python
import functools

import jax
import jax.numpy as jnp
from jax import lax
from jax.experimental import pallas as pl
from jax.experimental.pallas import tpu as pltpu

F32 = jnp.float32
BF16 = jnp.bfloat16
I32 = jnp.int32

LANES = 128
VMEM_LIMIT = 56 * 1024 * 1024

ROPE_THETA = 500000.0
EPS = 1e-6
NEG = -1e30
FORCE = 1e4

MLA_HEADS = 8
MLA_Q_LORA = 256
MLA_KV_LORA = 128
MLA_NOPE = 64
MLA_ROPE = 32
MLA_V = 64
MLA_QK = MLA_NOPE + MLA_ROPE

NSA_HEADS = 8
NSA_GROUPS = 2
NSA_REP = NSA_HEADS // NSA_GROUPS
NSA_DK = 64
NSA_ROT = NSA_DK // 4
CMP_LEN = 32
CMP_STRIDE = 16
SEL_LEN = 64
SEL_SHIFT = SEL_LEN.bit_length() - 1
SEL_BLOCKS_PAD = 64
N_SEL = 16
WIN = 512
SEL_CHUNK = 1024
NSA_TQ = 256
MLA_TQ = 512
MLA_HEADS_PER_STEP = 4
ROW_TILE = 512
MIX_TILE = 256
LOG2E = 1.4426950408889634

XA_HEADS = 4
XA_DH = 128

PEER_HEADS = 8
PEER_KEYS = 128
PEER_HALF = 64
PEER_TOPK = 16
PEER_PICKS = PEER_HEADS * PEER_TOPK
PEER_TOK = 128
PEER_WORD_ROWS = 4
PEER_UP_TRIP = 64
PEER_DOWN_TRIP = 64

COL_QN = 0
COL_KS = 1024
COL_KW = 1280
COL_CQ = 1536
COL_CKV = 1792
COL_KPE = 1920
COL_VS = 2048
COL_VW = 2304
COL_KC = 2560
COL_GATE = 2688
COL_VC = 3072
IN_COLS_PAD = 3200


def _cparams(*sem):
    return pltpu.CompilerParams(dimension_semantics=sem, vmem_limit_bytes=VMEM_LIMIT)


def _rms(x, g, n):
    ms = jnp.sum(x * x, axis=-1, keepdims=True) * (1.0 / n)
    return x * lax.rsqrt(ms + EPS) * g


def _rope(x, c, sa, sb, half):
    return x * c + pltpu.roll(x, LANES - half, 1) * sa + pltpu.roll(x, half, 1) * sb


def _gelu(x):
    return 0.5 * x * (1.0 + jnp.tanh(0.7978845608028654 * (x + 0.044715 * (x * x * x))))


def _dot_nt(a, b, **kw):
    return lax.dot_general(a, b, (((1,), (1,)), ((), ())), preferred_element_type=F32, **kw)


def _norm_matmul_kernel(x_ref, g_ref, w_ref, o_ref):
    x = x_ref[...]
    n = _rms(x, g_ref[...], x.shape[-1])
    o_ref[...] = jnp.dot(n.astype(BF16), w_ref[...], preferred_element_type=F32)


def _norm_matmul(x2d, g, w, tm):
    t, d = x2d.shape
    n = w.shape[1]
    return pl.pallas_call(
        _norm_matmul_kernel,
        grid=(t // tm,),
        in_specs=[pl.BlockSpec((tm, d), lambda i: (i, 0)),
                  pl.BlockSpec((1, d), lambda i: (0, 0)),
                  pl.BlockSpec((d, n), lambda i: (0, 0))],
        out_specs=pl.BlockSpec((tm, n), lambda i: (i, 0)),
        out_shape=jax.ShapeDtypeStruct((t, n), F32),
        compiler_params=_cparams("parallel"),
    )(x2d, g.reshape(1, d), w)


def _mla_prep_kernel(y_ref, gq_ref, gkv_ref, gqk_ref, wuq_ref, wuk_ref, wuv_ref,
                     c_ref, sa_ref, sb_ref, q_ref, k_ref, v_ref):
    y = y_ref[0]
    cq = y[:, 0:MLA_Q_LORA]
    ckv = y[:, MLA_Q_LORA:MLA_Q_LORA + MLA_KV_LORA]
    kpe = y[:, MLA_Q_LORA + MLA_KV_LORA:]
    nq = _rms(cq, gq_ref[...], MLA_Q_LORA).astype(BF16)
    nkv = _rms(ckv, gkv_ref[...], MLA_KV_LORA).astype(BF16)
    qf = jnp.dot(nq, wuq_ref[...], preferred_element_type=F32)
    kf = jnp.dot(nkv, wuk_ref[...], preferred_element_type=F32)
    vf = jnp.dot(nkv, wuv_ref[...], preferred_element_type=F32)
    c, sa, sb = c_ref[...], sa_ref[...], sb_ref[...]
    gq = gqk_ref[0:1, :]
    gk = gqk_ref[1:2, :]
    scale = MLA_QK ** -0.5 * LOG2E
    for h in range(MLA_HEADS):
        sl = slice(h * LANES, (h + 1) * LANES)
        qh = _rope(_rms(qf[:, sl], gq, MLA_QK), c, sa, sb, MLA_ROPE // 2)
        kh = _rope(_rms(kf[:, sl] + kpe, gk, MLA_QK), c, sa, sb, MLA_ROPE // 2)
        q_ref[0, h] = (qh * scale).astype(BF16)
        k_ref[0, h] = kh.astype(BF16)
        v_ref[0, h] = vf[:, sl].astype(BF16)


def _mla_prep(y3, gq, gkv, gqk, wuq, wuk, wuv, tabs, tm):
    b, s, _ = y3.shape
    hs = jax.ShapeDtypeStruct((b, MLA_HEADS, s, LANES), BF16)
    full = lambda shape: pl.BlockSpec(shape, lambda bi, i: (0,) * len(shape))
    tab = pl.BlockSpec((tm, LANES), lambda bi, i: (i, 0))
    hspec = pl.BlockSpec((1, MLA_HEADS, tm, LANES), lambda bi, i: (bi, 0, i, 0))
    return pl.pallas_call(
        _mla_prep_kernel,
        grid=(b, s // tm),
        in_specs=[pl.BlockSpec((1, tm, 512), lambda bi, i: (bi, i, COL_CQ // 512)),
                  full((1, MLA_Q_LORA)), full((1, MLA_KV_LORA)), full((2, LANES)),
                  full((MLA_Q_LORA, MLA_HEADS * LANES)), full((MLA_KV_LORA, MLA_HEADS * LANES)),
                  full((MLA_KV_LORA, MLA_HEADS * LANES)), tab, tab, tab],
        out_specs=[hspec, hspec, hspec],
        out_shape=[hs, hs, hs],
        compiler_params=_cparams("parallel", "parallel"),
    )(y3, gq, gkv, gqk, wuq, wuk, wuv, *tabs)


def _nsa_prep_kernel(yq_ref, yv_ref, gq_ref, gk_ref, c_ref, sa_ref, sb_ref,
                     q_ref, ks_ref, kw_ref, vs_ref, vw_ref):
    c, sa, sb = c_ref[...], sa_ref[...], sb_ref[...]
    half = NSA_ROT // 2
    scale = NSA_DK ** -0.5 * LOG2E
    yq = yq_ref[0]
    for h in range(NSA_HEADS):
        qh = _rms(yq[:, h * LANES:(h + 1) * LANES], gq_ref[...], NSA_DK)
        q_ref[0, h] = (_rope(qh, c, sa, sb, half) * scale).astype(BF16)
    for g in range(NSA_GROUPS):
        ks = yq[:, COL_KS + g * LANES:COL_KS + (g + 1) * LANES]
        kw = yq[:, COL_KW + g * LANES:COL_KW + (g + 1) * LANES]
        ks_ref[0, g] = _rope(_rms(ks, gk_ref[0:1, :], NSA_DK), c, sa, sb, half).astype(BF16)
        kw_ref[0, g] = _rope(_rms(kw, gk_ref[1:2, :], NSA_DK), c, sa, sb, half).astype(BF16)
    yv = yv_ref[0]
    for g in range(NSA_GROUPS):
        vs_ref[0, g] = yv[:, g * LANES:(g + 1) * LANES].astype(BF16)
        vw_ref[0, g] = yv[:, (NSA_GROUPS + g) * LANES:(NSA_GROUPS + g + 1) * LANES].astype(BF16)


def _nsa_prep(y3, gq, gk, tabs, tm):
    b, s, _ = y3.shape
    full = lambda shape: pl.BlockSpec(shape, lambda bi, i: (0,) * len(shape))
    tab = pl.BlockSpec((tm, LANES), lambda bi, i: (i, 0))
    qs = jax.ShapeDtypeStruct((b, NSA_HEADS, s, LANES), BF16)
    gs = jax.ShapeDtypeStruct((b, NSA_GROUPS, s, LANES), BF16)
    gspec = pl.BlockSpec((1, NSA_GROUPS, tm, LANES), lambda bi, i: (bi, 0, i, 0))
    return pl.pallas_call(
        _nsa_prep_kernel,
        grid=(b, s // tm),
        in_specs=[pl.BlockSpec((1, tm, 1536), lambda bi, i: (bi, i, 0)),
                  pl.BlockSpec((1, tm, 512), lambda bi, i: (bi, i, COL_VS // 512)),
                  full((1, LANES)), full((2, LANES)), tab, tab, tab],
        out_specs=[pl.BlockSpec((1, NSA_HEADS, tm, LANES), lambda bi, i: (bi, 0, i, 0)),
                   gspec, gspec, gspec, gspec],
        out_shape=[qs, gs, gs, gs, gs],
        compiler_params=_cparams("parallel", "parallel"),
    )(y3, y3, gq, gk, *tabs)


def _compress_kernel(tk_ref, tv_ref, pe_ref, w1_ref, w2_ref, gk_ref, c_ref, sa_ref, sb_ref,
                     kc_ref, vc_ref):
    nrow = tk_ref.shape[2]

    def mlp(t, kv):
        ta = (t + pe_ref[kv, 0:1, :]).astype(BF16)
        tb = (t + pe_ref[kv, 1:2, :]).astype(BF16)
        za = jnp.dot(ta, w1_ref[kv, 0], preferred_element_type=F32)
        zb = jnp.dot(tb, w1_ref[kv, 1], preferred_element_type=F32)
        hid = _gelu(za + pltpu.roll(zb, nrow - 1, 0))
        return jnp.dot(hid.astype(BF16), w2_ref[kv], preferred_element_type=F32)

    kc = mlp(tk_ref[0, 0], 0)
    kc = _rope(_rms(kc, gk_ref[...], NSA_DK), c_ref[...], sa_ref[...], sb_ref[...], NSA_ROT // 2)
    kc_ref[0, 0] = kc.astype(BF16)
    vc_ref[0, 0] = mlp(tv_ref[0, 0], 1).astype(BF16)


def _compress(tk16, tv16, pe, w1, w2, gk, tabs):
    b, g, nrow, width = tk16.shape
    full = lambda shape: pl.BlockSpec(shape, lambda bi, gi: (0,) * len(shape))
    tspec = pl.BlockSpec((1, 1, nrow, width), lambda bi, gi: (bi, gi, 0, 0))
    ospec = pl.BlockSpec((1, 1, nrow, LANES), lambda bi, gi: (bi, gi, 0, 0))
    os_ = jax.ShapeDtypeStruct((b, g, nrow, LANES), BF16)
    return pl.pallas_call(
        _compress_kernel,
        grid=(b, g),
        in_specs=[tspec, tspec, full(pe.shape), full(w1.shape), full(w2.shape), full((1, LANES)),
                  full((nrow, LANES)), full((nrow, LANES)), full((nrow, LANES))],
        out_specs=[ospec, ospec],
        out_shape=[os_, os_],
        compiler_params=_cparams("parallel", "parallel"),
    )(tk16, tv16, pe, w1, w2, gk, *tabs)


def _softmax_init(heads, tq):
    return (jnp.full((heads, tq, 1), NEG, F32), jnp.zeros((heads, tq, 1), F32),
            jnp.zeros((heads * tq, LANES), F32))


def _softmax_step(carry, q, k_ref, v_ref, start, width, mask):
    m, l, acc = carry
    heads, tq, _ = m.shape
    k = k_ref[pl.ds(start, width), :]
    v = v_ref[pl.ds(start, width), :]
    s = _dot_nt(q, k).reshape(heads, tq, width)
    if mask is not None:
        s = jnp.where(mask[None], s, NEG)
    m_new = jnp.maximum(m, jnp.max(s, axis=-1, keepdims=True))
    p = jnp.exp2(s - m_new)
    alpha = jnp.exp2(m - m_new)
    l = alpha * l + jnp.sum(p, axis=-1, keepdims=True)
    pv = jnp.dot(p.reshape(heads * tq, width).astype(BF16), v, preferred_element_type=F32)
    return m_new, l, alpha.reshape(heads * tq, 1) * acc + pv


def _softmax_finish(carry):
    _, l, acc = carry
    return acc * (1.0 / l).reshape(acc.shape[0], 1)


def _mla_attn_kernel(q_ref, k_ref, v_ref, o_ref, *, tq):
    q0 = pl.program_id(2) * tq
    heads = range(q_ref.shape[1])
    wide = 2 * tq
    nwide = q0 // wide

    def step(carry, start, width, mask):
        return tuple(_softmax_step(carry[h], q_ref[0, h], k_ref.at[0, h], v_ref.at[0, h], start, width, mask)
                     for h in heads)

    carry = lax.fori_loop(0, nwide, lambda j, c: step(c, pl.multiple_of(j * wide, wide), wide, None),
                          tuple(_softmax_init(1, tq) for _ in heads))
    carry = lax.fori_loop(0, (q0 - nwide * wide) // tq,
                          lambda j, c: step(c, pl.multiple_of(nwide * wide, tq), tq, None), carry)
    causal = lax.broadcasted_iota(I32, (tq, tq), 1) <= lax.broadcasted_iota(I32, (tq, tq), 0)
    carry = step(carry, pl.multiple_of(q0, tq), tq, causal)
    for h in heads:
        o_ref[0, :, h * LANES:(h + 1) * LANES] = _softmax_finish(carry[h])


def _mla_attn(q, k, v, tq):
    b, h, s, _ = q.shape
    hg = MLA_HEADS_PER_STEP
    kv = pl.BlockSpec((1, hg, s, LANES), lambda bi, hi, i: (bi, hi, 0, 0))
    return pl.pallas_call(
        functools.partial(_mla_attn_kernel, tq=tq),
        grid=(b, h // hg, s // tq),
        in_specs=[pl.BlockSpec((1, hg, tq, LANES), lambda bi, hi, i: (bi, hi, i, 0)), kv, kv],
        out_specs=pl.BlockSpec((1, tq, hg * LANES), lambda bi, hi, i: (bi, i, hi)),
        out_shape=jax.ShapeDtypeStruct((b, s, h * LANES), F32),
        compiler_params=_cparams("parallel", "parallel", "arbitrary"),
    )(q, k, v)


def _compressed_branch(q, kc, vc, q0, tq):
    ncmp = kc.shape[0]
    s = _dot_nt(q, kc).reshape(NSA_REP, tq, ncmp)
    qpos = q0 + lax.broadcasted_iota(I32, (tq, ncmp), 0)
    n = lax.broadcasted_iota(I32, (tq, ncmp), 1)
    msk = (n * CMP_STRIDE + (CMP_LEN - 1) <= qpos)[None]
    s = jnp.where(msk, s, NEG)
    m = jnp.max(s, axis=-1, keepdims=True)
    p = jnp.where(msk, jnp.exp2(s - m), 0.0)
    l = jnp.sum(p, axis=-1, keepdims=True)
    p = p * jnp.where(l > 0.0, 1.0 / l, 0.0)
    o = jnp.dot(p.reshape(NSA_REP * tq, ncmp).astype(BF16), vc, preferred_element_type=F32)

    nblk = SEL_BLOCKS_PAD
    psum = p[0] + p[1] + p[2] + p[3]
    bm = lax.broadcasted_iota(I32, (nblk, ncmp), 0)
    bn = lax.broadcasted_iota(I32, (nblk, ncmp), 1)
    overlap = ((bn * CMP_STRIDE < bm * SEL_LEN + SEL_LEN)
               & (bn * CMP_STRIDE + CMP_LEN > bm * SEL_LEN)).astype(F32)
    imp = _dot_nt(overlap, psum, precision=lax.Precision.HIGHEST)
    blk = lax.broadcasted_iota(I32, (nblk, tq), 0)
    cur = (q0 + lax.broadcasted_iota(I32, (nblk, tq), 1)) >> SEL_SHIFT
    forced = (blk == 0) | (blk == cur) | (blk == cur - 1)
    score = jnp.where(blk <= cur, jnp.where(forced, FORCE, imp), NEG)
    rank = jnp.zeros((nblk, tq), I32)
    for j in range(nblk):
        row = score[j:j + 1, :]
        rank = rank + jnp.where(row > score, 1, jnp.where(row == score, (blk > j).astype(I32), 0))
    sel = jnp.where(rank < N_SEL, (blk <= cur).astype(F32), 0.0)
    sel = jnp.concatenate([sel, jnp.zeros((LANES - nblk, tq), F32)], axis=0)
    return o, sel.T


def _selected_branch(q, kr, vr, sel, q0, tq, tk):
    per_chunk = tk // SEL_LEN
    eb = lax.broadcasted_iota(I32, (LANES, tk), 0)
    ec = lax.broadcasted_iota(I32, (LANES, tk), 1)
    expand = ((ec >> SEL_SHIFT) == eb).astype(BF16)

    def chosen(j):
        local = pltpu.roll(sel, (LANES - j * per_chunk) % LANES, 1).astype(BF16)
        return jnp.dot(local, expand, preferred_element_type=F32) > 0.5

    nfull = q0 // tk
    carry = lax.fori_loop(
        0, nfull, lambda j, c: _softmax_step(c, q, kr, vr, pl.multiple_of(j * tk, tk), tk, chosen(j)),
        _softmax_init(NSA_REP, tq))
    start = pl.multiple_of(nfull * tk, tk)
    causal = start + lax.broadcasted_iota(I32, (tq, tk), 1) <= q0 + lax.broadcasted_iota(I32, (tq, tk), 0)
    carry = _softmax_step(carry, q, kr, vr, start, tk, chosen(nfull) & causal)
    return _softmax_finish(carry)


def _window_branch(q, kr, vr, q0, tq, tk):
    start = pl.multiple_of(jnp.maximum(q0 + tq - tk, 0), tq)
    dist = (q0 + lax.broadcasted_iota(I32, (tq, tk), 0)) - (start + lax.broadcasted_iota(I32, (tq, tk), 1))
    carry = _softmax_step(_softmax_init(NSA_REP, tq), q, kr, vr, start, tk, (dist >= 0) & (dist < WIN))
    return _softmax_finish(carry)


def _nsa_attn_kernel(q_ref, kc_ref, vc_ref, ks_ref, vs_ref, kw_ref, vw_ref, gate_ref, o_ref, *, tq):
    group = pl.program_id(1)
    q0 = pl.program_id(2) * tq
    q = q_ref[0].reshape(NSA_REP * tq, LANES)
    o_win = _window_branch(q, kw_ref.at[0, 0], vw_ref.at[0, 0], q0, tq, WIN + tq)
    o_cmp, sel = _compressed_branch(q, kc_ref[0, 0], vc_ref[0, 0], q0, tq)
    o_sel = _selected_branch(q, ks_ref.at[0, 0], vs_ref.at[0, 0], sel, q0, tq, SEL_CHUNK)
    gate = jax.nn.sigmoid(gate_ref[0])
    shift = (LANES - NSA_REP * group) % LANES
    gates = [pltpu.roll(gate[:, j * LANES:(j + 1) * LANES], shift, 1) for j in range(3)]
    for r in range(NSA_REP):
        rows = slice(r * tq, (r + 1) * tq)
        o_ref[0, :, r * LANES:(r + 1) * LANES] = (gates[0][:, r:r + 1] * o_cmp[rows]
                                                  + gates[1][:, r:r + 1] * o_sel[rows]
                                                  + gates[2][:, r:r + 1] * o_win[rows])


def _nsa_attn(q, kc, vc, ks, vs, kw, vw, gates, tq):
    b, _, s, _ = q.shape
    ncmp = kc.shape[2]
    assert s % SEL_CHUNK == 0 and SEL_CHUNK % tq == 0 and s >= WIN + tq
    kv = pl.BlockSpec((1, 1, s, LANES), lambda bi, gi, i: (bi, gi, 0, 0))
    cspec = pl.BlockSpec((1, 1, ncmp, LANES), lambda bi, gi, i: (bi, gi, 0, 0))
    return pl.pallas_call(
        functools.partial(_nsa_attn_kernel, tq=tq),
        grid=(b, NSA_GROUPS, s // tq),
        in_specs=[pl.BlockSpec((1, NSA_REP, tq, LANES), lambda bi, gi, i: (bi, gi, i, 0)),
                  cspec, cspec, kv, kv, kv, kv,
                  pl.BlockSpec((1, tq, 3 * LANES), lambda bi, gi, i: (bi, i, COL_GATE // (3 * LANES)))],
        out_specs=pl.BlockSpec((1, tq, NSA_REP * LANES), lambda bi, gi, i: (bi, i, gi)),
        out_shape=jax.ShapeDtypeStruct((b, s, NSA_HEADS * LANES), F32),
        compiler_params=_cparams("parallel", "parallel", "arbitrary"),
    )(q, kc, vc, ks, vs, kw, vw, gates)


def _mix_out_kernel(x_ref, om_ref, on_ref, gm_ref, gn_ref, w_ref, o_ref):
    width = MLA_HEADS * MLA_V
    mixed = jnp.concatenate([_rms(om_ref[0], gm_ref[...], width), _rms(on_ref[0], gn_ref[...], width)], axis=-1)
    o_ref[0] = x_ref[0] + jnp.dot(mixed.astype(BF16), w_ref[...], preferred_element_type=F32)


def _mix_out(x, om, on, gm, gn, w, tm):
    b, s, d = x.shape
    wide = om.shape[-1]
    full = lambda shape: pl.BlockSpec(shape, lambda bi, i: (0,) * len(shape))
    row = lambda width: pl.BlockSpec((1, tm, width), lambda bi, i: (bi, i, 0))
    return pl.pallas_call(
        _mix_out_kernel,
        grid=(b, s // tm),
        in_specs=[row(d), row(wide), row(wide), full((1, wide)), full((1, wide)), full(w.shape)],
        out_specs=row(d),
        out_shape=jax.ShapeDtypeStruct((b, s, d), F32),
        compiler_params=_cparams("parallel", "parallel"),
    )(x, om, on, gm, gn, w)


def _mem_kv_kernel(m_ref, g_ref, w_ref, gk_ref, k_ref, v_ref):
    mem = m_ref[0]
    n = _rms(mem, g_ref[...], mem.shape[-1]).astype(BF16)
    kv = jnp.dot(n, w_ref[...], preferred_element_type=F32)
    width = XA_HEADS * XA_DH
    for h in range(XA_HEADS):
        sl = slice(h * XA_DH, (h + 1) * XA_DH)
        k_ref[0, :, sl] = _rms(kv[:, sl], gk_ref[...], XA_DH).astype(BF16)
    v_ref[0] = kv[:, width:].astype(BF16)


def _mem_kv(mem, g, w, gk):
    b, m, d = mem.shape
    width = XA_HEADS * XA_DH
    full = lambda shape: pl.BlockSpec(shape, lambda bi: (0,) * len(shape))
    ospec = pl.BlockSpec((1, m, width), lambda bi: (bi, 0, 0))
    os_ = jax.ShapeDtypeStruct((b, m, width), BF16)
    return pl.pallas_call(
        _mem_kv_kernel,
        grid=(b,),
        in_specs=[pl.BlockSpec((1, m, d), lambda bi: (bi, 0, 0)), full((1, d)), full(w.shape), full((1, XA_DH))],
        out_specs=[ospec, ospec],
        out_shape=[os_, os_],
        compiler_params=_cparams("parallel"),
    )(mem, g, w, gk)


def _xattn_kernel(h_ref, g_ref, wq_ref, gq_ref, k_ref, v_ref, wo_ref, o_ref):
    h = h_ref[0]
    hn = _rms(h, g_ref[...], h.shape[-1]).astype(BF16)
    q = jnp.dot(hn, wq_ref[...], preferred_element_type=F32)
    outs = []
    for hd in range(XA_HEADS):
        sl = slice(hd * XA_DH, (hd + 1) * XA_DH)
        qh = (_rms(q[:, sl], gq_ref[...], XA_DH) * (XA_DH ** -0.5)).astype(BF16)
        s = _dot_nt(qh, k_ref[0, :, sl])
        p = jnp.exp(s - jnp.max(s, axis=-1, keepdims=True))
        p = p * (1.0 / jnp.sum(p, axis=-1, keepdims=True))
        outs.append(jnp.dot(p.astype(BF16), v_ref[0, :, sl], preferred_element_type=F32))
    o = jnp.concatenate(outs, axis=-1).astype(BF16)
    o_ref[0] = h + jnp.dot(o, wo_ref[...], preferred_element_type=F32)


def _xattn(h, g, wq, gq, k, v, wo, tm):
    b, s, d = h.shape
    m, width = k.shape[1], k.shape[2]
    full = lambda shape: pl.BlockSpec(shape, lambda bi, i: (0,) * len(shape))
    kv = pl.BlockSpec((1, m, width), lambda bi, i: (bi, 0, 0))
    row = pl.BlockSpec((1, tm, d), lambda bi, i: (bi, i, 0))
    return pl.pallas_call(
        _xattn_kernel,
        grid=(b, s // tm),
        in_specs=[row, full((1, d)), full(wq.shape), full((1, XA_DH)), kv, kv, full(wo.shape)],
        out_specs=row,
        out_shape=jax.ShapeDtypeStruct((b, s, d), F32),
        compiler_params=_cparams("parallel", "parallel"),
    )(h, g, wq, gq, k, v, wo)


def _peer_score_kernel(h_ref, g_ref, wq_ref, kbd_ref, hn_ref, s_ref):
    h = h_ref[...]
    hn = _rms(h, g_ref[...], h.shape[-1]).astype(BF16)
    hn_ref[...] = hn
    q = jnp.dot(hn, wq_ref[...], preferred_element_type=F32).astype(BF16)
    s_ref[...] = _dot_nt(kbd_ref[...], q)


def _peer_score(h2d, g, wq, kbd, tm):
    t, d = h2d.shape
    nk = kbd.shape[0]
    full = lambda shape: pl.BlockSpec(shape, lambda i: (0,) * len(shape))
    return pl.pallas_call(
        _peer_score_kernel,
        grid=(t // tm,),
        in_specs=[pl.BlockSpec((tm, d), lambda i: (i, 0)), full((1, d)), full(wq.shape), full(kbd.shape)],
        out_specs=[pl.BlockSpec((tm, d), lambda i: (i, 0)), pl.BlockSpec((nk, tm), lambda i: (0, i))],
        out_shape=[jax.ShapeDtypeStruct((t, d), BF16), jax.ShapeDtypeStruct((nk, t), F32)],
        compiler_params=_cparams("parallel"),
    )(h2d, g, wq, kbd)


def _topk_rows(x, k, ids, payload=None):
    vals, outs = [], []
    for _ in range(k):
        m = jnp.max(x, axis=0, keepdims=True)
        idx = jnp.min(jnp.where(x == m, ids, jnp.iinfo(jnp.int32).max), axis=0, keepdims=True)
        hit = ids == idx
        vals.append(m)
        if payload is None:
            outs.append(idx)
        else:
            outs.append(jnp.max(jnp.where(hit, payload, -1), axis=0, keepdims=True))
        x = jnp.where(hit, -jnp.inf, x)
    return jnp.concatenate(vals, axis=0), jnp.concatenate(outs, axis=0)


PEER_EDGE = 4
assert (PEER_EDGE + 1) ** 2 > PEER_TOPK


def _route_head(s_ref, h, e_scr, g_scr):
    tm = s_ref.shape[1]
    key_ids = lax.broadcasted_iota(I32, (PEER_KEYS, tm), 0)
    a_ids = lax.broadcasted_iota(I32, (PEER_TOPK, tm), 0)
    base = pl.multiple_of(h * (2 * PEER_KEYS), 2 * PEER_KEYS)
    s1, i1 = _topk_rows(s_ref[pl.ds(base, PEER_KEYS), :], PEER_TOPK, key_ids)
    s2, i2 = _topk_rows(s_ref[pl.ds(base + PEER_KEYS, PEER_KEYS), :], PEER_TOPK, key_ids)
    i1 = i1 * PEER_KEYS
    cand, flat, cidx = [], [], []
    for a in range(PEER_EDGE):
        cand.append(s1[a:a + 1, :] + s2)
        flat.append(a * PEER_TOPK + a_ids)
        cidx.append(i1[a:a + 1, :] + i2)
    for b in range(PEER_EDGE):
        cand.append(jnp.where(a_ids >= PEER_EDGE, s1 + s2[b:b + 1, :], -jnp.inf))
        flat.append(a_ids * PEER_TOPK + b)
        cidx.append(i1 + i2[b:b + 1, :])
    top, eidx = _topk_rows(jnp.concatenate(cand, axis=0), PEER_TOPK, jnp.concatenate(flat, axis=0),
                           payload=jnp.concatenate(cidx, axis=0))
    p = jnp.exp(top - top[0:1, :])
    p = p * (1.0 / jnp.sum(p, axis=0, keepdims=True))
    row = pl.multiple_of(h * PEER_TOPK, PEER_TOPK)
    e_scr[pl.ds(row, PEER_TOPK), :] = eidx * PEER_WORD_ROWS
    g_scr[pl.ds(row, PEER_TOPK), :] = p


def _gather_rows(idx_ref, tab_ref, tile_ref, j):
    for k in range(PEER_PICKS):
        r = pl.multiple_of(idx_ref[j, k], PEER_WORD_ROWS)
        tile_ref[pl.ds(k * PEER_WORD_ROWS, PEER_WORD_ROWS), :] = tab_ref[pl.ds(r, PEER_WORD_ROWS), :]


def _tile_rows(tile_ref):
    chunks = [pltpu.bitcast(tile_ref[pl.ds(c, PEER_PICKS, stride=PEER_WORD_ROWS), :], BF16)
              for c in range(PEER_WORD_ROWS)]
    return jnp.concatenate(chunks, axis=1)


def _token_loop(idx_vm, idx_sm, sem, tab_ref, tile_a, tile_b, compute, per_trip=None):
    per = idx_sm.shape[1]
    ntrip = PEER_TOK // per

    def fetch(t, slot):
        rows = pl.ds(pl.multiple_of(t * per, per), per)
        return pltpu.make_async_copy(idx_vm.at[rows], idx_sm.at[slot], sem.at[slot])

    fetch(0, 0).start()
    fetch(0, 0).wait()
    _gather_rows(idx_sm.at[0], tab_ref, tile_a, 0)

    def group(i, carry):
        for slot in range(2):
            t = 2 * i + slot
            ahead = jnp.minimum(t + 1, ntrip - 1)
            fetch(ahead, 1 - slot).start()
            if per_trip is not None:
                per_trip(t)
            for u in range(per):
                cur, nxt = (tile_a, tile_b) if u % 2 == 0 else (tile_b, tile_a)
                rows = _tile_rows(cur)
                if u + 1 < per:
                    _gather_rows(idx_sm.at[slot], tab_ref, nxt, u + 1)
                else:
                    fetch(ahead, 1 - slot).wait()
                    _gather_rows(idx_sm.at[1 - slot], tab_ref, nxt, 0)
                compute(t * per + u, rows)
        return carry

    assert per % 2 == 0 and ntrip % 2 == 0
    lax.fori_loop(0, ntrip // 2, group, 0)


def _idx_scratch(per_trip_tokens):
    return [pltpu.SMEM((2, per_trip_tokens, PEER_PICKS), I32), pltpu.SemaphoreType.DMA((2,))]


def _pair_matrix():
    r = lax.broadcasted_iota(I32, (2 * PEER_PICKS, PEER_PICKS), 0)
    c = lax.broadcasted_iota(I32, (2 * PEER_PICKS, PEER_PICKS), 1)
    return (r >> 1) == c


def _peer_route_up_kernel(s_ref, x_ref, tab_ref, w_ref, e_ref, ebuf_ref, gbuf_ref, e_scr, g_scr,
                          tile_a, tile_b, act_ref, xf_ref, idx_sm, sem):
    c = pl.program_id(0)
    cur, nxt = (c + 1) % 2, c % 2
    half = x_ref.shape[1] // 2
    sub = lax.broadcasted_iota(I32, (16, half), 0)
    odd_lane = (lax.broadcasted_iota(I32, (1, 2 * PEER_PICKS), 1) & 1) == 1

    @pl.when(c == 0)
    def _():
        ebuf_ref[1] = jnp.zeros(ebuf_ref.shape[1:], I32)
        gbuf_ref[1] = jnp.zeros(gbuf_ref.shape[1:], F32)

    xf_ref[...] = x_ref[...].astype(F32)

    def compute(j, rows):
        xr = xf_ref[pl.ds(j, 1), :]
        lo = jnp.broadcast_to(xr[:, :half], (16, half))
        hi = jnp.broadcast_to(xr[:, half:], (16, half))
        xsel = jnp.where(sub == 0, lo, jnp.where(sub == 1, hi, 0.0)).astype(BF16)
        r = _dot_nt(xsel, rows)
        act_ref[pl.ds(j, 1), :] = jnp.where(odd_lane, r[1:2, :], r[0:1, :])

    heads_per_trip = PEER_HEADS * PEER_UP_TRIP // PEER_TOK

    def route(t):
        for i in range(heads_per_trip):
            _route_head(s_ref, t * heads_per_trip + i, e_scr, g_scr)

    _token_loop(ebuf_ref.at[cur], idx_sm, sem, tab_ref, tile_a, tile_b, compute, per_trip=route)
    pair = _pair_matrix()
    act = jnp.dot(act_ref[...], pair.astype(F32), preferred_element_type=F32,
                  precision=lax.Precision.HIGHEST)
    w = (gbuf_ref[cur] * _gelu(act)).astype(BF16)
    w_ref[...] = _dot_nt(w, pair.astype(BF16))

    routed = e_scr[...].T
    e_ref[...] = routed
    ebuf_ref[nxt] = routed
    gbuf_ref[nxt] = g_scr[...].T


def _peer_down_kernel(idx_ref, w_ref, h_ref, tab_ref, o_ref, tile_a, tile_b, idx_sm, sem):
    half = h_ref.shape[1] // 2
    shape = (16, 2 * PEER_PICKS)
    parity = (lax.broadcasted_iota(I32, shape, 1) & 1) == lax.broadcasted_iota(I32, shape, 0)

    def compute(j, rows):
        wrow = jnp.broadcast_to(w_ref[pl.ds(j, 1), :], shape)
        wm = jnp.where(parity, wrow, 0.0).astype(BF16)
        o2 = jnp.dot(wm, rows, preferred_element_type=F32)
        o_ref[pl.ds(j, 1), 0:half] = h_ref[pl.ds(j, 1), 0:half] + o2[0:1, :]
        o_ref[pl.ds(j, 1), half:] = h_ref[pl.ds(j, 1), half:] + o2[1:2, :]

    _token_loop(idx_ref, idx_sm, sem, tab_ref, tile_a, tile_b, compute)


def _tile_scratch():
    return pltpu.VMEM((PEER_WORD_ROWS * PEER_PICKS, LANES), I32)


def _table_spec(tab):
    return pl.BlockSpec(tab.shape, lambda c: (0, 0), pipeline_mode=pl.Buffered(1))


def _peer_route_up(st, hn, tab):
    t, d = hn.shape
    nk = st.shape[0]
    nblk = t // PEER_TOK
    prev = lambda c: (jnp.maximum(c - 1, 0), 0)
    return pl.pallas_call(
        _peer_route_up_kernel,
        grid=(nblk + 1,),
        in_specs=[pl.BlockSpec((nk, PEER_TOK), lambda c: (0, jnp.minimum(c, nblk - 1))),
                  pl.BlockSpec((PEER_TOK, d), prev),
                  _table_spec(tab)],
        out_specs=[pl.BlockSpec((PEER_TOK, 2 * PEER_PICKS), prev),
                   pl.BlockSpec((PEER_TOK, PEER_PICKS), lambda c: (jnp.minimum(c, nblk - 1), 0))],
        out_shape=[jax.ShapeDtypeStruct((t, 2 * PEER_PICKS), F32), jax.ShapeDtypeStruct((t, PEER_PICKS), I32)],
        scratch_shapes=[pltpu.VMEM((2, PEER_TOK, PEER_PICKS), I32), pltpu.VMEM((2, PEER_TOK, PEER_PICKS), F32),
                        pltpu.VMEM((PEER_PICKS, PEER_TOK), I32), pltpu.VMEM((PEER_PICKS, PEER_TOK), F32),
                        _tile_scratch(), _tile_scratch(), pltpu.VMEM((PEER_TOK, 2 * PEER_PICKS), F32),
                        pltpu.VMEM((PEER_TOK, d), F32)] + _idx_scratch(PEER_UP_TRIP),
        compiler_params=_cparams("arbitrary"),
    )(st, hn, tab)


def _peer_down(eidx, w, h2d, tab):
    t, d = h2d.shape
    return pl.pallas_call(
        _peer_down_kernel,
        grid=(t // PEER_TOK,),
        in_specs=[pl.BlockSpec((PEER_TOK, PEER_PICKS), lambda c: (c, 0)),
                  pl.BlockSpec((PEER_TOK, 2 * PEER_PICKS), lambda c: (c, 0)),
                  pl.BlockSpec((PEER_TOK, d), lambda c: (c, 0)),
                  _table_spec(tab)],
        out_specs=pl.BlockSpec((PEER_TOK, d), lambda c: (c, 0)),
        out_shape=jax.ShapeDtypeStruct((t, d), F32),
        scratch_shapes=[_tile_scratch(), _tile_scratch()] + _idx_scratch(PEER_DOWN_TRIP),
        compiler_params=_cparams("arbitrary"),
    )(eidx, w, h2d, tab)


def _pad_heads(w, heads, dim, dim_pad=LANES):
    lead = w.shape[:-1]
    w = w.reshape(lead + (heads, dim))
    w = jnp.pad(w, [(0, 0)] * len(lead) + [(0, 0), (0, dim_pad - dim)])
    return w.reshape(lead + (heads * dim_pad,))


def _pad_lanes(v, offset=0, width=LANES):
    return jnp.pad(v, [(0, 0)] * (v.ndim - 1) + [(offset, width - offset - v.shape[-1])])


def _rope_tabs(pos, rot_dim, offset):
    inv = 1.0 / (ROPE_THETA ** (jnp.arange(0, rot_dim, 2, dtype=F32) / rot_dim))
    ang = pos.astype(F32)[:, None] * inv[None, :]
    cos, sin = jnp.cos(ang), jnp.sin(ang)
    zero = jnp.zeros_like(sin)
    c = jnp.pad(jnp.concatenate([cos, cos], -1) - 1.0, ((0, 0), (offset, LANES - offset - rot_dim))) + 1.0
    sa = _pad_lanes(jnp.concatenate([-sin, zero], -1), offset)
    sb = _pad_lanes(jnp.concatenate([zero, sin], -1), offset)
    return c, sa, sb


def _pack_table(tab):
    e, d = tab.shape
    bits = lax.bitcast_convert_type(tab.astype(BF16), jnp.uint16).astype(jnp.uint32)
    word = bits[:, :d // 2] | (bits[:, d // 2:] << 16)
    return lax.bitcast_convert_type(word, I32).reshape(e * PEER_WORD_ROWS, LANES)


def _relayout_w_in(w_in):
    d = w_in.shape[0]
    o = 0
    seg = {}
    for name, size in (("cq", 256), ("ckv", 128), ("kpe", 32), ("qn", 512), ("kc", 128), ("vc", 128),
                       ("ks", 128), ("vs", 128), ("kw", 128), ("vw", 128), ("gate", 24)):
        seg[name] = w_in[:, o:o + size]
        o += size
    gates = [_pad_lanes(seg["gate"][:, j::3]) for j in range(3)]
    cols = [_pad_heads(seg["qn"], NSA_HEADS, NSA_DK),
            _pad_heads(seg["ks"], NSA_GROUPS, NSA_DK), _pad_heads(seg["kw"], NSA_GROUPS, NSA_DK),
            seg["cq"], seg["ckv"], _pad_lanes(seg["kpe"], MLA_NOPE),
            _pad_heads(seg["vs"], NSA_GROUPS, NSA_DK), _pad_heads(seg["vw"], NSA_GROUPS, NSA_DK),
            seg["kc"]] + gates + [seg["vc"]]
    w = jnp.concatenate(cols, axis=1)
    assert w.shape == (d, IN_COLS_PAD)
    return w.astype(BF16)


def _layer(h, mem, p):
    b, s, d = h.shape
    t = b * s
    assert s % SEL_CHUNK == 0 and N_SEL <= s // SEL_LEN <= SEL_BLOCKS_PAD and d == 1024
    pos = jnp.arange(s)

    y = _norm_matmul(h.reshape(t, d), p["mix_norm"], _relayout_w_in(p["w_in"]), ROW_TILE)
    y3 = y.reshape(b, s, IN_COLS_PAD)

    wuq = _pad_heads(p["mla_w_uq"], MLA_HEADS, MLA_QK).astype(BF16)
    wukv = p["mla_w_ukv"].reshape(MLA_KV_LORA, MLA_HEADS, MLA_NOPE + MLA_V)
    wuk = _pad_heads(wukv[..., :MLA_NOPE].reshape(MLA_KV_LORA, -1), MLA_HEADS, MLA_NOPE).astype(BF16)
    wuv = _pad_heads(wukv[..., MLA_NOPE:].reshape(MLA_KV_LORA, -1), MLA_HEADS, MLA_V).astype(BF16)
    q, k, v = _mla_prep(y3, p["mla_q_norm"][None], p["mla_kv_norm"][None], _pad_lanes(p["mla_qk_norm"]),
                        wuq, wuk, wuv, _rope_tabs(pos, MLA_ROPE, MLA_NOPE), ROW_TILE)
    o_mla = _mla_attn(q, k, v, MLA_TQ)

    tabs = _rope_tabs(pos, NSA_ROT, 0)
    qn, ks, kw, vs, vw = _nsa_prep(y3, _pad_lanes(p["nsa_q_norm"][None]), _pad_lanes(p["nsa_k_norm"][1:3]),
                                   tabs, ROW_TILE)
    nrow = s // CMP_STRIDE

    def blocks16(col):
        tkv = y3[:, :, col:col + NSA_GROUPS * NSA_DK].reshape(b, nrow, CMP_STRIDE, NSA_GROUPS, NSA_DK)
        return tkv.transpose(0, 3, 1, 2, 4).reshape(b, NSA_GROUPS, nrow, CMP_STRIDE * NSA_DK)

    pe = p["nsa_cmp_pe"].reshape(2, 2, CMP_STRIDE * NSA_DK)
    w1 = p["nsa_cmp_w1"].reshape(2, 2, CMP_STRIDE * NSA_DK, -1).astype(BF16)
    w2 = _pad_lanes(p["nsa_cmp_w2"]).astype(BF16)
    cmp_end = jnp.arange(nrow) * CMP_STRIDE + CMP_LEN - 1
    kc, vc = _compress(blocks16(COL_KC), blocks16(COL_VC), pe, w1, w2, _pad_lanes(p["nsa_k_norm"][0:1]),
                       _rope_tabs(cmp_end, NSA_ROT, 0))
    o_nsa = _nsa_attn(qn, kc, vc, ks, vs, kw, vw, y3, NSA_TQ)

    width = MLA_HEADS * MLA_V
    g_out = p["mix_out_norm"]
    gm = _pad_heads(g_out[None, :width], MLA_HEADS, MLA_V)
    gn = _pad_heads(g_out[None, width:], NSA_HEADS, NSA_DK)
    w_out = p["w_out"].reshape(2 * MLA_HEADS, MLA_V, d)
    w_out = jnp.pad(w_out, ((0, 0), (0, LANES - MLA_V), (0, 0))).reshape(2 * MLA_HEADS * LANES, d).astype(BF16)
    h = _mix_out(h, o_mla, o_nsa, gm, gn, w_out, MIX_TILE)

    kx, vx = _mem_kv(mem, p["mem_norm"][None], p["xa_wkv"].astype(BF16), p["xa_qk_norm"][1:2])
    h = _xattn(h, p["xa_norm"][None], p["xa_wq"].astype(BF16), p["xa_qk_norm"][0:1], kx, vx,
               p["xa_wo"].astype(BF16), MIX_TILE)

    keys = p["peer_keys"]
    nhp = PEER_HEADS * 2
    eye = jnp.eye(nhp, dtype=F32)
    kbd = (keys.reshape(nhp, PEER_KEYS, PEER_HALF)[:, :, None, :] * eye[:, None, :, None])
    kbd = kbd.reshape(nhp * PEER_KEYS, nhp * PEER_HALF).astype(BF16)
    h2d = h.reshape(t, d)
    hn, st = _peer_score(h2d, p["ffn_norm"][None], p["peer_wq"].astype(BF16), kbd, ROW_TILE)
    w, eidx = _peer_route_up(st, hn, _pack_table(p["peer_u"]))
    out = _peer_down(eidx, w, h2d, _pack_table(p["peer_v"]))
    return out.reshape(b, s, d)


def kernel(x, mem, mix_norm, w_in, mla_q_norm, mla_kv_norm, mla_w_uq, mla_w_ukv, mla_qk_norm, nsa_q_norm, nsa_k_norm, nsa_cmp_pe, nsa_cmp_w1, nsa_cmp_w2, mix_out_norm, w_out, xa_norm, mem_norm, xa_wq, xa_wkv, xa_qk_norm, xa_wo, ffn_norm, peer_wq, peer_keys, peer_u, peer_v):
    params = dict(mix_norm=mix_norm, w_in=w_in, mla_q_norm=mla_q_norm, mla_kv_norm=mla_kv_norm,
                  mla_w_uq=mla_w_uq, mla_w_ukv=mla_w_ukv, mla_qk_norm=mla_qk_norm, nsa_q_norm=nsa_q_norm,
                  nsa_k_norm=nsa_k_norm, nsa_cmp_pe=nsa_cmp_pe, nsa_cmp_w1=nsa_cmp_w1, nsa_cmp_w2=nsa_cmp_w2,
                  mix_out_norm=mix_out_norm, w_out=w_out, xa_norm=xa_norm, mem_norm=mem_norm, xa_wq=xa_wq,
                  xa_wkv=xa_wkv, xa_qk_norm=xa_qk_norm, xa_wo=xa_wo, ffn_norm=ffn_norm, peer_wq=peer_wq,
                  peer_keys=peer_keys, peer_u=peer_u, peer_v=peer_v)
    h = x
    for layer in range(w_in.shape[0]):
        h = _layer(h, mem, {name: val[layer] for name, val in params.items()})
    return h
```

```python
import functools

import jax
import jax.numpy as jnp
from jax import lax
from jax.experimental import pallas as pl
from jax.experimental.pallas import tpu as pltpu

F32 = jnp.float32
BF16 = jnp.bfloat16
I32 = jnp.int32

LANES = 128
VMEM_LIMIT = 56 * 1024 * 1024

ROPE_THETA = 500000.0
EPS = 1e-6
NEG = -1e30
FORCE = 1e4

MLA_HEADS = 8
MLA_Q_LORA = 256
MLA_KV_LORA = 128
MLA_NOPE = 64
MLA_ROPE = 32
MLA_V = 64
MLA_QK = MLA_NOPE + MLA_ROPE

NSA_HEADS = 8
NSA_GROUPS = 2
NSA_REP = NSA_HEADS // NSA_GROUPS
NSA_DK = 64
NSA_ROT = NSA_DK // 4
CMP_LEN = 32
CMP_STRIDE = 16
SEL_LEN = 64
SEL_SHIFT = SEL_LEN.bit_length() - 1
SEL_BLOCKS_PAD = 64
N_SEL = 16
WIN = 512
SEL_CHUNK = 1024
NSA_TQ = 256
MLA_TQ = 512
MLA_HEADS_PER_STEP = 4
ROW_TILE = 512
MIX_TILE = 256
LOG2E = 1.4426950408889634

XA_HEADS = 4
XA_DH = 128

PEER_HEADS = 8
PEER_KEYS = 128
PEER_HALF = 64
PEER_TOPK = 16
PEER_PICKS = PEER_HEADS * PEER_TOPK
PEER_TOK = 128
PEER_WORD_ROWS = 4
PEER_UP_TRIP = 64
PEER_DOWN_TRIP = 64

COL_QN = 0
COL_KS = 1024
COL_KW = 1280
COL_CQ = 1536
COL_CKV = 1792
COL_KPE = 1920
COL_VS = 2048
COL_VW = 2304
COL_KC = 2560
COL_GATE = 2688
COL_VC = 3072
IN_COLS_PAD = 3200


def _cparams(*sem):
    return pltpu.CompilerParams(dimension_semantics=sem, vmem_limit_bytes=VMEM_LIMIT)


def _rms(x, g, n):
    ms = jnp.sum(x * x, axis=-1, keepdims=True) * (1.0 / n)
    return x * lax.rsqrt(ms + EPS) * g


def _rope(x, c, sa, sb, half):
    return x * c + pltpu.roll(x, LANES - half, 1) * sa + pltpu.roll(x, half, 1) * sb


def _gelu(x):
    return 0.5 * x * (1.0 + jnp.tanh(0.7978845608028654 * (x + 0.044715 * (x * x * x))))


def _dot_nt(a, b, **kw):
    return lax.dot_general(a, b, (((1,), (1,)), ((), ())), preferred_element_type=F32, **kw)


def _norm_matmul_kernel(x_ref, g_ref, w_ref, o_ref):
    x = x_ref[...]
    n = _rms(x, g_ref[...], x.shape[-1])
    o_ref[...] = jnp.dot(n.astype(BF16), w_ref[...], preferred_element_type=F32)


def _norm_matmul(x2d, g, w, tm):
    t, d = x2d.shape
    n = w.shape[1]
    return pl.pallas_call(
        _norm_matmul_kernel,
        grid=(t // tm,),
        in_specs=[pl.BlockSpec((tm, d), lambda i: (i, 0)),
                  pl.BlockSpec((1, d), lambda i: (0, 0)),
                  pl.BlockSpec((d, n), lambda i: (0, 0))],
        out_specs=pl.BlockSpec((tm, n), lambda i: (i, 0)),
        out_shape=jax.ShapeDtypeStruct((t, n), F32),
        compiler_params=_cparams("parallel"),
    )(x2d, g.reshape(1, d), w)


def _mla_prep_kernel(y_ref, gq_ref, gkv_ref, gqk_ref, wuq_ref, wuk_ref, wuv_ref,
                     c_ref, sa_ref, sb_ref, q_ref, k_ref, v_ref):
    y = y_ref[0]
    cq = y[:, 0:MLA_Q_LORA]
    ckv = y[:, MLA_Q_LORA:MLA_Q_LORA + MLA_KV_LORA]
    kpe = y[:, MLA_Q_LORA + MLA_KV_LORA:]
    nq = _rms(cq, gq_ref[...], MLA_Q_LORA).astype(BF16)
    nkv = _rms(ckv, gkv_ref[...], MLA_KV_LORA).astype(BF16)
    qf = jnp.dot(nq, wuq_ref[...], preferred_element_type=F32)
    kf = jnp.dot(nkv, wuk_ref[...], preferred_element_type=F32)
    vf = jnp.dot(nkv, wuv_ref[...], preferred_element_type=F32)
    c, sa, sb = c_ref[...], sa_ref[...], sb_ref[...]
    gq = gqk_ref[0:1, :]
    gk = gqk_ref[1:2, :]
    scale = MLA_QK ** -0.5 * LOG2E
    for h in range(MLA_HEADS):
        sl = slice(h * LANES, (h + 1) * LANES)
        qh = _rope(_rms(qf[:, sl], gq, MLA_QK), c, sa, sb, MLA_ROPE // 2)
        kh = _rope(_rms(kf[:, sl] + kpe, gk, MLA_QK), c, sa, sb, MLA_ROPE // 2)
        q_ref[0, h] = (qh * scale).astype(BF16)
        k_ref[0, h] = kh.astype(BF16)
        v_ref[0, h] = vf[:, sl].astype(BF16)


def _mla_prep(y3, gq, gkv, gqk, wuq, wuk, wuv, tabs, tm):
    b, s, _ = y3.shape
    hs = jax.ShapeDtypeStruct((b, MLA_HEADS, s, LANES), BF16)
    full = lambda shape: pl.BlockSpec(shape, lambda bi, i: (0,) * len(shape))
    tab = pl.BlockSpec((tm, LANES), lambda bi, i: (i, 0))
    hspec = pl.BlockSpec((1, MLA_HEADS, tm, LANES), lambda bi, i: (bi, 0, i, 0))
    return pl.pallas_call(
        _mla_prep_kernel,
        grid=(b, s // tm),
        in_specs=[pl.BlockSpec((1, tm, 512), lambda bi, i: (bi, i, COL_CQ // 512)),
                  full((1, MLA_Q_LORA)), full((1, MLA_KV_LORA)), full((2, LANES)),
                  full((MLA_Q_LORA, MLA_HEADS * LANES)), full((MLA_KV_LORA, MLA_HEADS * LANES)),
                  full((MLA_KV_LORA, MLA_HEADS * LANES)), tab, tab, tab],
        out_specs=[hspec, hspec, hspec],
        out_shape=[hs, hs, hs],
        compiler_params=_cparams("parallel", "parallel"),
    )(y3, gq, gkv, gqk, wuq, wuk, wuv, *tabs)


def _nsa_prep_kernel(yq_ref, yv_ref, gq_ref, gk_ref, c_ref, sa_ref, sb_ref,
                     q_ref, ks_ref, kw_ref, vs_ref, vw_ref):
    c, sa, sb = c_ref[...], sa_ref[...], sb_ref[...]
    half = NSA_ROT // 2
    scale = NSA_DK ** -0.5 * LOG2E
    yq = yq_ref[0]
    for h in range(NSA_HEADS):
        qh = _rms(yq[:, h * LANES:(h + 1) * LANES], gq_ref[...], NSA_DK)
        q_ref[0, h] = (_rope(qh, c, sa, sb, half) * scale).astype(BF16)
    for g in range(NSA_GROUPS):
        ks = yq[:, COL_KS + g * LANES:COL_KS + (g + 1) * LANES]
        kw = yq[:, COL_KW + g * LANES:COL_KW + (g + 1) * LANES]
        ks_ref[0, g] = _rope(_rms(ks, gk_ref[0:1, :], NSA_DK), c, sa, sb, half).astype(BF16)
        kw_ref[0, g] = _rope(_rms(kw, gk_ref[1:2, :], NSA_DK), c, sa, sb, half).astype(BF16)
    yv = yv_ref[0]
    for g in range(NSA_GROUPS):
        vs_ref[0, g] = yv[:, g * LANES:(g + 1) * LANES].astype(BF16)
        vw_ref[0, g] = yv[:, (NSA_GROUPS + g) * LANES:(NSA_GROUPS + g + 1) * LANES].astype(BF16)


def _nsa_prep(y3, gq, gk, tabs, tm):
    b, s, _ = y3.shape
    full = lambda shape: pl.BlockSpec(shape, lambda bi, i: (0,) * len(shape))
    tab = pl.BlockSpec((tm, LANES), lambda bi, i: (i, 0))
    qs = jax.ShapeDtypeStruct((b, NSA_HEADS, s, LANES), BF16)
    gs = jax.ShapeDtypeStruct((b, NSA_GROUPS, s, LANES), BF16)
    gspec = pl.BlockSpec((1, NSA_GROUPS, tm, LANES), lambda bi, i: (bi, 0, i, 0))
    return pl.pallas_call(
        _nsa_prep_kernel,
        grid=(b, s // tm),
        in_specs=[pl.BlockSpec((1, tm, 1536), lambda bi, i: (bi, i, 0)),
                  pl.BlockSpec((1, tm, 512), lambda bi, i: (bi, i, COL_VS // 512)),
                  full((1, LANES)), full((2, LANES)), tab, tab, tab],
        out_specs=[pl.BlockSpec((1, NSA_HEADS, tm, LANES), lambda bi, i: (bi, 0, i, 0)),
                   gspec, gspec, gspec, gspec],
        out_shape=[qs, gs, gs, gs, gs],
        compiler_params=_cparams("parallel", "parallel"),
    )(y3, y3, gq, gk, *tabs)


def _compress_kernel(tk_ref, tv_ref, pe_ref, w1_ref, w2_ref, gk_ref, c_ref, sa_ref, sb_ref,
                     kc_ref, vc_ref):
    nrow = tk_ref.shape[2]

    def mlp(t, kv):
        ta = (t + pe_ref[kv, 0:1, :]).astype(BF16)
        tb = (t + pe_ref[kv, 1:2, :]).astype(BF16)
        za = jnp.dot(ta, w1_ref[kv, 0], preferred_element_type=F32)
        zb = jnp.dot(tb, w1_ref[kv, 1], preferred_element_type=F32)
        hid = _gelu(za + pltpu.roll(zb, nrow - 1, 0))
        return jnp.dot(hid.astype(BF16), w2_ref[kv], preferred_element_type=F32)

    kc = mlp(tk_ref[0, 0], 0)
    kc = _rope(_rms(kc, gk_ref[...], NSA_DK), c_ref[...], sa_ref[...], sb_ref[...], NSA_ROT // 2)
    kc_ref[0, 0] = kc.astype(BF16)
    vc_ref[0, 0] = mlp(tv_ref[0, 0], 1).astype(BF16)


def _compress(tk16, tv16, pe, w1, w2, gk, tabs):
    b, g, nrow, width = tk16.shape
    full = lambda shape: pl.BlockSpec(shape, lambda bi, gi: (0,) * len(shape))
    tspec = pl.BlockSpec((1, 1, nrow, width), lambda bi, gi: (bi, gi, 0, 0))
    ospec = pl.BlockSpec((1, 1, nrow, LANES), lambda bi, gi: (bi, gi, 0, 0))
    os_ = jax.ShapeDtypeStruct((b, g, nrow, LANES), BF16)
    return pl.pallas_call(
        _compress_kernel,
        grid=(b, g),
        in_specs=[tspec, tspec, full(pe.shape), full(w1.shape), full(w2.shape), full((1, LANES)),
                  full((nrow, LANES)), full((nrow, LANES)), full((nrow, LANES))],
        out_specs=[ospec, ospec],
        out_shape=[os_, os_],
        compiler_params=_cparams("parallel", "parallel"),
    )(tk16, tv16, pe, w1, w2, gk, *tabs)


def _softmax_init(heads, tq):
    return (jnp.full((heads, tq, 1), NEG, F32), jnp.zeros((heads, tq, 1), F32),
            jnp.zeros((heads * tq, LANES), F32))


def _softmax_step(carry, q, k_ref, v_ref, start, width, mask):
    m, l, acc = carry
    heads, tq, _ = m.shape
    k = k_ref[pl.ds(start, width), :]
    v = v_ref[pl.ds(start, width), :]
    s = _dot_nt(q, k).reshape(heads, tq, width)
    if mask is not None:
        s = jnp.where(mask[None], s, NEG)
    m_new = jnp.maximum(m, jnp.max(s, axis=-1, keepdims=True))
    p = jnp.exp2(s - m_new)
    alpha = jnp.exp2(m - m_new)
    l = alpha * l + jnp.sum(p, axis=-1, keepdims=True)
    pv = jnp.dot(p.reshape(heads * tq, width).astype(BF16), v, preferred_element_type=F32)
    return m_new, l, alpha.reshape(heads * tq, 1) * acc + pv


def _softmax_finish(carry):
    _, l, acc = carry
    return acc * (1.0 / l).reshape(acc.shape[0], 1)


def _mla_attn_kernel(q_ref, k_ref, v_ref, o_ref, *, tq):
    q0 = pl.program_id(2) * tq
    heads = range(q_ref.shape[1])
    wide = 2 * tq
    nwide = q0 // wide

    def step(carry, start, width, mask):
        return tuple(_softmax_step(carry[h], q_ref[0, h], k_ref.at[0, h], v_ref.at[0, h], start, width, mask)
                     for h in heads)

    carry = lax.fori_loop(0, nwide, lambda j, c: step(c, pl.multiple_of(j * wide, wide), wide, None),
                          tuple(_softmax_init(1, tq) for _ in heads))
    carry = lax.fori_loop(0, (q0 - nwide * wide) // tq,
                          lambda j, c: step(c, pl.multiple_of(nwide * wide, tq), tq, None), carry)
    causal = lax.broadcasted_iota(I32, (tq, tq), 1) <= lax.broadcasted_iota(I32, (tq, tq), 0)
    carry = step(carry, pl.multiple_of(q0, tq), tq, causal)
    for h in heads:
        o_ref[0, :, h * LANES:(h + 1) * LANES] = _softmax_finish(carry[h])


def _mla_attn(q, k, v, tq):
    b, h, s, _ = q.shape
    hg = MLA_HEADS_PER_STEP
    kv = pl.BlockSpec((1, hg, s, LANES), lambda bi, hi, i: (bi, hi, 0, 0))
    return pl.pallas_call(
        functools.partial(_mla_attn_kernel, tq=tq),
        grid=(b, h // hg, s // tq),
        in_specs=[pl.BlockSpec((1, hg, tq, LANES), lambda bi, hi, i: (bi, hi, i, 0)), kv, kv],
        out_specs=pl.BlockSpec((1, tq, hg * LANES), lambda bi, hi, i: (bi, i, hi)),
        out_shape=jax.ShapeDtypeStruct((b, s, h * LANES), F32),
        compiler_params=_cparams("parallel", "parallel", "arbitrary"),
    )(q, k, v)


def _compressed_branch(q, kc, vc, q0, tq):
    ncmp = kc.shape[0]
    s = _dot_nt(q, kc).reshape(NSA_REP, tq, ncmp)
    qpos = q0 + lax.broadcasted_iota(I32, (tq, ncmp), 0)
    n = lax.broadcasted_iota(I32, (tq, ncmp), 1)
    msk = (n * CMP_STRIDE + (CMP_LEN - 1) <= qpos)[None]
    s = jnp.where(msk, s, NEG)
    m = jnp.max(s, axis=-1, keepdims=True)
    p = jnp.where(msk, jnp.exp2(s - m), 0.0)
    l = jnp.sum(p, axis=-1, keepdims=True)
    p = p * jnp.where(l > 0.0, 1.0 / l, 0.0)
    o = jnp.dot(p.reshape(NSA_REP * tq, ncmp).astype(BF16), vc, preferred_element_type=F32)

    nblk = SEL_BLOCKS_PAD
    psum = p[0] + p[1] + p[2] + p[3]
    bm = lax.broadcasted_iota(I32, (nblk, ncmp), 0)
    bn = lax.broadcasted_iota(I32, (nblk, ncmp), 1)
    overlap = ((bn * CMP_STRIDE < bm * SEL_LEN + SEL_LEN)
               & (bn * CMP_STRIDE + CMP_LEN > bm * SEL_LEN)).astype(F32)
    imp = _dot_nt(overlap, psum, precision=lax.Precision.HIGHEST)
    blk = lax.broadcasted_iota(I32, (nblk, tq), 0)
    cur = (q0 + lax.broadcasted_iota(I32, (nblk, tq), 1)) >> SEL_SHIFT
    forced = (blk == 0) | (blk == cur) | (blk == cur - 1)
    score = jnp.where(blk <= cur, jnp.where(forced, FORCE, imp), NEG)
    rank = jnp.zeros((nblk, tq), I32)
    for j in range(nblk):
        row = score[j:j + 1, :]
        rank = rank + jnp.where(row > score, 1, jnp.where(row == score, (blk > j).astype(I32), 0))
    sel = jnp.where(rank < N_SEL, (blk <= cur).astype(F32), 0.0)
    sel = jnp.concatenate([sel, jnp.zeros((LANES - nblk, tq), F32)], axis=0)
    return o, sel.T


def _selected_branch(q, kr, vr, sel, q0, tq, tk):
    per_chunk = tk // SEL_LEN
    eb = lax.broadcasted_iota(I32, (LANES, tk), 0)
    ec = lax.broadcasted_iota(I32, (LANES, tk), 1)
    expand = ((ec >> SEL_SHIFT) == eb).astype(BF16)

    def chosen(j):
        local = pltpu.roll(sel, (LANES - j * per_chunk) % LANES, 1).astype(BF16)
        return jnp.dot(local, expand, preferred_element_type=F32) > 0.5

    nfull = q0 // tk
    carry = lax.fori_loop(
        0, nfull, lambda j, c: _softmax_step(c, q, kr, vr, pl.multiple_of(j * tk, tk), tk, chosen(j)),
        _softmax_init(NSA_REP, tq))
    start = pl.multiple_of(nfull * tk, tk)
    causal = start + lax.broadcasted_iota(I32, (tq, tk), 1) <= q0 + lax.broadcasted_iota(I32, (tq, tk), 0)
    carry = _softmax_step(carry, q, kr, vr, start, tk, chosen(nfull) & causal)
    return _softmax_finish(carry)


def _window_branch(q, kr, vr, q0, tq, tk):
    start = pl.multiple_of(jnp.maximum(q0 + tq - tk, 0), tq)
    dist = (q0 + lax.broadcasted_iota(I32, (tq, tk), 0)) - (start + lax.broadcasted_iota(I32, (tq, tk), 1))
    carry = _softmax_step(_softmax_init(NSA_REP, tq), q, kr, vr, start, tk, (dist >= 0) & (dist < WIN))
    return _softmax_finish(carry)


def _nsa_attn_kernel(q_ref, kc_ref, vc_ref, ks_ref, vs_ref, kw_ref, vw_ref, gate_ref, o_ref, *, tq):
    group = pl.program_id(1)
    q0 = pl.program_id(2) * tq
    q = q_ref[0].reshape(NSA_REP * tq, LANES)
    o_win = _window_branch(q, kw_ref.at[0, 0], vw_ref.at[0, 0], q0, tq, WIN + tq)
    o_cmp, sel = _compressed_branch(q, kc_ref[0, 0], vc_ref[0, 0], q0, tq)
    o_sel = _selected_branch(q, ks_ref.at[0, 0], vs_ref.at[0, 0], sel, q0, tq, SEL_CHUNK)
    gate = jax.nn.sigmoid(gate_ref[0])
    shift = (LANES - NSA_REP * group) % LANES
    gates = [pltpu.roll(gate[:, j * LANES:(j + 1) * LANES], shift, 1) for j in range(3)]
    for r in range(NSA_REP):
        rows = slice(r * tq, (r + 1) * tq)
        o_ref[0, :, r * LANES:(r + 1) * LANES] = (gates[0][:, r:r + 1] * o_cmp[rows]
                                                  + gates[1][:, r:r + 1] * o_sel[rows]
                                                  + gates[2][:, r:r + 1] * o_win[rows])


def _nsa_attn(q, kc, vc, ks, vs, kw, vw, gates, tq):
    b, _, s, _ = q.shape
    ncmp = kc.shape[2]
    assert s % SEL_CHUNK == 0 and SEL_CHUNK % tq == 0 and s >= WIN + tq
    kv = pl.BlockSpec((1, 1, s, LANES), lambda bi, gi, i: (bi, gi, 0, 0))
    cspec = pl.BlockSpec((1, 1, ncmp, LANES), lambda bi, gi, i: (bi, gi, 0, 0))
    return pl.pallas_call(
        functools.partial(_nsa_attn_kernel, tq=tq),
        grid=(b, NSA_GROUPS, s // tq),
        in_specs=[pl.BlockSpec((1, NSA_REP, tq, LANES), lambda bi, gi, i: (bi, gi, i, 0)),
                  cspec, cspec, kv, kv, kv, kv,
                  pl.BlockSpec((1, tq, 3 * LANES), lambda bi, gi, i: (bi, i, COL_GATE // (3 * LANES)))],
        out_specs=pl.BlockSpec((1, tq, NSA_REP * LANES), lambda bi, gi, i: (bi, i, gi)),
        out_shape=jax.ShapeDtypeStruct((b, s, NSA_HEADS * LANES), F32),
        compiler_params=_cparams("parallel", "parallel", "arbitrary"),
    )(q, kc, vc, ks, vs, kw, vw, gates)


def _mix_out_kernel(x_ref, om_ref, on_ref, gm_ref, gn_ref, w_ref, o_ref):
    width = MLA_HEADS * MLA_V
    mixed = jnp.concatenate([_rms(om_ref[0], gm_ref[...], width), _rms(on_ref[0], gn_ref[...], width)], axis=-1)
    o_ref[0] = x_ref[0] + jnp.dot(mixed.astype(BF16), w_ref[...], preferred_element_type=F32)


def _mix_out(x, om, on, gm, gn, w, tm):
    b, s, d = x.shape
    wide = om.shape[-1]
    full = lambda shape: pl.BlockSpec(shape, lambda bi, i: (0,) * len(shape))
    row = lambda width: pl.BlockSpec((1, tm, width), lambda bi, i: (bi, i, 0))
    return pl.pallas_call(
        _mix_out_kernel,
        grid=(b, s // tm),
        in_specs=[row(d), row(wide), row(wide), full((1, wide)), full((1, wide)), full(w.shape)],
        out_specs=row(d),
        out_shape=jax.ShapeDtypeStruct((b, s, d), F32),
        compiler_params=_cparams("parallel", "parallel"),
    )(x, om, on, gm, gn, w)


def _mem_kv_kernel(m_ref, g_ref, w_ref, gk_ref, k_ref, v_ref):
    mem = m_ref[0]
    n = _rms(mem, g_ref[...], mem.shape[-1]).astype(BF16)
    kv = jnp.dot(n, w_ref[...], preferred_element_type=F32)
    width = XA_HEADS * XA_DH
    for h in range(XA_HEADS):
        sl = slice(h * XA_DH, (h + 1) * XA_DH)
        k_ref[0, :, sl] = _rms(kv[:, sl], gk_ref[...], XA_DH).astype(BF16)
    v_ref[0] = kv[:, width:].astype(BF16)


def _mem_kv(mem, g, w, gk):
    b, m, d = mem.shape
    width = XA_HEADS * XA_DH
    full = lambda shape: pl.BlockSpec(shape, lambda bi: (0,) * len(shape))
    ospec = pl.BlockSpec((1, m, width), lambda bi: (bi, 0, 0))
    os_ = jax.ShapeDtypeStruct((b, m, width), BF16)
    return pl.pallas_call(
        _mem_kv_kernel,
        grid=(b,),
        in_specs=[pl.BlockSpec((1, m, d), lambda bi: (bi, 0, 0)), full((1, d)), full(w.shape), full((1, XA_DH))],
        out_specs=[ospec, ospec],
        out_shape=[os_, os_],
        compiler_params=_cparams("parallel"),
    )(mem, g, w, gk)


def _xattn_kernel(h_ref, g_ref, wq_ref, gq_ref, k_ref, v_ref, wo_ref, o_ref):
    h = h_ref[0]
    hn = _rms(h, g_ref[...], h.shape[-1]).astype(BF16)
    q = jnp.dot(hn, wq_ref[...], preferred_element_type=F32)
    outs = []
    for hd in range(XA_HEADS):
        sl = slice(hd * XA_DH, (hd + 1) * XA_DH)
        qh = (_rms(q[:, sl], gq_ref[...], XA_DH) * (XA_DH ** -0.5)).astype(BF16)
        s = _dot_nt(qh, k_ref[0, :, sl])
        p = jnp.exp(s - jnp.max(s, axis=-1, keepdims=True))
        p = p * (1.0 / jnp.sum(p, axis=-1, keepdims=True))
        outs.append(jnp.dot(p.astype(BF16), v_ref[0, :, sl], preferred_element_type=F32))
    o = jnp.concatenate(outs, axis=-1).astype(BF16)
    o_ref[0] = h + jnp.dot(o, wo_ref[...], preferred_element_type=F32)


def _xattn(h, g, wq, gq, k, v, wo, tm):
    b, s, d = h.shape
    m, width = k.shape[1], k.shape[2]
    full = lambda shape: pl.BlockSpec(shape, lambda bi, i: (0,) * len(shape))
    kv = pl.BlockSpec((1, m, width), lambda bi, i: (bi, 0, 0))
    row = pl.BlockSpec((1, tm, d), lambda bi, i: (bi, i, 0))
    return pl.pallas_call(
        _xattn_kernel,
        grid=(b, s // tm),
        in_specs=[row, full((1, d)), full(wq.shape), full((1, XA_DH)), kv, kv, full(wo.shape)],
        out_specs=row,
        out_shape=jax.ShapeDtypeStruct((b, s, d), F32),
        compiler_params=_cparams("parallel", "parallel"),
    )(h, g, wq, gq, k, v, wo)


def _peer_score_kernel(h_ref, g_ref, wq_ref, kbd_ref, hn_ref, s_ref):
    h = h_ref[...]
    hn = _rms(h, g_ref[...], h.shape[-1]).astype(BF16)
    hn_ref[...] = hn
    q = jnp.dot(hn, wq_ref[...], preferred_element_type=F32).astype(BF16)
    s_ref[...] = _dot_nt(kbd_ref[...], q)


def _peer_score(h2d, g, wq, kbd, tm):
    t, d = h2d.shape
    nk = kbd.shape[0]
    full = lambda shape: pl.BlockSpec(shape, lambda i: (0,) * len(shape))
    return pl.pallas_call(
        _peer_score_kernel,
        grid=(t // tm,),
        in_specs=[pl.BlockSpec((tm, d), lambda i: (i, 0)), full((1, d)), full(wq.shape), full(kbd.shape)],
        out_specs=[pl.BlockSpec((tm, d), lambda i: (i, 0)), pl.BlockSpec((nk, tm), lambda i: (0, i))],
        out_shape=[jax.ShapeDtypeStruct((t, d), BF16), jax.ShapeDtypeStruct((nk, t), F32)],
        compiler_params=_cparams("parallel"),
    )(h2d, g, wq, kbd)


def _topk_rows(x, k, ids, payload=None):
    vals, outs = [], []
    for _ in range(k):
        m = jnp.max(x, axis=0, keepdims=True)
        idx = jnp.min(jnp.where(x == m, ids, jnp.iinfo(jnp.int32).max), axis=0, keepdims=True)
        hit = ids == idx
        vals.append(m)
        if payload is None:
            outs.append(idx)
        else:
            outs.append(jnp.max(jnp.where(hit, payload, -1), axis=0, keepdims=True))
        x = jnp.where(hit, -jnp.inf, x)
    return jnp.concatenate(vals, axis=0), jnp.concatenate(outs, axis=0)


PEER_EDGE = 4
PEER_ROW_GROUPS = ((0, 16), (1, 8), (2, 8), (3, 8))
PEER_COL_GROUPS = ((0, 16), (1, 8), (2, 8))
_covered = ([(a, b) for a, n in PEER_ROW_GROUPS for b in range(n)]
            + [(a, b) for b, n in PEER_COL_GROUPS for a in range(PEER_EDGE, n)])
assert len(set(_covered)) == len(_covered) and all(a < PEER_EDGE for a, _ in PEER_ROW_GROUPS)
assert all((a, b) in _covered for a in range(PEER_TOPK) for b in range(PEER_TOPK) if (a + 1) * (b + 1) <= PEER_TOPK)


def _route_head(s_ref, h, e_scr, g_scr):
    tm = s_ref.shape[1]
    key_ids = lax.broadcasted_iota(I32, (PEER_KEYS, tm), 0)
    a_ids = lax.broadcasted_iota(I32, (PEER_TOPK, tm), 0)
    base = pl.multiple_of(h * (2 * PEER_KEYS), 2 * PEER_KEYS)
    s1, i1 = _topk_rows(s_ref[pl.ds(base, PEER_KEYS), :], PEER_TOPK, key_ids)
    s2, i2 = _topk_rows(s_ref[pl.ds(base + PEER_KEYS, PEER_KEYS), :], PEER_TOPK, key_ids)
    i1 = i1 * PEER_KEYS
    cand, flat, cidx = [], [], []
    for a, n in PEER_ROW_GROUPS:
        cand.append(s1[a:a + 1, :] + s2[:n])
        flat.append(a * PEER_TOPK + a_ids[:n])
        cidx.append(i1[a:a + 1, :] + i2[:n])
    for b, n in PEER_COL_GROUPS:
        cand.append(jnp.where(a_ids[:n] >= PEER_EDGE, s1[:n] + s2[b:b + 1, :], -jnp.inf))
        flat.append(a_ids[:n] * PEER_TOPK + b)
        cidx.append(i1[:n] + i2[b:b + 1, :])
    top, eidx = _topk_rows(jnp.concatenate(cand, axis=0), PEER_TOPK, jnp.concatenate(flat, axis=0),
                           payload=jnp.concatenate(cidx, axis=0))
    p = jnp.exp(top - top[0:1, :])
    p = p * (1.0 / jnp.sum(p, axis=0, keepdims=True))
    row = pl.multiple_of(h * PEER_TOPK, PEER_TOPK)
    e_scr[pl.ds(row, PEER_TOPK), :] = eidx * PEER_WORD_ROWS
    g_scr[pl.ds(row, PEER_TOPK), :] = p


def _gather_rows(idx_ref, tab_ref, tile_ref, j):
    for k in range(PEER_PICKS):
        r = pl.multiple_of(idx_ref[j, k], PEER_WORD_ROWS)
        tile_ref[pl.ds(k * PEER_WORD_ROWS, PEER_WORD_ROWS), :] = tab_ref[pl.ds(r, PEER_WORD_ROWS), :]


def _tile_rows(tile_ref):
    chunks = [pltpu.bitcast(tile_ref[pl.ds(c, PEER_PICKS, stride=PEER_WORD_ROWS), :], BF16)
              for c in range(PEER_WORD_ROWS)]
    return jnp.concatenate(chunks, axis=1)


def _token_loop(idx_vm, idx_sm, sem, tab_ref, tile_a, tile_b, compute, per_trip=None):
    per = idx_sm.shape[1]
    ntrip = PEER_TOK // per

    def fetch(t, slot):
        rows = pl.ds(pl.multiple_of(t * per, per), per)
        return pltpu.make_async_copy(idx_vm.at[rows], idx_sm.at[slot], sem.at[slot])

    fetch(0, 0).start()
    fetch(0, 0).wait()
    _gather_rows(idx_sm.at[0], tab_ref, tile_a, 0)

    def group(i, carry):
        for slot in range(2):
            t = 2 * i + slot
            ahead = jnp.minimum(t + 1, ntrip - 1)
            fetch(ahead, 1 - slot).start()
            if per_trip is not None:
                per_trip(t)
            for u in range(per):
                cur, nxt = (tile_a, tile_b) if u % 2 == 0 else (tile_b, tile_a)
                rows = _tile_rows(cur)
                if u + 1 < per:
                    _gather_rows(idx_sm.at[slot], tab_ref, nxt, u + 1)
                else:
                    fetch(ahead, 1 - slot).wait()
                    _gather_rows(idx_sm.at[1 - slot], tab_ref, nxt, 0)
                compute(t * per + u, rows)
        return carry

    assert per % 2 == 0 and ntrip % 2 == 0
    lax.fori_loop(0, ntrip // 2, group, 0)


def _idx_scratch(per_trip_tokens):
    return [pltpu.SMEM((2, per_trip_tokens, PEER_PICKS), I32), pltpu.SemaphoreType.DMA((2,))]


def _pair_matrix():
    r = lax.broadcasted_iota(I32, (2 * PEER_PICKS, PEER_PICKS), 0)
    c = lax.broadcasted_iota(I32, (2 * PEER_PICKS, PEER_PICKS), 1)
    return (r >> 1) == c


def _peer_route_up_kernel(s_ref, x_ref, tab_ref, w_ref, e_ref, ebuf_ref, gbuf_ref, e_scr, g_scr,
                          tile_a, tile_b, act_ref, xf_ref, idx_sm, sem):
    c = pl.program_id(0)
    cur, nxt = (c + 1) % 2, c % 2
    half = x_ref.shape[1] // 2
    sub = lax.broadcasted_iota(I32, (16, half), 0)
    odd_lane = (lax.broadcasted_iota(I32, (1, 2 * PEER_PICKS), 1) & 1) == 1

    @pl.when(c == 0)
    def _():
        ebuf_ref[1] = jnp.zeros(ebuf_ref.shape[1:], I32)
        gbuf_ref[1] = jnp.zeros(gbuf_ref.shape[1:], F32)

    xf_ref[...] = x_ref[...].astype(F32)

    def compute(j, rows):
        xr = xf_ref[pl.ds(j, 1), :]
        lo = jnp.broadcast_to(xr[:, :half], (16, half))
        hi = jnp.broadcast_to(xr[:, half:], (16, half))
        xsel = jnp.where(sub == 0, lo, jnp.where(sub == 1, hi, 0.0)).astype(BF16)
        r = _dot_nt(xsel, rows)
        act_ref[pl.ds(j, 1), :] = jnp.where(odd_lane, r[1:2, :], r[0:1, :])

    heads_per_trip = PEER_HEADS * PEER_UP_TRIP // PEER_TOK

    def route(t):
        for i in range(heads_per_trip):
            _route_head(s_ref, t * heads_per_trip + i, e_scr, g_scr)

    _token_loop(ebuf_ref.at[cur], idx_sm, sem, tab_ref, tile_a, tile_b, compute, per_trip=route)
    pair = _pair_matrix()
    act = jnp.dot(act_ref[...], pair.astype(F32), preferred_element_type=F32,
                  precision=lax.Precision.HIGHEST)
    w = (gbuf_ref[cur] * _gelu(act)).astype(BF16)
    w_ref[...] = _dot_nt(w, pair.astype(BF16))

    routed = e_scr[...].T
    e_ref[...] = routed
    ebuf_ref[nxt] = routed
    gbuf_ref[nxt] = g_scr[...].T


def _peer_down_kernel(idx_ref, w_ref, h_ref, tab_ref, o_ref, tile_a, tile_b, idx_sm, sem):
    half = h_ref.shape[1] // 2
    shape = (16, 2 * PEER_PICKS)
    parity = (lax.broadcasted_iota(I32, shape, 1) & 1) == lax.broadcasted_iota(I32, shape, 0)

    def compute(j, rows):
        wrow = jnp.broadcast_to(w_ref[pl.ds(j, 1), :], shape)
        wm = jnp.where(parity, wrow, 0.0).astype(BF16)
        o2 = jnp.dot(wm, rows, preferred_element_type=F32)
        o_ref[pl.ds(j, 1), 0:half] = h_ref[pl.ds(j, 1), 0:half] + o2[0:1, :]
        o_ref[pl.ds(j, 1), half:] = h_ref[pl.ds(j, 1), half:] + o2[1:2, :]

    _token_loop(idx_ref, idx_sm, sem, tab_ref, tile_a, tile_b, compute)


def _tile_scratch():
    return pltpu.VMEM((PEER_WORD_ROWS * PEER_PICKS, LANES), I32)


def _table_spec(tab):
    return pl.BlockSpec(tab.shape, lambda c: (0, 0), pipeline_mode=pl.Buffered(1))


def _peer_route_up(st, hn, tab):
    t, d = hn.shape
    nk = st.shape[0]
    nblk = t // PEER_TOK
    prev = lambda c: (jnp.maximum(c - 1, 0), 0)
    return pl.pallas_call(
        _peer_route_up_kernel,
        grid=(nblk + 1,),
        in_specs=[pl.BlockSpec((nk, PEER_TOK), lambda c: (0, jnp.minimum(c, nblk - 1))),
                  pl.BlockSpec((PEER_TOK, d), prev),
                  _table_spec(tab)],
        out_specs=[pl.BlockSpec((PEER_TOK, 2 * PEER_PICKS), prev),
                   pl.BlockSpec((PEER_TOK, PEER_PICKS), lambda c: (jnp.minimum(c, nblk - 1), 0))],
        out_shape=[jax.ShapeDtypeStruct((t, 2 * PEER_PICKS), F32), jax.ShapeDtypeStruct((t, PEER_PICKS), I32)],
        scratch_shapes=[pltpu.VMEM((2, PEER_TOK, PEER_PICKS), I32), pltpu.VMEM((2, PEER_TOK, PEER_PICKS), F32),
                        pltpu.VMEM((PEER_PICKS, PEER_TOK), I32), pltpu.VMEM((PEER_PICKS, PEER_TOK), F32),
                        _tile_scratch(), _tile_scratch(), pltpu.VMEM((PEER_TOK, 2 * PEER_PICKS), F32),
                        pltpu.VMEM((PEER_TOK, d), F32)] + _idx_scratch(PEER_UP_TRIP),
        compiler_params=_cparams("arbitrary"),
    )(st, hn, tab)


def _peer_down(eidx, w, h2d, tab):
    t, d = h2d.shape
    return pl.pallas_call(
        _peer_down_kernel,
        grid=(t // PEER_TOK,),
        in_specs=[pl.BlockSpec((PEER_TOK, PEER_PICKS), lambda c: (c, 0)),
                  pl.BlockSpec((PEER_TOK, 2 * PEER_PICKS), lambda c: (c, 0)),
                  pl.BlockSpec((PEER_TOK, d), lambda c: (c, 0)),
                  _table_spec(tab)],
        out_specs=pl.BlockSpec((PEER_TOK, d), lambda c: (c, 0)),
        out_shape=jax.ShapeDtypeStruct((t, d), F32),
        scratch_shapes=[_tile_scratch(), _tile_scratch()] + _idx_scratch(PEER_DOWN_TRIP),
        compiler_params=_cparams("arbitrary"),
    )(eidx, w, h2d, tab)


def _pad_heads(w, heads, dim, dim_pad=LANES):
    lead = w.shape[:-1]
    w = w.reshape(lead + (heads, dim))
    w = jnp.pad(w, [(0, 0)] * len(lead) + [(0, 0), (0, dim_pad - dim)])
    return w.reshape(lead + (heads * dim_pad,))


def _pad_lanes(v, offset=0, width=LANES):
    return jnp.pad(v, [(0, 0)] * (v.ndim - 1) + [(offset, width - offset - v.shape[-1])])


def _rope_tabs(pos, rot_dim, offset):
    inv = 1.0 / (ROPE_THETA ** (jnp.arange(0, rot_dim, 2, dtype=F32) / rot_dim))
    ang = pos.astype(F32)[:, None] * inv[None, :]
    cos, sin = jnp.cos(ang), jnp.sin(ang)
    zero = jnp.zeros_like(sin)
    c = jnp.pad(jnp.concatenate([cos, cos], -1) - 1.0, ((0, 0), (offset, LANES - offset - rot_dim))) + 1.0
    sa = _pad_lanes(jnp.concatenate([-sin, zero], -1), offset)
    sb = _pad_lanes(jnp.concatenate([zero, sin], -1), offset)
    return c, sa, sb


def _pack_table(tab):
    e, d = tab.shape
    bits = lax.bitcast_convert_type(tab.astype(BF16), jnp.uint16).astype(jnp.uint32)
    word = bits[:, :d // 2] | (bits[:, d // 2:] << 16)
    return lax.bitcast_convert_type(word, I32).reshape(e * PEER_WORD_ROWS, LANES)


def _relayout_w_in(w_in):
    d = w_in.shape[0]
    o = 0
    seg = {}
    for name, size in (("cq", 256), ("ckv", 128), ("kpe", 32), ("qn", 512), ("kc", 128), ("vc", 128),
                       ("ks", 128), ("vs", 128), ("kw", 128), ("vw", 128), ("gate", 24)):
        seg[name] = w_in[:, o:o + size]
        o += size
    gates = [_pad_lanes(seg["gate"][:, j::3]) for j in range(3)]
    cols = [_pad_heads(seg["qn"], NSA_HEADS, NSA_DK),
            _pad_heads(seg["ks"], NSA_GROUPS, NSA_DK), _pad_heads(seg["kw"], NSA_GROUPS, NSA_DK),
            seg["cq"], seg["ckv"], _pad_lanes(seg["kpe"], MLA_NOPE),
            _pad_heads(seg["vs"], NSA_GROUPS, NSA_DK), _pad_heads(seg["vw"], NSA_GROUPS, NSA_DK),
            seg["kc"]] + gates + [seg["vc"]]
    w = jnp.concatenate(cols, axis=1)
    assert w.shape == (d, IN_COLS_PAD)
    return w.astype(BF16)


def _layer(h, mem, p):
    b, s, d = h.shape
    t = b * s
    assert s % SEL_CHUNK == 0 and N_SEL <= s // SEL_LEN <= SEL_BLOCKS_PAD and d == 1024
    pos = jnp.arange(s)

    y = _norm_matmul(h.reshape(t, d), p["mix_norm"], _relayout_w_in(p["w_in"]), ROW_TILE)
    y3 = y.reshape(b, s, IN_COLS_PAD)

    wuq = _pad_heads(p["mla_w_uq"], MLA_HEADS, MLA_QK).astype(BF16)
    wukv = p["mla_w_ukv"].reshape(MLA_KV_LORA, MLA_HEADS, MLA_NOPE + MLA_V)
    wuk = _pad_heads(wukv[..., :MLA_NOPE].reshape(MLA_KV_LORA, -1), MLA_HEADS, MLA_NOPE).astype(BF16)
    wuv = _pad_heads(wukv[..., MLA_NOPE:].reshape(MLA_KV_LORA, -1), MLA_HEADS, MLA_V).astype(BF16)
    q, k, v = _mla_prep(y3, p["mla_q_norm"][None], p["mla_kv_norm"][None], _pad_lanes(p["mla_qk_norm"]),
                        wuq, wuk, wuv, _rope_tabs(pos, MLA_ROPE, MLA_NOPE), ROW_TILE)
    o_mla = _mla_attn(q, k, v, MLA_TQ)

    tabs = _rope_tabs(pos, NSA_ROT, 0)
    qn, ks, kw, vs, vw = _nsa_prep(y3, _pad_lanes(p["nsa_q_norm"][None]), _pad_lanes(p["nsa_k_norm"][1:3]),
                                   tabs, ROW_TILE)
    nrow = s // CMP_STRIDE

    def blocks16(col):
        tkv = y3[:, :, col:col + NSA_GROUPS * NSA_DK].reshape(b, nrow, CMP_STRIDE, NSA_GROUPS, NSA_DK)
        return tkv.transpose(0, 3, 1, 2, 4).reshape(b, NSA_GROUPS, nrow, CMP_STRIDE * NSA_DK)

    pe = p["nsa_cmp_pe"].reshape(2, 2, CMP_STRIDE * NSA_DK)
    w1 = p["nsa_cmp_w1"].reshape(2, 2, CMP_STRIDE * NSA_DK, -1).astype(BF16)
    w2 = _pad_lanes(p["nsa_cmp_w2"]).astype(BF16)
    cmp_end = jnp.arange(nrow) * CMP_STRIDE + CMP_LEN - 1
    kc, vc = _compress(blocks16(COL_KC), blocks16(COL_VC), pe, w1, w2, _pad_lanes(p["nsa_k_norm"][0:1]),
                       _rope_tabs(cmp_end, NSA_ROT, 0))
    o_nsa = _nsa_attn(qn, kc, vc, ks, vs, kw, vw, y3, NSA_TQ)

    width = MLA_HEADS * MLA_V
    g_out = p["mix_out_norm"]
    gm = _pad_heads(g_out[None, :width], MLA_HEADS, MLA_V)
    gn = _pad_heads(g_out[None, width:], NSA_HEADS, NSA_DK)
    w_out = p["w_out"].reshape(2 * MLA_HEADS, MLA_V, d)
    w_out = jnp.pad(w_out, ((0, 0), (0, LANES - MLA_V), (0, 0))).reshape(2 * MLA_HEADS * LANES, d).astype(BF16)
    h = _mix_out(h, o_mla, o_nsa, gm, gn, w_out, MIX_TILE)

    kx, vx = _mem_kv(mem, p["mem_norm"][None], p["xa_wkv"].astype(BF16), p["xa_qk_norm"][1:2])
    h = _xattn(h, p["xa_norm"][None], p["xa_wq"].astype(BF16), p["xa_qk_norm"][0:1], kx, vx,
               p["xa_wo"].astype(BF16), MIX_TILE)

    keys = p["peer_keys"]
    nhp = PEER_HEADS * 2
    eye = jnp.eye(nhp, dtype=F32)
    kbd = (keys.reshape(nhp, PEER_KEYS, PEER_HALF)[:, :, None, :] * eye[:, None, :, None])
    kbd = kbd.reshape(nhp * PEER_KEYS, nhp * PEER_HALF).astype(BF16)
    h2d = h.reshape(t, d)
    hn, st = _peer_score(h2d, p["ffn_norm"][None], p["peer_wq"].astype(BF16), kbd, ROW_TILE)
    w, eidx = _peer_route_up(st, hn, _pack_table(p["peer_u"]))
    out = _peer_down(eidx, w, h2d, _pack_table(p["peer_v"]))
    return out.reshape(b, s, d)


def kernel(x, mem, mix_norm, w_in, mla_q_norm, mla_kv_norm, mla_w_uq, mla_w_ukv, mla_qk_norm, nsa_q_norm, nsa_k_norm, nsa_cmp_pe, nsa_cmp_w1, nsa_cmp_w2, mix_out_norm, w_out, xa_norm, mem_norm, xa_wq, xa_wkv, xa_qk_norm, xa_wo, ffn_norm, peer_wq, peer_keys, peer_u, peer_v):
    params = dict(mix_norm=mix_norm, w_in=w_in, mla_q_norm=mla_q_norm, mla_kv_norm=mla_kv_norm,
                  mla_w_uq=mla_w_uq, mla_w_ukv=mla_w_ukv, mla_qk_norm=mla_qk_norm, nsa_q_norm=nsa_q_norm,
                  nsa_k_norm=nsa_k_norm, nsa_cmp_pe=nsa_cmp_pe, nsa_cmp_w1=nsa_cmp_w1, nsa_cmp_w2=nsa_cmp_w2,
                  mix_out_norm=mix_out_norm, w_out=w_out, xa_norm=xa_norm, mem_norm=mem_norm, xa_wq=xa_wq,
                  xa_wkv=xa_wkv, xa_qk_norm=xa_qk_norm, xa_wo=xa_wo, ffn_norm=ffn_norm, peer_wq=peer_wq,
                  peer_keys=peer_keys, peer_u=peer_u, peer_v=peer_v)
    h = x
    for layer in range(w_in.shape[0]):
        h = _layer(h, mem, {name: val[layer] for name, val in params.items()})
    return h
```
